```python
import jax
import jax.numpy as jnp
from jax import lax
import numpy as np

D_MODEL = 2048
BATCH = 4
SEQ = 2048
DEPTH = 4
DEC_BATCH = 128
DEC_SEQ = 1
PAST_LEN = 16384
PAGE_SIZE = 128

N_EVEN = (DEPTH + 1) // 2
N_ODD = DEPTH // 2
MIX_W = D_MODEL
GLA_HEADS = 4
GLA_W = MIX_W // 2
GLA_DV = GLA_W // GLA_HEADS
GLA_KEY_W = GLA_W // 2
GLA_DK = GLA_KEY_W // GLA_HEADS
GLA_GATE_RANK = 16
GLA_GATE_TEMP = 16.0
POOL_W = MIX_W - GLA_W
POOL_WINDOWS = (2, 4, 8, 16)
POOL_GROUPS = len(POOL_WINDOWS)
POOL_GC = POOL_W // POOL_GROUPS
POOL_PAST = max(POOL_WINDOWS) - 1
EVEN_IN = 2 * GLA_KEY_W + 2 * GLA_W + GLA_GATE_RANK + POOL_W
HGRN_W = MIX_W // 2
HGRN_DK = 128
HGRN_HEADS = HGRN_W // HGRN_DK
HGRN_DV = HGRN_W // HGRN_HEADS
CONF_W = MIX_W - HGRN_W
CONV_W = 31
CONV_PAST = CONV_W - 1
ODD_IN = 4 * HGRN_W + 2 * CONF_W
D_FF = 4 * D_MODEL
LA_CHUNK = 64
EPS = 1e-6
MIN_GATE = 1e-30

kernel_name = 'hybrid_gla_pool_hgrn2_conformer_decode_step'


def split_cols(z, widths):
    out, start = [], 0
    for w in widths:
        out.append(z[..., start:start + w])
        start += w
    return out


def rms_norm(x, g):
    xf = x.astype(jnp.float32)
    y = xf * lax.rsqrt(jnp.mean(xf * xf, axis=-1, keepdims=True) + EPS)
    return (y * g.astype(jnp.float32)).astype(x.dtype)


def layer_norm(x, g, b):
    xf = x.astype(jnp.float32)
    mu = jnp.mean(xf, axis=-1, keepdims=True)
    var = jnp.mean(jnp.square(xf - mu), axis=-1, keepdims=True)
    return ((xf - mu) * lax.rsqrt(var + EPS) * g.astype(jnp.float32) + b.astype(jnp.float32)).astype(x.dtype)


def gated_linear_recurrence(q, k, v, log_a, s0):
    out_dtype = v.dtype
    B, T, H, _ = q.shape
    Dv = v.shape[-1]
    C = min(LA_CHUNK, T)
    n = -(-T // C)
    pad = n * C - T

    def blocks(a):
        a = jnp.pad(a.astype(jnp.float32), ((0, 0), (0, pad), (0, 0), (0, 0)))
        return a.reshape(B, n, C, H, a.shape[-1]).transpose(1, 0, 3, 2, 4)

    causal = jnp.tril(jnp.ones((C, C), dtype=bool))[:, :, None]

    def step(S, blk):
        qc, kc, vc, gc = blk
        b = jnp.cumsum(gc, axis=2)
        o_inter = jnp.einsum('bhtd,bhde->bhte', qc * jnp.exp(b), S)
        diff = b[:, :, :, None, :] - b[:, :, None, :, :]
        decay = jnp.where(causal, jnp.exp(jnp.where(causal, diff, 0.0)), 0.0)
        scores = jnp.einsum('bhtd,bhtsd,bhsd->bhts', qc, decay, kc)
        o_intra = jnp.einsum('bhts,bhse->bhte', scores, vc)
        b_end = b[:, :, -1:, :]
        S_new = jnp.exp(b_end[:, :, 0, :, None]) * S + jnp.einsum('bhsd,bhse->bhde', kc * jnp.exp(b_end - b), vc)
        return S_new, o_inter + o_intra

    s_fin, o = lax.scan(step, s0.astype(jnp.float32), (blocks(q), blocks(k), blocks(v), blocks(log_a)))
    o = o.transpose(1, 0, 3, 2, 4).reshape(B, n * C, H, Dv)[:, :T]
    return o.astype(out_dtype), s_fin.astype(s0.dtype)


def multiscale_pool(u, prefix, pos0, pool_w, pool_scale):
    B, T, _ = u.shape
    uf = jnp.concatenate([prefix.astype(jnp.float32), u.astype(jnp.float32)], axis=1)
    cs = jnp.pad(jnp.cumsum(uf, axis=1), ((0, 0), (1, 0), (0, 0)))
    cur = uf[:, POOL_PAST:]
    pos = jnp.arange(T, dtype=jnp.int32) + pos0
    groups = []
    for g, w in enumerate(POOL_WINDOWS):
        c0, c1 = g * POOL_GC, (g + 1) * POOL_GC
        win = cs[:, POOL_PAST + 1:, c0:c1] - cs[:, POOL_PAST + 1 - w:POOL_PAST + 1 - w + T, c0:c1]
        cnt = jnp.minimum(pos + 1, w).astype(jnp.float32)[None, :, None]
        groups.append(win / cnt - cur[:, :, c0:c1])
    d = jnp.stack(groups, axis=2)
    y = jnp.einsum('btgc,gce->btge', d, pool_w.astype(jnp.float32)).reshape(B, T, POOL_W)
    y = y * pool_scale.astype(jnp.float32)
    return y.astype(u.dtype), uf[:, -POOL_PAST:].astype(prefix.dtype)


def even_mixer(h, pos0, s_gla, s_pool, w_in, w_gate_up, b_gate, gla_norm, pool_w, pool_scale, w_out):
    B, T, _ = h.shape
    z = h @ w_in
    q, k, v, r, a_lr, u = split_cols(z, [GLA_KEY_W, GLA_KEY_W, GLA_W, GLA_W, GLA_GATE_RANK, POOL_W])
    q = q.reshape(B, T, GLA_HEADS, GLA_DK) * (GLA_DK ** -0.5)
    k = k.reshape(B, T, GLA_HEADS, GLA_DK)
    v = v.reshape(B, T, GLA_HEADS, GLA_DV)
    log_a = jax.nn.log_sigmoid((a_lr @ w_gate_up + b_gate).astype(jnp.float32)) / GLA_GATE_TEMP
    log_a = log_a.reshape(B, T, GLA_HEADS, GLA_DK)
    o, s_gla_new = gated_linear_recurrence(q, k, v, log_a, s_gla)
    o = rms_norm(o, gla_norm) * jax.nn.silu(r.reshape(B, T, GLA_HEADS, GLA_DV))
    o = o.reshape(B, T, GLA_W)
    p, s_pool_new = multiscale_pool(u, s_pool, pos0, pool_w, pool_scale)
    y = jnp.concatenate([o, p.astype(o.dtype)], axis=-1) @ w_out
    return y, s_gla_new, s_pool_new


def odd_mixer(h, s_hgrn, s_conv, lb, w_in, hgrn_norm, conv_w, conv_b, ln_g, ln_b, w_out):
    B, T, _ = h.shape
    z = h @ w_in
    q, fz, i, g, glu = split_cols(z, [HGRN_W, HGRN_W, HGRN_W, HGRN_W, 2 * CONF_W])
    fz32 = fz.astype(jnp.float32)
    f = lb + (1.0 - lb) * jax.nn.sigmoid(fz32)
    log_f = jnp.log(jnp.maximum(f, MIN_GATE))
    k = (1.0 - lb) * jax.nn.sigmoid(-fz32)
    o, s_hgrn_new = gated_linear_recurrence(
        q.reshape(B, T, HGRN_HEADS, HGRN_DK), k.reshape(B, T, HGRN_HEADS, HGRN_DK),
        i.reshape(B, T, HGRN_HEADS, HGRN_DV), log_f.reshape(B, T, HGRN_HEADS, HGRN_DK), s_hgrn)
    o = rms_norm(o, hgrn_norm) * jax.nn.silu(g.reshape(B, T, HGRN_HEADS, HGRN_DV))
    o = o.reshape(B, T, HGRN_W)
    ga, gb = split_cols(glu, [CONF_W, CONF_W])
    cv_in = ga * jax.nn.sigmoid(gb)
    cvf = jnp.concatenate([s_conv.astype(cv_in.dtype), cv_in], axis=1)
    cv = lax.conv_general_dilated(cvf, conv_w[:, None, :].astype(cvf.dtype), (1,), 'VALID',
                                  dimension_numbers=('NWC', 'WIO', 'NWC'), feature_group_count=CONF_W)
    cv = jax.nn.silu(layer_norm(cv + conv_b.astype(cv.dtype), ln_g, ln_b))
    y = jnp.concatenate([o, cv.astype(o.dtype)], axis=-1) @ w_out
    return y, s_hgrn_new, cvf[:, -CONV_PAST:].astype(s_conv.dtype)


def sq_relu_mlp(h, w_up, w_down):
    return jnp.square(jax.nn.relu(h @ w_up)) @ w_down


def trunk(x, c, pos0, st_gla, st_pool, st_hgrn, st_conv, p):
    new_gla, new_pool, new_hgrn, new_conv = [], [], [], []
    c_act = jax.nn.silu(c)
    for l in range(DEPTH):
        mod = (c_act @ p['w_ada'][l] + p['b_ada'][l])[:, None, :]
        sh1, sc1, g1, sh2, sc2, g2 = split_cols(mod, [D_MODEL] * 6)
        h = rms_norm(x, p['norm_mix'][l]) * (1.0 + sc1) + sh1
        if l % 2 == 0:
            e = l // 2
            y, sg, sp = even_mixer(h, pos0, st_gla[e], st_pool[e], p['ev_w_in'][e], p['ev_w_gate_up'][e],
                                   p['ev_b_gate'][e], p['ev_gla_norm'][e], p['ev_pool_w'][e],
                                   p['ev_pool_scale'][e], p['ev_w_out'][e])
            new_gla.append(sg)
            new_pool.append(sp)
        else:
            o = l // 2
            y, shg, scv = odd_mixer(h, st_hgrn[o], st_conv[o], p['lower_bounds'][o], p['od_w_in'][o],
                                    p['od_hgrn_norm'][o], p['od_conv_w'][o], p['od_conv_b'][o],
                                    p['od_ln_g'][o], p['od_ln_b'][o], p['od_w_out'][o])
            new_hgrn.append(shg)
            new_conv.append(scv)
        x = x + g1 * y
        h = rms_norm(x, p['norm_mlp'][l]) * (1.0 + sc2) + sh2
        x = x + g2 * sq_relu_mlp(h, p['mlp_w_up'][l], p['mlp_w_down'][l])
    y = rms_norm(x, p['norm_final'])
    return y, jnp.stack(new_gla), jnp.stack(new_pool), jnp.stack(new_hgrn), jnp.stack(new_conv)


def setup_inputs(seed: int = 0) -> dict:
    key = jax.random.key(seed)
    ks = iter(jax.random.split(key, 40))

    def nrm(shape, s):
        return jax.random.normal(next(ks), shape, jnp.float32) * s

    def gain(shape):
        return 1.0 + nrm(shape, 0.1)

    return {
        'x_prompt': nrm((BATCH, SEQ, D_MODEL), 1.0),
        'x_sample': nrm((DEC_BATCH, DEC_SEQ, D_MODEL), 1.0),
        'c_prompt': nrm((BATCH, D_MODEL), 1.0),
        'c_sample': nrm((DEC_BATCH, D_MODEL), 1.0),
        'state_gla': nrm((N_EVEN, DEC_BATCH, GLA_HEADS, GLA_DK, GLA_DV), 0.5),
        'state_pool': nrm((N_EVEN, DEC_BATCH, POOL_PAST, POOL_W), 1.0),
        'state_hgrn': nrm((N_ODD, DEC_BATCH, HGRN_HEADS, HGRN_DK, HGRN_DV), 0.5),
        'state_conv': nrm((N_ODD, DEC_BATCH, CONV_PAST, CONF_W), 0.5),
        'w_ada': nrm((DEPTH, D_MODEL, 6 * D_MODEL), 0.5 * D_MODEL ** -0.5),
        'b_ada': nrm((DEPTH, 6 * D_MODEL), 0.02),
        'norm_mix': gain((DEPTH, D_MODEL)),
        'norm_mlp': gain((DEPTH, D_MODEL)),
        'norm_final': gain((D_MODEL,)),
        'ev_w_in': nrm((N_EVEN, D_MODEL, EVEN_IN), D_MODEL ** -0.5),
        'ev_w_gate_up': nrm((N_EVEN, GLA_GATE_RANK, GLA_KEY_W), GLA_GATE_RANK ** -0.5),
        'ev_b_gate': nrm((N_EVEN, GLA_KEY_W), 0.1),
        'ev_gla_norm': gain((N_EVEN, GLA_DV)),
        'ev_pool_w': nrm((N_EVEN, POOL_GROUPS, POOL_GC, POOL_GC), POOL_GC ** -0.5),
        'ev_pool_scale': gain((N_EVEN, POOL_W)),
        'ev_w_out': nrm((N_EVEN, MIX_W, D_MODEL), MIX_W ** -0.5),
        'od_w_in': nrm((N_ODD, D_MODEL, ODD_IN), D_MODEL ** -0.5),
        'od_lb_raw': nrm((N_ODD, HGRN_W), 0.5),
        'od_hgrn_norm': gain((N_ODD, HGRN_DV)),
        'od_conv_w': nrm((N_ODD, CONV_W, CONF_W), CONV_W ** -0.5),
        'od_conv_b': nrm((N_ODD, CONF_W), 0.02),
        'od_ln_g': gain((N_ODD, CONF_W)),
        'od_ln_b': nrm((N_ODD, CONF_W), 0.02),
        'od_w_out': nrm((N_ODD, MIX_W, D_MODEL), MIX_W ** -0.5),
        'mlp_w_up': nrm((DEPTH, D_MODEL, D_FF), D_MODEL ** -0.5),
        'mlp_w_down': nrm((DEPTH, D_FF, D_MODEL), D_FF ** -0.5),
    }


def reference(x_prompt, x_sample, c_prompt, c_sample, state_gla, state_pool, state_hgrn, state_conv,
              w_ada, b_ada, norm_mix, norm_mlp, norm_final,
              ev_w_in, ev_w_gate_up, ev_b_gate, ev_gla_norm, ev_pool_w, ev_pool_scale, ev_w_out,
              od_w_in, od_lb_raw, od_hgrn_norm, od_conv_w, od_conv_b, od_ln_g, od_ln_b, od_w_out,
              mlp_w_up, mlp_w_down):
    sm = jax.nn.softmax(od_lb_raw.astype(jnp.float32), axis=0)
    lower_bounds = jnp.cumsum(sm, axis=0) - sm[0:1]
    p = dict(w_ada=w_ada, b_ada=b_ada, norm_mix=norm_mix, norm_mlp=norm_mlp, norm_final=norm_final,
             ev_w_in=ev_w_in, ev_w_gate_up=ev_w_gate_up, ev_b_gate=ev_b_gate, ev_gla_norm=ev_gla_norm,
             ev_pool_w=ev_pool_w, ev_pool_scale=ev_pool_scale, ev_w_out=ev_w_out,
             od_w_in=od_w_in, lower_bounds=lower_bounds, od_hgrn_norm=od_hgrn_norm, od_conv_w=od_conv_w,
             od_conv_b=od_conv_b, od_ln_g=od_ln_g, od_ln_b=od_ln_b, od_w_out=od_w_out,
             mlp_w_up=mlp_w_up, mlp_w_down=mlp_w_down)
    bp = x_prompt.shape[0]
    zero_gla = jnp.zeros((N_EVEN, bp, GLA_HEADS, GLA_DK, GLA_DV), state_gla.dtype)
    zero_pool = jnp.zeros((N_EVEN, bp, POOL_PAST, POOL_W), state_pool.dtype)
    zero_hgrn = jnp.zeros((N_ODD, bp, HGRN_HEADS, HGRN_DK, HGRN_DV), state_hgrn.dtype)
    zero_conv = jnp.zeros((N_ODD, bp, CONV_PAST, CONF_W), state_conv.dtype)
    y_prompt, gla_p, pool_p, hgrn_p, conv_p = trunk(x_prompt, c_prompt, 0, zero_gla, zero_pool,
                                                     zero_hgrn, zero_conv, p)
    y_sample, gla_s, pool_s, hgrn_s, conv_s = trunk(x_sample, c_sample, PAST_LEN, state_gla, state_pool,
                                                     state_hgrn, state_conv, p)
    return (y_prompt, y_sample, gla_p, pool_p, hgrn_p, conv_p, gla_s, pool_s, hgrn_s, conv_s)
```

```python
import functools

import jax
import jax.numpy as jnp
from jax import lax
from jax.experimental import pallas as pl
from jax.experimental.pallas import tpu as pltpu

F32 = jnp.float32
BF16 = jnp.bfloat16

EPS = 1e-6
MIN_GATE = 1e-30
PAST_LEN = 16384
GLA_HEADS = 4
GLA_GATE_RANK = 16
GLA_GATE_TEMP = 16.0
POOL_WINDOWS = (2, 4, 8, 16)
POOL_PAST = max(POOL_WINDOWS) - 1
CONV_W = 31
CONV_PAST = CONV_W - 1
HGRN_DK = 128
LANES = 128
SUBLANES = 8
REC_CHUNK = 128
VMEM_LIMIT = 56 * 1024 * 1024

_NT = (((1,), (1,)), ((), ()))


def _params(*sem):
    return pltpu.CompilerParams(dimension_semantics=sem, vmem_limit_bytes=VMEM_LIMIT)


def _sigmoid(x):
    return 1.0 / (1.0 + jnp.exp(-x))


def _silu(x):
    return x * _sigmoid(x)


def _log_sigmoid(x):
    return jnp.minimum(x, 0.0) - jnp.log(1.0 + jnp.exp(-jnp.abs(x)))


def _rms_mod(x, g, sc, sh):
    ms = jnp.mean(x * x, axis=-1, keepdims=True)
    return (x * lax.rsqrt(ms + EPS) * g) * (1.0 + sc) + sh


def _head_norm_gate(o, nw, r):
    ms = jnp.mean(o * o, axis=-1, keepdims=True)
    return o * lax.rsqrt(ms + EPS) * nw * _silu(r)


def _ada_kernel(c_ref, w_ref, b_ref, o_ref):
    act = _silu(c_ref[...]).astype(BF16)
    o_ref[...] = jnp.dot(act, w_ref[...].astype(BF16), preferred_element_type=F32) + b_ref[...]


def _ada(c_all, w_ada, b_ada, tn=1024):
    L, D, N = w_ada.shape
    R = c_all.shape[0]
    return pl.pallas_call(
        _ada_kernel,
        grid=(L, N // tn),
        in_specs=[pl.BlockSpec((R, D), lambda l, j: (0, 0)),
                  pl.BlockSpec((None, D, tn), lambda l, j: (l, 0, j)),
                  pl.BlockSpec((None, 1, tn), lambda l, j: (l, 0, j))],
        out_specs=pl.BlockSpec((None, R, tn), lambda l, j: (l, 0, j)),
        out_shape=jax.ShapeDtypeStruct((L, R, N), F32),
        compiler_params=_params("parallel", "parallel"),
        name="ada_mod",
    )(c_all, w_ada, b_ada.reshape(L, 1, N))


def _mod_spec(per_row, tm, D, seq_len, part):
    if per_row:
        return pl.BlockSpec((tm, D), lambda i, *_: (i, part))
    return pl.BlockSpec((None, 1, D), lambda i, *_: ((i * tm) // seq_len, 0, part))


def _in_kernel(x_ref, g_ref, sc_ref, sh_ref, w_ref, z_ref, h_scr):
    @pl.when(pl.program_id(1) == 0)
    def _():
        h_scr[...] = _rms_mod(x_ref[...], g_ref[...], sc_ref[...], sh_ref[...]).astype(BF16)

    z_ref[...] = jnp.dot(h_scr[...], w_ref[...], preferred_element_type=F32)


def _in_proj(x, norm_g, mod, w, *, tm, tn, per_row, seq_len):
    M, D = x.shape
    N = w.shape[1]
    return pl.pallas_call(
        _in_kernel,
        grid=(M // tm, N // tn),
        in_specs=[pl.BlockSpec((tm, D), lambda i, j: (i, 0)),
                  pl.BlockSpec((1, D), lambda i, j: (0, 0)),
                  _mod_spec(per_row, tm, D, seq_len, 1),
                  _mod_spec(per_row, tm, D, seq_len, 0),
                  pl.BlockSpec((D, tn), lambda i, j: (0, j))],
        out_specs=pl.BlockSpec((tm, tn), lambda i, j: (i, j)),
        out_shape=jax.ShapeDtypeStruct((M, N), F32),
        scratch_shapes=[pltpu.VMEM((tm, D), BF16)],
        compiler_params=_params("parallel", "arbitrary"),
        name="in_proj",
    )(x, norm_g, mod, mod, w)


def _out_kernel(a_ref, b_ref, wa_ref, wb_ref, x_ref, g_ref, o_ref):
    y = jnp.dot(a_ref[...], wa_ref[...], preferred_element_type=F32)
    y = y + jnp.dot(b_ref[...], wb_ref[...], preferred_element_type=F32)
    o_ref[...] = x_ref[...] + g_ref[...] * y


def _out_proj(a, b, w_out, x, mod, *, tm, per_row, seq_len):
    M, D = x.shape
    Ka, Kb = a.shape[1], b.shape[1]
    return pl.pallas_call(
        _out_kernel,
        grid=(M // tm,),
        in_specs=[pl.BlockSpec((tm, Ka), lambda i: (i, 0)),
                  pl.BlockSpec((tm, Kb), lambda i: (i, 0)),
                  pl.BlockSpec((Ka, D), lambda i: (0, 0)),
                  pl.BlockSpec((Kb, D), lambda i: (Ka // Kb, 0)),
                  pl.BlockSpec((tm, D), lambda i: (i, 0)),
                  _mod_spec(per_row, tm, D, seq_len, 2)],
        out_specs=pl.BlockSpec((tm, D), lambda i: (i, 0)),
        out_shape=jax.ShapeDtypeStruct((M, D), F32),
        compiler_params=_params("parallel"),
        name="out_proj",
    )(a, b, w_out, w_out, x, mod)


def _mlp_kernel(*refs, final_norm):
    if final_norm:
        x_ref, g_ref, sc_ref, sh_ref, gate_ref, wu_ref, wd_ref, gf_ref, o_ref, h_scr, acc_scr = refs
    else:
        x_ref, g_ref, sc_ref, sh_ref, gate_ref, wu_ref, wd_ref, o_ref, h_scr, acc_scr = refs
    f = pl.program_id(1)

    @pl.when(f == 0)
    def _():
        h_scr[...] = _rms_mod(x_ref[...], g_ref[...], sc_ref[...], sh_ref[...]).astype(BF16)
        acc_scr[...] = jnp.zeros_like(acc_scr)

    a = jnp.dot(h_scr[...], wu_ref[...], preferred_element_type=F32)
    a = jnp.square(jnp.maximum(a, 0.0)).astype(BF16)
    acc_scr[...] += jnp.dot(a, wd_ref[...], preferred_element_type=F32)

    @pl.when(f == pl.num_programs(1) - 1)
    def _():
        y = x_ref[...] + gate_ref[...] * acc_scr[...]
        if final_norm:
            ms = jnp.mean(y * y, axis=-1, keepdims=True)
            y = y * lax.rsqrt(ms + EPS) * gf_ref[...]
        o_ref[...] = y


def _mlp(x, norm_g, mod, w_up, w_down, norm_final, *, tm, tf, per_row, seq_len):
    M, D = x.shape
    FF = w_up.shape[1]
    final_norm = norm_final is not None
    in_specs = [pl.BlockSpec((tm, D), lambda i, f: (i, 0)),
                pl.BlockSpec((1, D), lambda i, f: (0, 0)),
                _mod_spec(per_row, tm, D, seq_len, 4),
                _mod_spec(per_row, tm, D, seq_len, 3),
                _mod_spec(per_row, tm, D, seq_len, 5),
                pl.BlockSpec((D, tf), lambda i, f: (0, f)),
                pl.BlockSpec((tf, D), lambda i, f: (f, 0))]
    args = [x, norm_g, mod, mod, mod, w_up, w_down]
    if final_norm:
        in_specs.append(pl.BlockSpec((1, D), lambda i, f: (0, 0)))
        args.append(norm_final)
    return pl.pallas_call(
        functools.partial(_mlp_kernel, final_norm=final_norm),
        grid=(M // tm, FF // tf),
        in_specs=in_specs,
        out_specs=pl.BlockSpec((tm, D), lambda i, f: (i, 0)),
        out_shape=jax.ShapeDtypeStruct((M, D), F32),
        scratch_shapes=[pltpu.VMEM((tm, D), BF16), pltpu.VMEM((tm, D), F32)],
        compiler_params=_params("parallel", "arbitrary"),
        name="mlp",
    )(*args)


def _gla_log_decay(alr, wg, bg):
    pre = jnp.dot(alr.astype(BF16), wg, preferred_element_type=F32) + bg
    return _log_sigmoid(pre) / GLA_GATE_TEMP


def _hgrn_lower_bound(lbraw_ref, layer):
    rows = [lbraw_ref[j:j + 1, :] for j in range(lbraw_ref.shape[0])]
    mx = functools.reduce(jnp.maximum, rows)
    es = [jnp.exp(r - mx) for r in rows]
    tot = functools.reduce(lambda a, b: a + b, es)
    sm = [e / tot for e in es]
    cum = functools.reduce(lambda a, b: a + b, sm[:layer + 1])
    return cum - sm[0]


def _hgrn_gates(fz, lb):
    f = lb + (1.0 - lb) * _sigmoid(fz)
    g = jnp.log(jnp.maximum(f, MIN_GATE))
    k = (1.0 - lb) * _sigmoid(-fz)
    return g, k


def _level_ref(b_scr, m, C, W):
    if 2 * m >= 2 * SUBLANES:
        pieces = [jnp.broadcast_to(b_scr[n * 2 * m + m - 1:n * 2 * m + m, :], (2 * m, W))
                  for n in range(C // (2 * m))]
    elif m == 4:
        pieces = [jnp.broadcast_to(b_scr[j * 8 + 3:j * 8 + 4, :], (8, W)) for j in range(C // 8)]
    else:
        sub = lax.broadcasted_iota(jnp.int32, (8, W), 0)
        pieces = [jnp.where(sub < 4,
                            jnp.broadcast_to(b_scr[j * 8 + 1:j * 8 + 2, :], (8, W)),
                            jnp.broadcast_to(b_scr[j * 8 + 5:j * 8 + 6, :], (8, W)))
                  for j in range(C // 8)]
    return jnp.concatenate(pieces, axis=0)


def _rec_kernel(*refs, mode, H, dk, dv, C, layer):
    if mode == "gla":
        q_ref, k_ref, v_ref, r_ref, alr_ref, wg_ref, bg_ref, nw_ref, o_ref, s_ref, st_scr, b_scr = refs
    else:
        q_ref, fz_ref, v_ref, r_ref, lbraw_ref, nw_ref, o_ref, s_ref, st_scr, b_scr = refs
    W = H * dk
    c = pl.program_id(1)

    @pl.when(c == 0)
    def _():
        st_scr[...] = jnp.zeros_like(st_scr)

    if mode == "gla":
        q = q_ref[...] * (dk ** -0.5)
        k = k_ref[...]
        g = _gla_log_decay(alr_ref[...], wg_ref[...], bg_ref[...])
    else:
        q = q_ref[...]
        g, k = _hgrn_gates(fz_ref[...], _hgrn_lower_bound(lbraw_ref, layer))

    row = lax.broadcasted_iota(jnp.int32, (C, C), 0)
    col = lax.broadcasted_iota(jnp.int32, (C, C), 1)
    tril = jnp.where(row >= col, 1.0, 0.0).astype(BF16)
    g1 = g.astype(BF16)
    rem = g - g1.astype(F32)
    g2 = rem.astype(BF16)
    g3 = (rem - g2.astype(F32)).astype(BF16)
    b = (jnp.dot(tril, g1, preferred_element_type=F32)
         + jnp.dot(tril, g2, preferred_element_type=F32)
         + jnp.dot(tril, g3, preferred_element_type=F32))
    b_scr[...] = b

    scores = [None] * H
    levels = []
    m = C // 2
    while m >= 1:
        levels.append(m)
        m //= 2
    for m in levels + [0]:
        if m == 0:
            qe, ke = q, k
            mask = row == col
        elif m == 1:
            qe, ke = q * jnp.exp(g), k
            mask = (row == col + 1) & ((row & 1) == 1)
        else:
            e = jnp.exp(-jnp.abs(b - _level_ref(b_scr, m, C, W)))
            qe, ke = q * e, k * e
            sh = (2 * m).bit_length() - 1
            mask = (((row >> sh) == (col >> sh))
                    & ((row & (2 * m - 1)) >= m) & ((col & (2 * m - 1)) < m))
        qe = qe.astype(BF16)
        ke = ke.astype(BF16)
        for h in range(H):
            hs = slice(h * dk, (h + 1) * dk)
            p = lax.dot_general(qe[:, hs], ke[:, hs], _NT, preferred_element_type=F32)
            scores[h] = jnp.where(mask, p, 0.0 if scores[h] is None else scores[h])

    b_end = b_scr[C - 1:C, :]
    qi = (q * jnp.exp(b)).astype(BF16)
    kend = (k * jnp.exp(b_end - b)).astype(BF16)
    a_end = jnp.exp(b_end)
    nw = nw_ref[...]
    for h in range(H):
        hs = slice(h * dk, (h + 1) * dk)
        vs = slice(h * dv, (h + 1) * dv)
        v_h = v_ref[:, vs]
        st = st_scr[h]
        o = lax.dot_general(qi[:, hs], st.astype(BF16), _NT, preferred_element_type=F32)
        o = o + jnp.dot(scores[h].astype(BF16), v_h.astype(BF16), preferred_element_type=F32)
        st_scr[h] = st * a_end[:, hs] + jnp.dot(v_h.T.astype(BF16), kend[:, hs],
                                                preferred_element_type=F32)
        o_ref[:, vs] = _head_norm_gate(o, nw, r_ref[:, vs]).astype(o_ref.dtype)

    @pl.when(c == pl.num_programs(1) - 1)
    def _():
        for h in range(H):
            s_ref[h] = st_scr[h].T


def _rec_prompt(mode, z, cols, extra, norm_w, *, H, dk, dv, layer=0):
    B, T, _ = z.shape
    C = REC_CHUNK
    W, V = H * dk, H * dv

    def zspec(width, start):
        return pl.BlockSpec((None, C, width), lambda b, c: (b, c, start // width))

    const = lambda a: pl.BlockSpec(a.shape, lambda b, c: (0,) * a.ndim)
    in_specs = [zspec(W, cols[0]), zspec(W, cols[1]), zspec(V, cols[2]), zspec(V, cols[3])]
    args = [z, z, z, z]
    if mode == "gla":
        in_specs.append(zspec(LANES, cols[4]))
        args.append(z)
    for a in extra + (norm_w,):
        in_specs.append(const(a))
        args.append(a)
    return pl.pallas_call(
        functools.partial(_rec_kernel, mode=mode, H=H, dk=dk, dv=dv, C=C, layer=layer),
        grid=(B, T // C),
        in_specs=in_specs,
        out_specs=[pl.BlockSpec((None, C, V), lambda b, c: (b, c, 0)),
                   pl.BlockSpec((None, H, dk, dv), lambda b, c: (b, 0, 0, 0))],
        out_shape=[jax.ShapeDtypeStruct((B, T, V), BF16),
                   jax.ShapeDtypeStruct((B, H, dk, dv), F32)],
        scratch_shapes=[pltpu.VMEM((H, dv, dk), F32), pltpu.VMEM((C, W), F32)],
        compiler_params=_params("parallel", "arbitrary"),
        name="rec_" + mode,
    )(*args)


def _pool_kernel(u_ref, pw_ref, ps_ref, p_ref, s_ref, buf, *, Tc, gc):
    c = pl.program_id(1)
    PB = POOL_PAST + 1

    @pl.when(c == 0)
    def _():
        buf[0:PB, :] = jnp.zeros((PB, buf.shape[1]), F32)

    buf[PB:PB + Tc, :] = u_ref[...]
    pos = c * Tc + lax.broadcasted_iota(jnp.int32, (Tc, 1), 0)
    for g, w in enumerate(POOL_WINDOWS):
        cs = slice(g * gc, (g + 1) * gc)
        cur = u_ref[:, cs]
        win = cur
        for i in range(1, w):
            win = win + buf[PB - i:PB - i + Tc, cs]
        cnt = jnp.minimum(pos + 1, w).astype(F32)
        d = win / cnt - cur
        y = jnp.dot(d.astype(BF16), pw_ref[g], preferred_element_type=F32) * ps_ref[:, cs]
        p_ref[:, cs] = y.astype(p_ref.dtype)
    buf[0:PB, :] = buf[Tc:Tc + PB, :]

    @pl.when(c == pl.num_programs(1) - 1)
    def _():
        s_ref[...] = buf[Tc + 1:Tc + PB, :]


def _pool_prompt(z, col, pool_w, pool_scale, *, Tc):
    B, T, _ = z.shape
    G, gc, _ = pool_w.shape
    PW = G * gc
    return pl.pallas_call(
        functools.partial(_pool_kernel, Tc=Tc, gc=gc),
        grid=(B, T // Tc),
        in_specs=[pl.BlockSpec((None, Tc, PW), lambda b, c: (b, c, col // PW)),
                  pl.BlockSpec((G, gc, gc), lambda b, c: (0, 0, 0)),
                  pl.BlockSpec((1, PW), lambda b, c: (0, 0))],
        out_specs=[pl.BlockSpec((None, Tc, PW), lambda b, c: (b, c, 0)),
                   pl.BlockSpec((None, POOL_PAST, PW), lambda b, c: (b, 0, 0))],
        out_shape=[jax.ShapeDtypeStruct((B, T, PW), BF16),
                   jax.ShapeDtypeStruct((B, POOL_PAST, PW), F32)],
        scratch_shapes=[pltpu.VMEM((POOL_PAST + 1 + Tc, PW), F32)],
        compiler_params=_params("parallel", "arbitrary"),
        name="pool",
    )(z, pool_w, pool_scale)


def _ln_silu(x, g, b):
    mu = jnp.mean(x, axis=-1, keepdims=True)
    xc = x - mu
    var = jnp.mean(xc * xc, axis=-1, keepdims=True)
    return _silu(xc * lax.rsqrt(var + EPS) * g + b)


def _conv_kernel(ga_ref, gb_ref, cw_ref, cb_ref, lg_ref, lb_ref, o_ref, s_ref, buf, *, Tc, rt):
    c = pl.program_id(1)
    HB = CONV_PAST + 2

    @pl.when(c == 0)
    def _():
        buf[0:HB, :] = jnp.zeros((HB, buf.shape[1]), F32)

    buf[HB:HB + Tc, :] = ga_ref[...] * _sigmoid(gb_ref[...])
    for t in range(Tc // rt):
        acc = None
        for j in range(CONV_W):
            r0 = t * rt + 2 + j
            term = buf[r0:r0 + rt, :] * cw_ref[j:j + 1, :]
            acc = term if acc is None else acc + term
        y = _ln_silu(acc + cb_ref[...], lg_ref[...], lb_ref[...])
        o_ref[t * rt:(t + 1) * rt, :] = y.astype(o_ref.dtype)
    buf[0:HB, :] = buf[Tc:Tc + HB, :]

    @pl.when(c == pl.num_programs(1) - 1)
    def _():
        s_ref[...] = buf[Tc + 2:Tc + HB, :]


def _conv_prompt(z, col_a, col_b, conv_w, conv_b, ln_g, ln_b, *, Tc):
    B, T, _ = z.shape
    CW = conv_w.shape[1]
    const = lambda a: pl.BlockSpec(a.shape, lambda b, c: (0,) * a.ndim)
    return pl.pallas_call(
        functools.partial(_conv_kernel, Tc=Tc, rt=16),
        grid=(B, T // Tc),
        in_specs=[pl.BlockSpec((None, Tc, CW), lambda b, c: (b, c, col_a // CW)),
                  pl.BlockSpec((None, Tc, CW), lambda b, c: (b, c, col_b // CW)),
                  const(conv_w), const(conv_b), const(ln_g), const(ln_b)],
        out_specs=[pl.BlockSpec((None, Tc, CW), lambda b, c: (b, c, 0)),
                   pl.BlockSpec((None, CONV_PAST, CW), lambda b, c: (b, 0, 0))],
        out_shape=[jax.ShapeDtypeStruct((B, T, CW), BF16),
                   jax.ShapeDtypeStruct((B, CONV_PAST, CW), F32)],
        scratch_shapes=[pltpu.VMEM((CONV_PAST + 2 + Tc, CW), F32)],
        compiler_params=_params("parallel", "arbitrary"),
        name="conv",
    )(z, z, conv_w, conv_b, ln_g, ln_b)


def _step_recurrence(a, k, q, v_ref, s_ref, so_ref, tile_scr, row_scr, *, H, dk, dv, bb):
    for h in range(H):
        hs = slice(h * dk, (h + 1) * dk)
        tile_scr[...] = jnp.zeros_like(tile_scr)
        tile_scr[0:bb, :] = a[:, hs]
        tile_scr[bb:2 * bb, :] = k[:, hs]
        tile_scr[2 * bb:3 * bb, :] = q[:, hs]
        cols = tile_scr[...].T
        for s in range(bb):
            a_col = cols[:, s:s + 1]
            k_col = cols[:, bb + s:bb + s + 1]
            q_col = cols[:, 2 * bb + s:2 * bb + s + 1]
            v_row = v_ref[s:s + 1, h * dv:(h + 1) * dv]
            s_new = a_col * s_ref[s, h] + k_col * v_row
            so_ref[s, h] = s_new
            row_scr[s:s + 1, h * dv:(h + 1) * dv] = jnp.sum(q_col * s_new, axis=0, keepdims=True)


def _even1_kernel(q_ref, k_ref, v_ref, r_ref, alr_ref, u_ref, sg_ref, sp_ref,
                  wg_ref, bg_ref, nw_ref, pw_ref, ps_ref,
                  o_ref, p_ref, sgo_ref, spo_ref, tile_scr, row_scr, d_scr, *, H, dk, dv, bb, gc):
    g = _gla_log_decay(alr_ref[...], wg_ref[...], bg_ref[...])
    _step_recurrence(jnp.exp(g), k_ref[...], q_ref[...] * (dk ** -0.5), v_ref, sg_ref, sgo_ref,
                     tile_scr, row_scr, H=H, dk=dk, dv=dv, bb=bb)
    nw = nw_ref[...]
    for h in range(H):
        vs = slice(h * dv, (h + 1) * dv)
        o_ref[:, vs] = _head_norm_gate(row_scr[:, vs], nw, r_ref[:, vs]).astype(o_ref.dtype)

    for s in range(bb):
        for g_i, w in enumerate(POOL_WINDOWS):
            cs = slice(g_i * gc, (g_i + 1) * gc)
            cur = u_ref[s:s + 1, cs]
            win = cur + jnp.sum(sp_ref[s, POOL_PAST - (w - 1):POOL_PAST, cs], axis=0, keepdims=True)
            cnt = float(min(PAST_LEN + 1, w))
            d_scr[s:s + 1, cs] = win / cnt - cur
        spo_ref[s, 0:POOL_PAST - 1, :] = sp_ref[s, 1:POOL_PAST, :]
        spo_ref[s, POOL_PAST - 1:POOL_PAST, :] = u_ref[s:s + 1, :]
    for g_i in range(len(POOL_WINDOWS)):
        cs = slice(g_i * gc, (g_i + 1) * gc)
        y = jnp.dot(d_scr[:, cs].astype(BF16), pw_ref[g_i], preferred_element_type=F32) * ps_ref[:, cs]
        p_ref[:, cs] = y.astype(p_ref.dtype)


def _odd1_kernel(q_ref, fz_ref, v_ref, r_ref, ga_ref, gb_ref, sh_ref, sc_ref,
                 lbraw_ref, nw_ref, cw_ref, cb_ref, lg_ref, lb_ref,
                 o_ref, cv_ref, sho_ref, sco_ref, tile_scr, row_scr, d_scr, *, H, dk, dv, bb, layer):
    g, k = _hgrn_gates(fz_ref[...], _hgrn_lower_bound(lbraw_ref, layer))
    _step_recurrence(jnp.exp(g), k, q_ref[...], v_ref, sh_ref, sho_ref,
                     tile_scr, row_scr, H=H, dk=dk, dv=dv, bb=bb)
    nw = nw_ref[...]
    for h in range(H):
        vs = slice(h * dv, (h + 1) * dv)
        o_ref[:, vs] = _head_norm_gate(row_scr[:, vs], nw, r_ref[:, vs]).astype(o_ref.dtype)

    cv_in = ga_ref[...] * _sigmoid(gb_ref[...])
    d_scr[...] = cv_in * cw_ref[CONV_PAST:CONV_W, :]
    for s in range(bb):
        hist = sc_ref[s]
        d_scr[s:s + 1, :] += jnp.sum(hist * cw_ref[0:CONV_PAST, :], axis=0, keepdims=True)
        sco_ref[s, 0:CONV_PAST - 1, :] = sc_ref[s, 1:CONV_PAST, :]
        sco_ref[s, CONV_PAST - 1:CONV_PAST, :] = cv_in[s:s + 1, :]
    y = _ln_silu(d_scr[...] + cb_ref[...], lg_ref[...], lb_ref[...])
    cv_ref[...] = y.astype(cv_ref.dtype)


def _even_sample(z, cols, s_gla, s_pool, wg, bg, norm_w, pool_w, pool_scale, *, bb):
    Bs = z.shape[0]
    _, H, dk, dv = s_gla.shape
    G, gc, _ = pool_w.shape
    W, V, PW = H * dk, H * dv, G * gc

    def zspec(width, start):
        return pl.BlockSpec((bb, width), lambda i: (i, start // width))

    const = lambda a: pl.BlockSpec(a.shape, lambda i: (0,) * a.ndim)
    consts = (wg, bg, norm_w, pool_w, pool_scale)
    return pl.pallas_call(
        functools.partial(_even1_kernel, H=H, dk=dk, dv=dv, bb=bb, gc=gc),
        grid=(Bs // bb,),
        in_specs=[zspec(W, cols[0]), zspec(W, cols[1]), zspec(V, cols[2]), zspec(V, cols[3]),
                  zspec(LANES, cols[4]), zspec(PW, cols[5]),
                  pl.BlockSpec((bb, H, dk, dv), lambda i: (i, 0, 0, 0)),
                  pl.BlockSpec((bb, POOL_PAST, PW), lambda i: (i, 0, 0))] + [const(a) for a in consts],
        out_specs=[pl.BlockSpec((bb, V), lambda i: (i, 0)),
                   pl.BlockSpec((bb, PW), lambda i: (i, 0)),
                   pl.BlockSpec((bb, H, dk, dv), lambda i: (i, 0, 0, 0)),
                   pl.BlockSpec((bb, POOL_PAST, PW), lambda i: (i, 0, 0))],
        out_shape=[jax.ShapeDtypeStruct((Bs, V), BF16), jax.ShapeDtypeStruct((Bs, PW), BF16),
                   jax.ShapeDtypeStruct(s_gla.shape, F32), jax.ShapeDtypeStruct(s_pool.shape, F32)],
        scratch_shapes=[pltpu.VMEM((LANES, dk), F32), pltpu.VMEM((bb, V), F32),
                        pltpu.VMEM((bb, PW), F32)],
        compiler_params=_params("parallel"),
        name="even_sample",
    )(z, z, z, z, z, z, s_gla, s_pool, *consts)


def _odd_sample(z, cols, s_hgrn, s_conv, lb_raw, norm_w, conv_w, conv_b, ln_g, ln_b, *, bb, layer):
    Bs = z.shape[0]
    _, H, dk, dv = s_hgrn.shape
    CW = conv_w.shape[1]
    W, V = H * dk, H * dv

    def zspec(width, start):
        return pl.BlockSpec((bb, width), lambda i: (i, start // width))

    const = lambda a: pl.BlockSpec(a.shape, lambda i: (0,) * a.ndim)
    consts = (lb_raw, norm_w, conv_w, conv_b, ln_g, ln_b)
    return pl.pallas_call(
        functools.partial(_odd1_kernel, H=H, dk=dk, dv=dv, bb=bb, layer=layer),
        grid=(Bs // bb,),
        in_specs=[zspec(W, cols[0]), zspec(W, cols[1]), zspec(V, cols[2]), zspec(V, cols[3]),
                  zspec(CW, cols[4]), zspec(CW, cols[5]),
                  pl.BlockSpec((bb, H, dk, dv), lambda i: (i, 0, 0, 0)),
                  pl.BlockSpec((bb, CONV_PAST, CW), lambda i: (i, 0, 0))] + [const(a) for a in consts],
        out_specs=[pl.BlockSpec((bb, V), lambda i: (i, 0)),
                   pl.BlockSpec((bb, CW), lambda i: (i, 0)),
                   pl.BlockSpec((bb, H, dk, dv), lambda i: (i, 0, 0, 0)),
                   pl.BlockSpec((bb, CONV_PAST, CW), lambda i: (i, 0, 0))],
        out_shape=[jax.ShapeDtypeStruct((Bs, V), BF16), jax.ShapeDtypeStruct((Bs, CW), BF16),
                   jax.ShapeDtypeStruct(s_hgrn.shape, F32), jax.ShapeDtypeStruct(s_conv.shape, F32)],
        scratch_shapes=[pltpu.VMEM((LANES, dk), F32), pltpu.VMEM((bb, V), F32),
                        pltpu.VMEM((bb, CW), F32)],
        compiler_params=_params("parallel"),
        name="odd_sample",
    )(z, z, z, z, z, z, s_hgrn, s_conv, *consts)


def _pick(m, pref):
    return pref if m % pref == 0 else m


def kernel(x_prompt, x_sample, c_prompt, c_sample, state_gla, state_pool, state_hgrn, state_conv,
           w_ada, b_ada, norm_mix, norm_mlp, norm_final,
           ev_w_in, ev_w_gate_up, ev_b_gate, ev_gla_norm, ev_pool_w, ev_pool_scale, ev_w_out,
           od_w_in, od_lb_raw, od_hgrn_norm, od_conv_w, od_conv_b, od_ln_g, od_ln_b, od_w_out,
           mlp_w_up, mlp_w_down):
    B, T, D = x_prompt.shape
    Bs = x_sample.shape[0]
    depth = w_ada.shape[0]
    gla_w = D // 2
    gla_key_w = gla_w // 2
    gla_dk = gla_key_w // GLA_HEADS
    gla_dv = gla_w // GLA_HEADS
    pool_w = D - gla_w
    hgrn_w = D // 2
    hgrn_heads = hgrn_w // HGRN_DK
    hgrn_dv = hgrn_w // hgrn_heads
    conf_w = D - hgrn_w

    n_main = 2 * gla_key_w + 2 * gla_w
    ev_pad = jnp.zeros(ev_w_in.shape[:2] + (LANES - GLA_GATE_RANK,), ev_w_in.dtype)
    ev_w = jnp.concatenate([ev_w_in[..., :n_main], ev_w_in[..., n_main + GLA_GATE_RANK:],
                            ev_w_in[..., n_main:n_main + GLA_GATE_RANK], ev_pad], axis=-1).astype(BF16)
    ev_cols = (0, gla_key_w, 2 * gla_key_w, 2 * gla_key_w + gla_w, n_main + pool_w, n_main)
    wg = jnp.concatenate([ev_w_gate_up, jnp.zeros((ev_w_gate_up.shape[0], LANES - GLA_GATE_RANK,
                                                    gla_key_w), ev_w_gate_up.dtype)], axis=1).astype(BF16)
    od_w = od_w_in.astype(BF16)
    od_cols = (0, hgrn_w, 2 * hgrn_w, 3 * hgrn_w, 4 * hgrn_w, 4 * hgrn_w + conf_w)
    ev_wo = ev_w_out.astype(BF16)
    od_wo = od_w_out.astype(BF16)
    w_up = mlp_w_up.astype(BF16)
    w_down = mlp_w_down.astype(BF16)
    pool_wb = ev_pool_w.astype(BF16)

    rows = B + Bs
    pad = (-rows) % 16
    c_all = jnp.concatenate([c_prompt, c_sample, jnp.zeros((pad, D), F32)], axis=0)
    mod = _ada(c_all, w_ada, b_ada)
    mod_p = mod[:, :B].reshape(depth, B, 1, 6 * D)
    mod_s = mod[:, B:B + Bs]

    xp = x_prompt.reshape(B * T, D)
    xs = x_sample.reshape(Bs, D)
    Mp = B * T
    tm_p = _pick(Mp, 512)
    tc = _pick(T, 128)
    bb = 8
    row = lambda a: a.reshape(1, -1)

    new = {k: [] for k in ("gla_p", "pool_p", "hgrn_p", "conv_p", "gla_s", "pool_s", "hgrn_s", "conv_s")}
    for l in range(depth):
        e = l // 2
        last = l == depth - 1
        kw_p = dict(per_row=False, seq_len=T)
        kw_s = dict(per_row=True, seq_len=1)
        if l % 2 == 0:
            n_in = ev_w.shape[-1]
            tn = n_in // 3 if (n_in // 3) % LANES == 0 else LANES
            zp = _in_proj(xp, row(norm_mix[l]), mod_p[l], ev_w[e], tm=tm_p, tn=tn, **kw_p)
            zs = _in_proj(xs, row(norm_mix[l]), mod_s[l], ev_w[e], tm=Bs, tn=tn, **kw_s)
            zp3 = zp.reshape(B, T, n_in)
            extra = (wg[e], row(ev_b_gate[e]))
            op, sg = _rec_prompt("gla", zp3, ev_cols, extra, row(ev_gla_norm[e]),
                                 H=GLA_HEADS, dk=gla_dk, dv=gla_dv)
            pp, sp = _pool_prompt(zp3, ev_cols[5], pool_wb[e], row(ev_pool_scale[e]), Tc=tc)
            os_, ps_, sgs, sps = _even_sample(zs, ev_cols, state_gla[e], state_pool[e], wg[e],
                                              row(ev_b_gate[e]), row(ev_gla_norm[e]), pool_wb[e],
                                              row(ev_pool_scale[e]), bb=bb)
            new["gla_p"].append(sg)
            new["pool_p"].append(sp)
            new["gla_s"].append(sgs)
            new["pool_s"].append(sps)
            a_p, b_p, a_s, b_s, wo = op.reshape(Mp, -1), pp.reshape(Mp, -1), os_, ps_, ev_wo[e]
        else:
            n_in = od_w.shape[-1]
            tn = _pick(n_in, 1024)
            zp = _in_proj(xp, row(norm_mix[l]), mod_p[l], od_w[e], tm=tm_p, tn=tn, **kw_p)
            zs = _in_proj(xs, row(norm_mix[l]), mod_s[l], od_w[e], tm=Bs, tn=tn, **kw_s)
            zp3 = zp.reshape(B, T, n_in)
            op, shg = _rec_prompt("hgrn", zp3, od_cols, (od_lb_raw,), row(od_hgrn_norm[e]),
                                  H=hgrn_heads, dk=HGRN_DK, dv=hgrn_dv, layer=e)
            cw = jnp.concatenate([od_conv_w[e], jnp.zeros((1, conf_w), F32)], axis=0)
            cp, scv = _conv_prompt(zp3, od_cols[4], od_cols[5], cw, row(od_conv_b[e]),
                                   row(od_ln_g[e]), row(od_ln_b[e]), Tc=tc)
            os_, cs_, shs, scs = _odd_sample(zs, od_cols, state_hgrn[e], state_conv[e], od_lb_raw,
                                             row(od_hgrn_norm[e]), cw, row(od_conv_b[e]),
                                             row(od_ln_g[e]), row(od_ln_b[e]), bb=bb, layer=e)
            new["hgrn_p"].append(shg)
            new["conv_p"].append(scv)
            new["hgrn_s"].append(shs)
            new["conv_s"].append(scs)
            a_p, b_p, a_s, b_s, wo = op.reshape(Mp, -1), cp.reshape(Mp, -1), os_, cs_, od_wo[e]
        xp = _out_proj(a_p, b_p, wo, xp, mod_p[l], tm=tm_p, **kw_p)
        xs = _out_proj(a_s, b_s, wo, xs, mod_s[l], tm=Bs, **kw_s)
        nf = row(norm_final) if last else None
        xp = _mlp(xp, row(norm_mlp[l]), mod_p[l], w_up[l], w_down[l], nf, tm=tm_p, tf=1024, **kw_p)
        xs = _mlp(xs, row(norm_mlp[l]), mod_s[l], w_up[l], w_down[l], nf, tm=Bs, tf=1024, **kw_s)

    st = lambda k: jnp.stack(new[k])
    return (xp.reshape(B, T, D), xs.reshape(Bs, 1, D),
            st("gla_p"), st("pool_p"), st("hgrn_p"), st("conv_p"),
            st("gla_s"), st("pool_s"), st("hgrn_s"), st("conv_s"))
```

```python
import functools

import jax
import jax.numpy as jnp
from jax import lax
from jax.experimental import pallas as pl
from jax.experimental.pallas import tpu as pltpu

F32 = jnp.float32
BF16 = jnp.bfloat16

EPS = 1e-6
MIN_GATE = 1e-30
PAST_LEN = 16384
GLA_HEADS = 4
GLA_GATE_RANK = 16
GLA_GATE_TEMP = 16.0
POOL_WINDOWS = (2, 4, 8, 16)
POOL_PAST = max(POOL_WINDOWS) - 1
CONV_W = 31
CONV_PAST = CONV_W - 1
HGRN_DK = 128
LANES = 128
SUBLANES = 8
REC_CHUNK = 128
CONV_HIST = 32
VMEM_LIMIT = 56 * 1024 * 1024

_NT = (((1,), (1,)), ((), ()))


def _params(*sem):
    return pltpu.CompilerParams(dimension_semantics=sem, vmem_limit_bytes=VMEM_LIMIT)


def _sigmoid(x):
    return 0.5 * jnp.tanh(0.5 * x) + 0.5


def _sigmoid_pair(x):
    t = jnp.exp(-jnp.abs(x))
    r = 1.0 / (1.0 + t)
    tr = t * r
    pos = x >= 0.0
    return jnp.where(pos, r, tr), jnp.where(pos, tr, r)


def _silu(x):
    return x * _sigmoid(x)


def _log_sigmoid(x):
    return jnp.minimum(x, 0.0) - jnp.log(1.0 + jnp.exp(-jnp.abs(x)))


def _rms_mod(x, g, sc, sh):
    ms = jnp.mean(x * x, axis=-1, keepdims=True)
    return (x * lax.rsqrt(ms + EPS) * g) * (1.0 + sc) + sh


def _head_norm_gate(o, nw, r):
    ms = jnp.mean(o * o, axis=-1, keepdims=True)
    return o * lax.rsqrt(ms + EPS) * nw * _silu(r)


def _const_spec(a):
    return pl.BlockSpec(a.shape, lambda *_: (0,) * a.ndim)


def _layer_spec(a, layer):
    return pl.BlockSpec((None,) + a.shape[1:], lambda *_: (layer,) + (0,) * (a.ndim - 1))


def _ada_kernel(c_ref, w_ref, b_ref, ms_ref, mp_ref, *, n_s, n_p):
    act = _silu(c_ref[...]).astype(BF16)
    res = jnp.dot(act, w_ref[...].astype(BF16), preferred_element_type=F32) + b_ref[...]
    ms_ref[...] = res[:n_s]
    for i in range(n_p):
        mp_ref[i] = res[n_s + i:n_s + i + 1]


def _ada(c_all, w_ada, b_ada, n_s, n_p, tn=1024):
    L, D, N = w_ada.shape
    R = c_all.shape[0]
    return pl.pallas_call(
        functools.partial(_ada_kernel, n_s=n_s, n_p=n_p),
        grid=(L, N // tn),
        in_specs=[pl.BlockSpec((R, D), lambda l, j: (0, 0)),
                  pl.BlockSpec((None, D, tn), lambda l, j: (l, 0, j)),
                  pl.BlockSpec((None, 1, tn), lambda l, j: (l, 0, j))],
        out_specs=[pl.BlockSpec((None, n_s, tn), lambda l, j: (l, 0, j)),
                   pl.BlockSpec((None, n_p, 1, tn), lambda l, j: (l, 0, 0, j))],
        out_shape=[jax.ShapeDtypeStruct((L, n_s, N), F32),
                   jax.ShapeDtypeStruct((L, n_p, 1, N), F32)],
        compiler_params=_params("parallel", "parallel"),
        name="ada_mod",
    )(c_all, w_ada, b_ada.reshape(L, 1, N))


def _mod_spec(per_row, layer, tm, D, seq_len, part):
    if per_row:
        return pl.BlockSpec((None, tm, D), lambda i, *_: (layer, i, part))
    return pl.BlockSpec((None, None, 1, D), lambda i, *_: (layer, (i * tm) // seq_len, 0, part))


def _row_chunks(tm):
    return 4 if tm % 512 == 0 else 1


def _rows(ref, rows):
    return ref[...] if ref.shape[0] == 1 else ref[rows, :]


def _in_kernel(*refs, gate_block, n_chunks):
    if gate_block is None:
        x_ref, g_ref, sc_ref, sh_ref, w_ref, z_ref, h_scr = refs
    else:
        x_ref, g_ref, sc_ref, sh_ref, w_ref, z_ref, zg_ref, h_scr = refs
    j = pl.program_id(1)
    rc = x_ref.shape[0] // n_chunks

    @pl.when(j == 0)
    def _():
        for r in range(n_chunks):
            rows = slice(r * rc, (r + 1) * rc)
            h = _rms_mod(x_ref[rows, :], g_ref[...], _rows(sc_ref, rows), _rows(sh_ref, rows)).astype(BF16)
            h_scr[rows, :] = h
            z_ref[rows, :] = jnp.dot(h, w_ref[...], preferred_element_type=F32).astype(z_ref.dtype)

    @pl.when(j != 0)
    def _():
        acc = jnp.dot(h_scr[...], w_ref[...], preferred_element_type=F32)
        z_ref[...] = acc.astype(z_ref.dtype)
        if gate_block is not None:
            @pl.when(j == gate_block)
            def _():
                zg_ref[...] = acc


def _in_proj(x, norm_g, mod, w, layer, wl, *, tm, tn, per_row, seq_len, out_dtype, gate_block=None):
    M, D = x.shape
    N = w.shape[-1]
    out_specs = [pl.BlockSpec((tm, tn), lambda i, j: (i, j))]
    out_shape = [jax.ShapeDtypeStruct((M, N), out_dtype)]
    if gate_block is not None:
        out_specs.append(pl.BlockSpec((tm, tn), lambda i, j: (i, 0)))
        out_shape.append(jax.ShapeDtypeStruct((M, tn), F32))
    return pl.pallas_call(
        functools.partial(_in_kernel, gate_block=gate_block, n_chunks=_row_chunks(tm)),
        grid=(M // tm, N // tn),
        in_specs=[pl.BlockSpec((tm, D), lambda i, j: (i, 0)),
                  _layer_spec(norm_g, layer),
                  _mod_spec(per_row, layer, tm, D, seq_len, 1),
                  _mod_spec(per_row, layer, tm, D, seq_len, 0),
                  pl.BlockSpec((None, D, tn), lambda i, j: (wl, 0, j))],
        out_specs=out_specs,
        out_shape=out_shape,
        scratch_shapes=[pltpu.VMEM((tm, D), BF16)],
        compiler_params=_params("parallel", "arbitrary"),
        name="in_proj",
    )(x, norm_g, mod, mod, w)


def _out_kernel(a_ref, b_ref, wa_ref, wb_ref, x_ref, g_ref, o_ref):
    y = jnp.dot(a_ref[...], wa_ref[...], preferred_element_type=F32)
    y = y + jnp.dot(b_ref[...], wb_ref[...], preferred_element_type=F32)
    o_ref[...] = x_ref[...] + g_ref[...] * y


def _out_proj(a, b, w_out, wl, x, mod, layer, *, tm, per_row, seq_len):
    M, D = x.shape
    Ka, Kb = a.shape[1], b.shape[1]
    return pl.pallas_call(
        _out_kernel,
        grid=(M // tm,),
        in_specs=[pl.BlockSpec((tm, Ka), lambda i: (i, 0)),
                  pl.BlockSpec((tm, Kb), lambda i: (i, 0)),
                  pl.BlockSpec((None, Ka, D), lambda i: (wl, 0, 0)),
                  pl.BlockSpec((None, Kb, D), lambda i: (wl, Ka // Kb, 0)),
                  pl.BlockSpec((tm, D), lambda i: (i, 0)),
                  _mod_spec(per_row, layer, tm, D, seq_len, 2)],
        out_specs=pl.BlockSpec((tm, D), lambda i: (i, 0)),
        out_shape=jax.ShapeDtypeStruct((M, D), F32),
        compiler_params=_params("parallel"),
        name="out_proj",
    )(a, b, w_out, w_out, x, mod)


def _mlp_kernel(*refs, final_norm, n_chunks):
    if final_norm:
        x_ref, g_ref, sc_ref, sh_ref, gate_ref, wu_ref, wd_ref, gf_ref, o_ref, h_scr, acc_scr = refs
    else:
        x_ref, g_ref, sc_ref, sh_ref, gate_ref, wu_ref, wd_ref, o_ref, h_scr, acc_scr = refs
    f = pl.program_id(1)
    last = pl.num_programs(1) - 1
    rc = x_ref.shape[0] // n_chunks

    def up_down(h):
        a = jnp.dot(h, wu_ref[...], preferred_element_type=F32)
        a = jnp.square(jnp.maximum(a, 0.0)).astype(BF16)
        return jnp.dot(a, wd_ref[...], preferred_element_type=F32)

    @pl.when(f == 0)
    def _():
        for r in range(n_chunks):
            rows = slice(r * rc, (r + 1) * rc)
            h = _rms_mod(x_ref[rows, :], g_ref[...], _rows(sc_ref, rows), _rows(sh_ref, rows)).astype(BF16)
            h_scr[rows, :] = h
            acc_scr[rows, :] = up_down(h)

    @pl.when((f != 0) & (f != last))
    def _():
        acc_scr[...] += up_down(h_scr[...])

    @pl.when(f == last)
    def _():
        for r in range(n_chunks):
            rows = slice(r * rc, (r + 1) * rc)
            y = x_ref[rows, :] + _rows(gate_ref, rows) * (acc_scr[rows, :] + up_down(h_scr[rows, :]))
            if final_norm:
                ms = jnp.mean(y * y, axis=-1, keepdims=True)
                y = y * lax.rsqrt(ms + EPS) * gf_ref[...]
            o_ref[rows, :] = y


def _mlp(x, norm_g, mod, w_up, w_down, layer, norm_final, *, tm, tf, per_row, seq_len):
    M, D = x.shape
    FF = w_up.shape[-1]
    final_norm = norm_final is not None
    in_specs = [pl.BlockSpec((tm, D), lambda i, f: (i, 0)),
                _layer_spec(norm_g, layer),
                _mod_spec(per_row, layer, tm, D, seq_len, 4),
                _mod_spec(per_row, layer, tm, D, seq_len, 3),
                _mod_spec(per_row, layer, tm, D, seq_len, 5),
                pl.BlockSpec((None, D, tf), lambda i, f: (layer, 0, f)),
                pl.BlockSpec((None, tf, D), lambda i, f: (layer, f, 0))]
    args = [x, norm_g, mod, mod, mod, w_up, w_down]
    if final_norm:
        in_specs.append(_const_spec(norm_final))
        args.append(norm_final)
    return pl.pallas_call(
        functools.partial(_mlp_kernel, final_norm=final_norm, n_chunks=_row_chunks(tm)),
        grid=(M // tm, FF // tf),
        in_specs=in_specs,
        out_specs=pl.BlockSpec((tm, D), lambda i, f: (i, 0)),
        out_shape=jax.ShapeDtypeStruct((M, D), F32),
        scratch_shapes=[pltpu.VMEM((tm, D), BF16), pltpu.VMEM((tm, D), F32)],
        compiler_params=_params("parallel", "arbitrary"),
        name="mlp",
    )(*args)


def _gla_log_decay(alr, wg, bg):
    pre = jnp.dot(alr.astype(BF16), wg, preferred_element_type=F32) + bg
    return _log_sigmoid(pre) / GLA_GATE_TEMP


def _hgrn_lower_bound(lbraw_ref, layer):
    rows = [lbraw_ref[j:j + 1, :] for j in range(lbraw_ref.shape[0])]
    mx = functools.reduce(jnp.maximum, rows)
    es = [jnp.exp(r - mx) for r in rows]
    tot = functools.reduce(lambda a, b: a + b, es)
    sm = [e / tot for e in es]
    cum = functools.reduce(lambda a, b: a + b, sm[:layer + 1])
    return cum - sm[0]


def _hgrn_gates(fz, lb):
    sig, sig_neg = _sigmoid_pair(fz)
    f = lb + (1.0 - lb) * sig
    g = jnp.log(jnp.maximum(f, MIN_GATE))
    k = (1.0 - lb) * sig_neg
    return g, k


def _level_ref(b_scr, m, C, W):
    if 2 * m >= 2 * SUBLANES:
        pieces = [jnp.broadcast_to(b_scr[n * 2 * m + m - 1:n * 2 * m + m, :], (2 * m, W))
                  for n in range(C // (2 * m))]
    elif m == 4:
        pieces = [jnp.broadcast_to(b_scr[j * 8 + 3:j * 8 + 4, :], (8, W)) for j in range(C // 8)]
    else:
        sub = lax.broadcasted_iota(jnp.int32, (8, W), 0)
        pieces = [jnp.where(sub < 4,
                            jnp.broadcast_to(b_scr[j * 8 + 1:j * 8 + 2, :], (8, W)),
                            jnp.broadcast_to(b_scr[j * 8 + 5:j * 8 + 6, :], (8, W)))
                  for j in range(C // 8)]
    return jnp.concatenate(pieces, axis=0)


def _rec_kernel(*refs, mode, H, dk, dv, C, layer):
    if mode == "gla":
        q_ref, k_ref, v_ref, r_ref, alr_ref, wg_ref, bg_ref, nw_ref, o_ref, s_ref, st_scr, b_scr = refs
    else:
        q_ref, fz_ref, v_ref, r_ref, lbraw_ref, nw_ref, o_ref, s_ref, st_scr, b_scr = refs
    W = H * dk
    c = pl.program_id(1)

    @pl.when(c == 0)
    def _():
        st_scr[...] = jnp.zeros_like(st_scr)

    if mode == "gla":
        q = q_ref[...].astype(F32) * (dk ** -0.5)
        k = k_ref[...].astype(F32)
        g = _gla_log_decay(alr_ref[...], wg_ref[...], bg_ref[...])
    else:
        q = q_ref[...].astype(F32)
        g, k = _hgrn_gates(fz_ref[...], _hgrn_lower_bound(lbraw_ref, layer))

    row = lax.broadcasted_iota(jnp.int32, (C, C), 0)
    col = lax.broadcasted_iota(jnp.int32, (C, C), 1)
    tril = jnp.where(row >= col, 1.0, 0.0).astype(BF16)
    g1 = g.astype(BF16)
    rem = g - g1.astype(F32)
    g2 = rem.astype(BF16)
    g3 = (rem - g2.astype(F32)).astype(BF16)
    b = (jnp.dot(tril, g1, preferred_element_type=F32)
         + jnp.dot(tril, g2, preferred_element_type=F32)
         + jnp.dot(tril, g3, preferred_element_type=F32))
    b_scr[...] = b

    scores = [None] * H
    levels = []
    m = C // 2
    while m >= 1:
        levels.append(m)
        m //= 2
    sub = lax.broadcasted_iota(jnp.int32, (SUBLANES, W), 0)
    for m in levels + [0]:
        if m == 0:
            qe, ke = q.astype(BF16), k.astype(BF16)
            mask = row == col
        else:
            if m >= SUBLANES:
                src = jnp.concatenate(
                    [a[n * 2 * m + o:n * 2 * m + o + m] for n in range(C // (2 * m))
                     for a, o in ((k, 0), (q, m))], axis=0)
            else:
                upper = (sub & (2 * m - 1)) >= m
                src = jnp.concatenate(
                    [jnp.where(upper, q[j * SUBLANES:(j + 1) * SUBLANES],
                               k[j * SUBLANES:(j + 1) * SUBLANES])
                     for j in range(C // SUBLANES)], axis=0)
            if m == 1:
                odd = (sub & 1) == 1
                e = jnp.concatenate(
                    [jnp.where(odd, jnp.exp(g[j * SUBLANES:(j + 1) * SUBLANES]), 1.0)
                     for j in range(C // SUBLANES)], axis=0)
            else:
                e = jnp.exp(-jnp.abs(b - _level_ref(b_scr, m, C, W)))
            qe = ke = (src * e).astype(BF16)
            sh = (2 * m).bit_length() - 1
            mask = (((row >> sh) == (col >> sh))
                    & ((row & (2 * m - 1)) >= m) & ((col & (2 * m - 1)) < m))
        for h in range(H):
            hs = slice(h * dk, (h + 1) * dk)
            p = lax.dot_general(qe[:, hs], ke[:, hs], _NT, preferred_element_type=F32)
            scores[h] = jnp.where(mask, p, 0.0 if scores[h] is None else scores[h])

    b_end = b_scr[C - 1:C, :]
    qi = (q * jnp.exp(b)).astype(BF16)
    kend = (k * jnp.exp(b_end - b)).astype(BF16)
    a_end = jnp.exp(b_end)
    nw = nw_ref[...]
    for h in range(H):
        hs = slice(h * dk, (h + 1) * dk)
        vs = slice(h * dv, (h + 1) * dv)
        v_h = v_ref[:, vs]
        st = st_scr[h]
        o = lax.dot_general(qi[:, hs], st.astype(BF16), _NT, preferred_element_type=F32)
        o = o + jnp.dot(scores[h].astype(BF16), v_h, preferred_element_type=F32)
        st_scr[h] = st * a_end[:, hs] + jnp.dot(v_h.astype(F32).T.astype(BF16), kend[:, hs],
                                                preferred_element_type=F32)
        o_ref[:, vs] = _head_norm_gate(o, nw, r_ref[:, vs].astype(F32)).astype(o_ref.dtype)

    @pl.when(c == pl.num_programs(1) - 1)
    def _():
        for h in range(H):
            s_ref[h] = st_scr[h].T


def _rec_prompt(mode, z, zg, cols, extra, norm_w, el, *, H, dk, dv):
    B, T, _ = z.shape
    C = REC_CHUNK
    W, V = H * dk, H * dv

    def zspec(width, start):
        return pl.BlockSpec((None, C, width), lambda b, c: (b, c, start // width))

    if mode == "gla":
        in_specs = [zspec(W, cols[0]), zspec(W, cols[1]), zspec(V, cols[2]), zspec(V, cols[3]),
                    zspec(LANES, cols[4])]
        args = [z, z, z, z, z]
    else:
        in_specs = [zspec(W, cols[0]), pl.BlockSpec((None, C, W), lambda b, c: (b, c, 0)),
                    zspec(V, cols[2]), zspec(V, cols[3])]
        args = [z, zg, z, z]
    for a in extra:
        in_specs.append(_layer_spec(a, el) if a.ndim == 3 else _const_spec(a))
        args.append(a)
    in_specs.append(_layer_spec(norm_w, el))
    args.append(norm_w)
    return pl.pallas_call(
        functools.partial(_rec_kernel, mode=mode, H=H, dk=dk, dv=dv, C=C, layer=el),
        grid=(B, T // C),
        in_specs=in_specs,
        out_specs=[pl.BlockSpec((None, C, V), lambda b, c: (b, c, 0)),
                   pl.BlockSpec((None, H, dk, dv), lambda b, c: (b, 0, 0, 0))],
        out_shape=[jax.ShapeDtypeStruct((B, T, V), BF16),
                   jax.ShapeDtypeStruct((B, H, dk, dv), F32)],
        scratch_shapes=[pltpu.VMEM((H, dv, dk), F32), pltpu.VMEM((C, W), F32)],
        compiler_params=_params("parallel", "arbitrary"),
        name="rec_" + mode,
    )(*args)


def _pool_kernel(u_ref, pw_ref, ps_ref, p_ref, s_ref, buf, *, Tc, gc):
    c = pl.program_id(1)
    PB = POOL_PAST + 1

    @pl.when(c == 0)
    def _():
        buf[0:PB, :] = jnp.zeros((PB, buf.shape[1]), F32)

    u = u_ref[...].astype(F32)
    buf[PB:PB + Tc, :] = u
    pos = c * Tc + lax.broadcasted_iota(jnp.int32, (Tc, 1), 0)
    for g, w in enumerate(POOL_WINDOWS):
        cs = slice(g * gc, (g + 1) * gc)
        cur = u[:, cs]
        win = cur
        for i in range(1, w):
            win = win + buf[PB - i:PB - i + Tc, cs]
        cnt = jnp.minimum(pos + 1, w).astype(F32)
        d = win / cnt - cur
        y = jnp.dot(d.astype(BF16), pw_ref[g], preferred_element_type=F32) * ps_ref[:, cs]
        p_ref[:, cs] = y.astype(p_ref.dtype)
    buf[0:PB, :] = buf[Tc:Tc + PB, :]

    @pl.when(c == pl.num_programs(1) - 1)
    def _():
        s_ref[...] = buf[Tc + 1:Tc + PB, :]


def _pool_prompt(z, col, pool_w, pool_scale, el, *, Tc):
    B, T, _ = z.shape
    _, G, gc, _ = pool_w.shape
    PW = G * gc
    return pl.pallas_call(
        functools.partial(_pool_kernel, Tc=Tc, gc=gc),
        grid=(B, T // Tc),
        in_specs=[pl.BlockSpec((None, Tc, PW), lambda b, c: (b, c, col // PW)),
                  _layer_spec(pool_w, el), _layer_spec(pool_scale, el)],
        out_specs=[pl.BlockSpec((None, Tc, PW), lambda b, c: (b, c, 0)),
                   pl.BlockSpec((None, POOL_PAST, PW), lambda b, c: (b, 0, 0))],
        out_shape=[jax.ShapeDtypeStruct((B, T, PW), BF16),
                   jax.ShapeDtypeStruct((B, POOL_PAST, PW), F32)],
        scratch_shapes=[pltpu.VMEM((POOL_PAST + 1 + Tc, PW), F32)],
        compiler_params=_params("parallel", "arbitrary"),
        name="pool",
    )(z, pool_w, pool_scale)


def _ln_silu(x, g, b):
    mu = jnp.mean(x, axis=-1, keepdims=True)
    xc = x - mu
    var = jnp.mean(xc * xc, axis=-1, keepdims=True)
    return _silu(xc * lax.rsqrt(var + EPS) * g + b)


def _conv_kernel(ga_ref, gb_ref, cw_ref, cb_ref, lg_ref, lb_ref, o_ref, s_ref, buf, ph, *, Tc, rt):
    c = pl.program_id(1)
    HB = CONV_HIST
    S = SUBLANES

    @pl.when(c == 0)
    def _():
        buf[0:HB, :] = jnp.zeros((HB, buf.shape[1]), F32)

    buf[HB:HB + Tc, :] = ga_ref[...].astype(F32) * _sigmoid(gb_ref[...].astype(F32))
    for p in range(1, S):
        ph[p - 1, S:HB + Tc, :] = buf[S - p:HB + Tc - p, :]
    n_sub = rt // S
    for t in range(Tc // rt):
        accs = [None] * n_sub
        for back in range(CONV_W):
            a, p = divmod(back, S)
            j = CONV_PAST - back
            wt = cw_ref[j * S:(j + 1) * S, :]
            for q in range(n_sub):
                r0 = HB + t * rt + q * S - a * S
                src = buf[r0:r0 + S, :] if p == 0 else ph[p - 1, r0:r0 + S, :]
                term = src * wt
                accs[q] = term if accs[q] is None else accs[q] + term
        acc = jnp.concatenate(accs, axis=0)
        y = _ln_silu(acc + cb_ref[...], lg_ref[...], lb_ref[...])
        o_ref[t * rt:(t + 1) * rt, :] = y.astype(o_ref.dtype)
    buf[0:HB, :] = buf[Tc:Tc + HB, :]

    @pl.when(c == pl.num_programs(1) - 1)
    def _():
        s_ref[...] = buf[Tc + HB - CONV_PAST:Tc + HB, :]


def _conv_prompt(z, col_a, col_b, conv_wr, conv_b, ln_g, ln_b, el, *, Tc):
    B, T, _ = z.shape
    CW = conv_wr.shape[-1]
    return pl.pallas_call(
        functools.partial(_conv_kernel, Tc=Tc, rt=32),
        grid=(B, T // Tc),
        in_specs=[pl.BlockSpec((None, Tc, CW), lambda b, c: (b, c, col_a // CW)),
                  pl.BlockSpec((None, Tc, CW), lambda b, c: (b, c, col_b // CW)),
                  _layer_spec(conv_wr, el), _layer_spec(conv_b, el),
                  _layer_spec(ln_g, el), _layer_spec(ln_b, el)],
        out_specs=[pl.BlockSpec((None, Tc, CW), lambda b, c: (b, c, 0)),
                   pl.BlockSpec((None, CONV_PAST, CW), lambda b, c: (b, 0, 0))],
        out_shape=[jax.ShapeDtypeStruct((B, T, CW), BF16),
                   jax.ShapeDtypeStruct((B, CONV_PAST, CW), F32)],
        scratch_shapes=[pltpu.VMEM((CONV_HIST + Tc, CW), F32),
                        pltpu.VMEM((SUBLANES - 1, CONV_HIST + Tc, CW), F32)],
        compiler_params=_params("parallel", "arbitrary"),
        name="conv",
    )(z, z, conv_wr, conv_b, ln_g, ln_b)


def _step_recurrence(a, k, q, v_ref, s_ref, so_ref, tile_scr, row_scr, *, H, dk, dv, bb):
    for h in range(H):
        hs = slice(h * dk, (h + 1) * dk)
        tile_scr[...] = jnp.zeros_like(tile_scr)
        tile_scr[0:bb, :] = a[:, hs]
        tile_scr[bb:2 * bb, :] = k[:, hs]
        tile_scr[2 * bb:3 * bb, :] = q[:, hs]
        cols = tile_scr[...].T
        for s in range(bb):
            a_col = cols[:, s:s + 1]
            k_col = cols[:, bb + s:bb + s + 1]
            q_col = cols[:, 2 * bb + s:2 * bb + s + 1]
            v_row = v_ref[s:s + 1, h * dv:(h + 1) * dv]
            s_new = a_col * s_ref[s, h] + k_col * v_row
            so_ref[s, h] = s_new
            row_scr[s:s + 1, h * dv:(h + 1) * dv] = jnp.sum(q_col * s_new, axis=0, keepdims=True)


def _even1_kernel(*refs, H, dk, dv, bb, gc, n_alias):
    (q_ref, k_ref, v_ref, r_ref, alr_ref, u_ref, sg_ref, sp_ref,
     wg_ref, bg_ref, nw_ref, pw_ref, ps_ref) = refs[:13]
    o_ref, p_ref, sgo_ref, spo_ref, tile_scr, row_scr, d_scr = refs[13 + n_alias:]
    g = _gla_log_decay(alr_ref[...], wg_ref[...], bg_ref[...])
    _step_recurrence(jnp.exp(g), k_ref[...], q_ref[...] * (dk ** -0.5), v_ref, sg_ref, sgo_ref,
                     tile_scr, row_scr, H=H, dk=dk, dv=dv, bb=bb)
    nw = nw_ref[...]
    for h in range(H):
        vs = slice(h * dv, (h + 1) * dv)
        o_ref[:, vs] = _head_norm_gate(row_scr[:, vs], nw, r_ref[:, vs]).astype(o_ref.dtype)

    for s in range(bb):
        for g_i, w in enumerate(POOL_WINDOWS):
            cs = slice(g_i * gc, (g_i + 1) * gc)
            cur = u_ref[s:s + 1, cs]
            win = cur + jnp.sum(sp_ref[s, POOL_PAST - (w - 1):POOL_PAST, cs], axis=0, keepdims=True)
            cnt = float(min(PAST_LEN + 1, w))
            d_scr[s:s + 1, cs] = win / cnt - cur
        spo_ref[s, 0:POOL_PAST - 1, :] = sp_ref[s, 1:POOL_PAST, :]
        spo_ref[s, POOL_PAST - 1:POOL_PAST, :] = u_ref[s:s + 1, :]
    for g_i in range(len(POOL_WINDOWS)):
        cs = slice(g_i * gc, (g_i + 1) * gc)
        y = jnp.dot(d_scr[:, cs].astype(BF16), pw_ref[g_i], preferred_element_type=F32) * ps_ref[:, cs]
        p_ref[:, cs] = y.astype(p_ref.dtype)


def _odd1_kernel(*refs, H, dk, dv, bb, layer, n_alias):
    (q_ref, fz_ref, v_ref, r_ref, ga_ref, gb_ref, sh_ref, sc_ref,
     lbraw_ref, nw_ref, cw_ref, cb_ref, lg_ref, lb_ref) = refs[:14]
    o_ref, cv_ref, sho_ref, sco_ref, tile_scr, row_scr, d_scr = refs[14 + n_alias:]
    g, k = _hgrn_gates(fz_ref[...], _hgrn_lower_bound(lbraw_ref, layer))
    _step_recurrence(jnp.exp(g), k, q_ref[...], v_ref, sh_ref, sho_ref,
                     tile_scr, row_scr, H=H, dk=dk, dv=dv, bb=bb)
    nw = nw_ref[...]
    for h in range(H):
        vs = slice(h * dv, (h + 1) * dv)
        o_ref[:, vs] = _head_norm_gate(row_scr[:, vs], nw, r_ref[:, vs]).astype(o_ref.dtype)

    cv_in = ga_ref[...] * _sigmoid(gb_ref[...])
    d_scr[...] = cv_in * cw_ref[CONV_PAST:CONV_W, :]
    for s in range(bb):
        hist = sc_ref[s]
        d_scr[s:s + 1, :] += jnp.sum(hist * cw_ref[0:CONV_PAST, :], axis=0, keepdims=True)
        sco_ref[s, 0:CONV_PAST - 1, :] = sc_ref[s, 1:CONV_PAST, :]
        sco_ref[s, CONV_PAST - 1:CONV_PAST, :] = cv_in[s:s + 1, :]
    y = _ln_silu(d_scr[...] + cb_ref[...], lg_ref[...], lb_ref[...])
    cv_ref[...] = y.astype(cv_ref.dtype)


def _sample_mixer(kernel_fn, name, z, cols, widths, states, prev, consts, el, *, bb):
    Bs = z.shape[0]
    s_mat, s_hist = states
    _, _, H, dk, dv = s_mat.shape
    hist_len, HW = s_hist.shape[2:]
    V = H * dv

    def zspec(width, start):
        return pl.BlockSpec((bb, width), lambda i: (i, start // width))

    mat_spec = pl.BlockSpec((None, bb, H, dk, dv), lambda i: (el, i, 0, 0, 0))
    hist_spec = pl.BlockSpec((None, bb, hist_len, HW), lambda i: (el, i, 0, 0))
    in_specs = [zspec(w, c) for w, c in zip(widths, cols)] + [mat_spec, hist_spec]
    args = [z] * len(cols) + [s_mat, s_hist]
    for a in consts:
        in_specs.append(_layer_spec(a, el) if a.ndim >= 3 else _const_spec(a))
        args.append(a)
    aliases = {}
    if prev is not None:
        for n, p in enumerate(prev):
            aliases[len(args)] = 2 + n
            in_specs.append(pl.BlockSpec(memory_space=pl.ANY))
            args.append(p)
    n_alias = 0 if prev is None else len(prev)
    return pl.pallas_call(
        functools.partial(kernel_fn, H=H, dk=dk, dv=dv, bb=bb, n_alias=n_alias),
        grid=(Bs // bb,),
        in_specs=in_specs,
        out_specs=[pl.BlockSpec((bb, V), lambda i: (i, 0)),
                   pl.BlockSpec((bb, HW), lambda i: (i, 0)),
                   mat_spec, hist_spec],
        out_shape=[jax.ShapeDtypeStruct((Bs, V), BF16), jax.ShapeDtypeStruct((Bs, HW), BF16),
                   jax.ShapeDtypeStruct(s_mat.shape, F32), jax.ShapeDtypeStruct(s_hist.shape, F32)],
        scratch_shapes=[pltpu.VMEM((LANES, dk), F32), pltpu.VMEM((bb, V), F32),
                        pltpu.VMEM((bb, HW), F32)],
        input_output_aliases=aliases,
        compiler_params=_params("parallel"),
        name=name,
    )(*args)


def _pick(m, pref):
    return pref if m % pref == 0 else m


def kernel(x_prompt, x_sample, c_prompt, c_sample, state_gla, state_pool, state_hgrn, state_conv,
           w_ada, b_ada, norm_mix, norm_mlp, norm_final,
           ev_w_in, ev_w_gate_up, ev_b_gate, ev_gla_norm, ev_pool_w, ev_pool_scale, ev_w_out,
           od_w_in, od_lb_raw, od_hgrn_norm, od_conv_w, od_conv_b, od_ln_g, od_ln_b, od_w_out,
           mlp_w_up, mlp_w_down):
    B, T, D = x_prompt.shape
    Bs = x_sample.shape[0]
    depth = w_ada.shape[0]
    gla_w = D // 2
    gla_key_w = gla_w // 2
    gla_dk = gla_key_w // GLA_HEADS
    gla_dv = gla_w // GLA_HEADS
    pool_w = D - gla_w
    hgrn_w = D // 2
    hgrn_heads = hgrn_w // HGRN_DK
    hgrn_dv = hgrn_w // hgrn_heads
    conf_w = D - hgrn_w

    n_main = 2 * gla_key_w + 2 * gla_w
    ev_pad = jnp.zeros(ev_w_in.shape[:2] + (LANES - GLA_GATE_RANK,), ev_w_in.dtype)
    ev_w = jnp.concatenate([ev_w_in[..., :n_main], ev_w_in[..., n_main + GLA_GATE_RANK:],
                            ev_w_in[..., n_main:n_main + GLA_GATE_RANK], ev_pad], axis=-1).astype(BF16)
    ev_cols = (0, gla_key_w, 2 * gla_key_w, 2 * gla_key_w + gla_w, n_main + pool_w, n_main)
    ev_widths = (gla_key_w, gla_key_w, gla_w, gla_w, LANES, pool_w)
    wg = jnp.concatenate([ev_w_gate_up, jnp.zeros((ev_w_gate_up.shape[0], LANES - GLA_GATE_RANK,
                                                    gla_key_w), ev_w_gate_up.dtype)], axis=1).astype(BF16)
    od_w = od_w_in.astype(BF16)
    od_cols = (0, hgrn_w, 2 * hgrn_w, 3 * hgrn_w, 4 * hgrn_w, 4 * hgrn_w + conf_w)
    od_widths = (hgrn_w, hgrn_w, hgrn_w, hgrn_w, conf_w, conf_w)
    ev_wo = ev_w_out.astype(BF16)
    od_wo = od_w_out.astype(BF16)
    w_up = mlp_w_up.astype(BF16)
    w_down = mlp_w_down.astype(BF16)
    pool_wb = ev_pool_w.astype(BF16)
    conv_wr = jnp.repeat(od_conv_w, SUBLANES, axis=1)
    conv_wp = jnp.concatenate([od_conv_w, jnp.zeros_like(od_conv_w[:, :1])], axis=1)

    lay3 = lambda a: a.reshape(a.shape[0], 1, a.shape[1])
    norm_mix3, norm_mlp3 = lay3(norm_mix), lay3(norm_mlp)
    bg3, gn3, ps3 = lay3(ev_b_gate), lay3(ev_gla_norm), lay3(ev_pool_scale)
    hn3, cb3, lg3, lb3 = lay3(od_hgrn_norm), lay3(od_conv_b), lay3(od_ln_g), lay3(od_ln_b)

    pad = (-(B + Bs)) % 16
    c_all = jnp.concatenate([c_sample, c_prompt, jnp.zeros((pad, D), F32)], axis=0)
    mod_s, mod_p = _ada(c_all, w_ada, b_ada, Bs, B)

    xp = x_prompt.reshape(B * T, D)
    xs = x_sample.reshape(Bs, D)
    Mp = B * T
    tm_in = _pick(Mp, 1024)
    tm_p = _pick(Mp, 512)
    tc = _pick(T, 128)
    bb = 8

    new = {k: [] for k in ("gla_p", "pool_p", "hgrn_p", "conv_p")}
    ev_prev = od_prev = None
    for l in range(depth):
        e = l // 2
        last = l == depth - 1
        kw_p = dict(per_row=False, seq_len=T)
        kw_s = dict(per_row=True, seq_len=1)
        if l % 2 == 0:
            n_in = ev_w.shape[-1]
            tn = n_in // 3 if (n_in // 3) % LANES == 0 else LANES
            (zp,) = _in_proj(xp, norm_mix3, mod_p, ev_w, l, e, tm=tm_in, tn=tn, out_dtype=BF16, **kw_p)
            (zs,) = _in_proj(xs, norm_mix3, mod_s, ev_w, l, e, tm=Bs, tn=tn, out_dtype=F32, **kw_s)
            zp3 = zp.reshape(B, T, n_in)
            op, sg = _rec_prompt("gla", zp3, None, ev_cols, (wg, bg3), gn3, e,
                                 H=GLA_HEADS, dk=gla_dk, dv=gla_dv)
            pp, sp = _pool_prompt(zp3, ev_cols[5], pool_wb, ps3, e, Tc=tc)
            a_s, b_s, sgs, sps = _sample_mixer(
                functools.partial(_even1_kernel, gc=pool_w // len(POOL_WINDOWS)), "even_sample",
                zs, ev_cols, ev_widths, (state_gla, state_pool), ev_prev,
                (wg, bg3, gn3, pool_wb, ps3), e, bb=bb)
            ev_prev = (sgs, sps)
            new["gla_p"].append(sg)
            new["pool_p"].append(sp)
            a_p, b_p, wo = op.reshape(Mp, -1), pp.reshape(Mp, -1), ev_wo
        else:
            n_in = od_w.shape[-1]
            tn = _pick(n_in, 1024)
            zp, zg = _in_proj(xp, norm_mix3, mod_p, od_w, l, e, tm=tm_in, tn=tn, out_dtype=BF16,
                              gate_block=od_cols[1] // tn, **kw_p)
            (zs,) = _in_proj(xs, norm_mix3, mod_s, od_w, l, e, tm=Bs, tn=tn, out_dtype=F32, **kw_s)
            zp3 = zp.reshape(B, T, n_in)
            op, shg = _rec_prompt("hgrn", zp3, zg.reshape(B, T, tn), od_cols, (od_lb_raw,), hn3, e,
                                  H=hgrn_heads, dk=HGRN_DK, dv=hgrn_dv)
            cp, scv = _conv_prompt(zp3, od_cols[4], od_cols[5], conv_wr, cb3, lg3, lb3, e, Tc=tc)
            a_s, b_s, shs, scs = _sample_mixer(
                functools.partial(_odd1_kernel, layer=e), "odd_sample",
                zs, od_cols, od_widths, (state_hgrn, state_conv), od_prev,
                (od_lb_raw, hn3, conv_wp, cb3, lg3, lb3), e, bb=bb)
            od_prev = (shs, scs)
            new["hgrn_p"].append(shg)
            new["conv_p"].append(scv)
            a_p, b_p, wo = op.reshape(Mp, -1), cp.reshape(Mp, -1), od_wo
        xp = _out_proj(a_p, b_p, wo, e, xp, mod_p, l, tm=tm_p, **kw_p)
        xs = _out_proj(a_s, b_s, wo, e, xs, mod_s, l, tm=Bs, **kw_s)
        nf = norm_final.reshape(1, D) if last else None
        xp = _mlp(xp, norm_mlp3, mod_p, w_up, w_down, l, nf, tm=tm_p, tf=1024, **kw_p)
        xs = _mlp(xs, norm_mlp3, mod_s, w_up, w_down, l, nf, tm=Bs, tf=1024, **kw_s)

    st = lambda k: jnp.stack(new[k])
    return (xp.reshape(B, T, D), xs.reshape(Bs, 1, D),
            st("gla_p"), st("pool_p"), st("hgrn_p"), st("conv_p"),
            ev_prev[0], ev_prev[1], od_prev[0], od_prev[1])
```

```python
import functools

import jax
import jax.numpy as jnp
from jax import lax
from jax.experimental import pallas as pl
from jax.experimental.pallas import tpu as pltpu

F32 = jnp.float32
BF16 = jnp.bfloat16

EPS = 1e-6
MIN_GATE = 1e-30
PAST_LEN = 16384
GLA_HEADS = 4
GLA_GATE_RANK = 16
GLA_GATE_TEMP = 16.0
POOL_WINDOWS = (2, 4, 8, 16)
POOL_PAST = max(POOL_WINDOWS) - 1
CONV_W = 31
CONV_PAST = CONV_W - 1
HGRN_DK = 128
LANES = 128
SUBLANES = 8
REC_CHUNK = 128
CONV_HIST = 32
CONV_ROWS = 32
VMEM_LIMIT = 56 * 1024 * 1024

_NT = (((1,), (1,)), ((), ()))


def _params(*sem):
    return pltpu.CompilerParams(dimension_semantics=sem, vmem_limit_bytes=VMEM_LIMIT)


def _sigmoid(x):
    return 0.5 * jnp.tanh(0.5 * x) + 0.5


def _sigmoid_pair(x):
    t = jnp.exp(-jnp.abs(x))
    r = 1.0 / (1.0 + t)
    tr = t * r
    pos = x >= 0.0
    return jnp.where(pos, r, tr), jnp.where(pos, tr, r)


def _silu(x):
    return x * _sigmoid(x)


def _log_sigmoid(x):
    return jnp.minimum(x, 0.0) - jnp.log(1.0 + jnp.exp(-jnp.abs(x)))


def _rms_mod(x, g, sc, sh):
    ms = jnp.mean(x * x, axis=-1, keepdims=True)
    return (x * lax.rsqrt(ms + EPS) * g) * (1.0 + sc) + sh


def _head_norm_gate(o, nw, r):
    ms = jnp.mean(o * o, axis=-1, keepdims=True)
    return o * lax.rsqrt(ms + EPS) * nw * _silu(r)


def _shift_lanes(x, off, width):
    if off == 0:
        return x[:, :width]
    return pltpu.roll(x, x.shape[1] - off, 1)[:, :width]


def _const_spec(a):
    return pl.BlockSpec(a.shape, lambda *_: (0,) * a.ndim)


def _layer_spec(a, layer):
    return pl.BlockSpec((None,) + a.shape[1:], lambda *_: (layer,) + (0,) * (a.ndim - 1))


def _ada_kernel(c_ref, w_ref, b_ref, ms_ref, mp_ref, *, n_s, n_p):
    act = _silu(c_ref[...]).astype(BF16)
    res = jnp.dot(act, w_ref[...].astype(BF16), preferred_element_type=F32) + b_ref[...]
    ms_ref[...] = res[:n_s]
    for i in range(n_p):
        mp_ref[i] = res[n_s + i:n_s + i + 1]


def _ada(c_all, w_ada, b_ada, n_s, n_p, tn=1024):
    L, D, N = w_ada.shape
    R = c_all.shape[0]
    return pl.pallas_call(
        functools.partial(_ada_kernel, n_s=n_s, n_p=n_p),
        grid=(L, N // tn),
        in_specs=[pl.BlockSpec((R, D), lambda l, j: (0, 0)),
                  pl.BlockSpec((None, D, tn), lambda l, j: (l, 0, j)),
                  pl.BlockSpec((None, 1, tn), lambda l, j: (l, 0, j))],
        out_specs=[pl.BlockSpec((None, n_s, tn), lambda l, j: (l, 0, j)),
                   pl.BlockSpec((None, n_p, 1, tn), lambda l, j: (l, 0, 0, j))],
        out_shape=[jax.ShapeDtypeStruct((L, n_s, N), F32),
                   jax.ShapeDtypeStruct((L, n_p, 1, N), F32)],
        compiler_params=_params("parallel", "parallel"),
        name="ada_mod",
    )(c_all, w_ada, b_ada.reshape(L, 1, N))


def _mod_spec(per_row, layer, tm, D, seq_len, part):
    if per_row:
        return pl.BlockSpec((None, tm, D), lambda i, *_: (layer, i, part))
    return pl.BlockSpec((None, None, 1, D), lambda i, *_: (layer, (i * tm) // seq_len, 0, part))


def _row_chunks(tm):
    return 4 if tm % 512 == 0 else 1


def _rows(ref, rows):
    return ref[...] if ref.shape[0] == 1 else ref[rows, :]


def _w_spec(w, wl, block, index):
    if w.ndim == 2:
        return pl.BlockSpec(block, index)
    return pl.BlockSpec((None,) + block, lambda *g: (wl,) + index(*g))


def _in_kernel(*refs, gate_block, n_chunks, cast_w):
    x_ref, g_ref, sc_ref, sh_ref, w_ref, z_ref = refs[:6]
    rest = list(refs[6:])
    zg_ref = rest.pop(0) if gate_block is not None else None
    wb_ref = rest.pop(0) if cast_w else None
    (h_scr,) = rest
    j = pl.program_id(1)
    rc = x_ref.shape[0] // n_chunks
    w = w_ref[...]
    if cast_w:
        w = w.astype(BF16)
        wb_ref[...] = w

    @pl.when(j == 0)
    def _():
        for r in range(n_chunks):
            rows = slice(r * rc, (r + 1) * rc)
            h = _rms_mod(x_ref[rows, :], g_ref[...], _rows(sc_ref, rows), _rows(sh_ref, rows)).astype(BF16)
            h_scr[rows, :] = h
            z_ref[rows, :] = jnp.dot(h, w, preferred_element_type=F32).astype(z_ref.dtype)

    @pl.when(j != 0)
    def _():
        acc = jnp.dot(h_scr[...], w, preferred_element_type=F32)
        z_ref[...] = acc.astype(z_ref.dtype)
        if gate_block is not None:
            @pl.when(j == gate_block)
            def _():
                zg_ref[...] = acc


def _in_proj(x, norm_g, mod, w, layer, wl, *, tm, tn, per_row, seq_len, out_dtype, gate_block=None):
    M, D = x.shape
    N = w.shape[-1]
    cast_w = w.dtype != BF16
    assert gate_block != 0 and N // tn >= 2
    out_specs = [pl.BlockSpec((tm, tn), lambda i, j: (i, j))]
    out_shape = [jax.ShapeDtypeStruct((M, N), out_dtype)]
    if gate_block is not None:
        out_specs.append(pl.BlockSpec((tm, tn), lambda i, j: (i, 0)))
        out_shape.append(jax.ShapeDtypeStruct((M, tn), F32))
    if cast_w:
        assert M == tm
        out_specs.append(pl.BlockSpec((D, tn), lambda i, j: (0, j)))
        out_shape.append(jax.ShapeDtypeStruct((D, N), BF16))
    return pl.pallas_call(
        functools.partial(_in_kernel, gate_block=gate_block, n_chunks=_row_chunks(tm), cast_w=cast_w),
        grid=(M // tm, N // tn),
        in_specs=[pl.BlockSpec((tm, D), lambda i, j: (i, 0)),
                  _layer_spec(norm_g, layer),
                  _mod_spec(per_row, layer, tm, D, seq_len, 1),
                  _mod_spec(per_row, layer, tm, D, seq_len, 0),
                  _w_spec(w, wl, (D, tn), lambda i, j: (0, j))],
        out_specs=out_specs,
        out_shape=out_shape,
        scratch_shapes=[pltpu.VMEM((tm, D), BF16)],
        compiler_params=_params("parallel", "arbitrary"),
        name="in_proj",
    )(x, norm_g, mod, mod, w)


def _out_kernel(a_ref, b_ref, w_ref, x_ref, g_ref, o_ref, wb_ref):
    ka = a_ref.shape[1]
    w = w_ref[...].astype(BF16)
    wb_ref[...] = w
    y = jnp.dot(a_ref[...], w[:ka], preferred_element_type=F32)
    y = y + jnp.dot(b_ref[...], w[ka:], preferred_element_type=F32)
    o_ref[...] = x_ref[...] + g_ref[...] * y


def _out_proj_sample(a, b, w_out, wl, x, mod, layer, *, tn):
    M, D = x.shape
    Ka, Kb = a.shape[1], b.shape[1]
    nj = D // tn
    return pl.pallas_call(
        _out_kernel,
        grid=(nj,),
        in_specs=[pl.BlockSpec((M, Ka), lambda j: (0, 0)),
                  pl.BlockSpec((M, Kb), lambda j: (0, 0)),
                  pl.BlockSpec((None, Ka + Kb, tn), lambda j: (wl, 0, j)),
                  pl.BlockSpec((M, tn), lambda j: (0, j)),
                  pl.BlockSpec((None, M, tn), lambda j: (layer, 0, 2 * nj + j))],
        out_specs=[pl.BlockSpec((M, tn), lambda j: (0, j)),
                   pl.BlockSpec((Ka + Kb, tn), lambda j: (0, j))],
        out_shape=[jax.ShapeDtypeStruct((M, D), F32),
                   jax.ShapeDtypeStruct((Ka + Kb, D), BF16)],
        compiler_params=_params("parallel"),
        name="out_proj_sample",
    )(a, b, w_out, x, mod)


def _mlp_kernel(*refs, final_norm, n_chunks, cast_w):
    x_ref, g_ref, sc_ref, sh_ref, gate_ref, wu_ref, wd_ref = refs[:7]
    rest = list(refs[7:])
    gf_ref = rest.pop(0) if final_norm else None
    o_ref = rest.pop(0)
    wub_ref, wdb_ref = (rest.pop(0), rest.pop(0)) if cast_w else (None, None)
    h_scr, acc_scr = rest
    f = pl.program_id(1)
    last = pl.num_programs(1) - 1
    rc = x_ref.shape[0] // n_chunks
    wu = wu_ref[...]
    wd = wd_ref[...]
    if cast_w:
        wu = wu.astype(BF16)
        wd = wd.astype(BF16)
        wub_ref[...] = wu
        wdb_ref[...] = wd

    def up_down(h):
        a = jnp.dot(h, wu, preferred_element_type=F32)
        a = jnp.square(jnp.maximum(a, 0.0)).astype(BF16)
        return jnp.dot(a, wd, preferred_element_type=F32)

    @pl.when(f == 0)
    def _():
        for r in range(n_chunks):
            rows = slice(r * rc, (r + 1) * rc)
            h = _rms_mod(x_ref[rows, :], g_ref[...], _rows(sc_ref, rows), _rows(sh_ref, rows)).astype(BF16)
            h_scr[rows, :] = h
            acc_scr[rows, :] = up_down(h)

    @pl.when((f != 0) & (f != last))
    def _():
        acc_scr[...] += up_down(h_scr[...])

    @pl.when(f == last)
    def _():
        for r in range(n_chunks):
            rows = slice(r * rc, (r + 1) * rc)
            y = x_ref[rows, :] + _rows(gate_ref, rows) * (acc_scr[rows, :] + up_down(h_scr[rows, :]))
            if final_norm:
                ms = jnp.mean(y * y, axis=-1, keepdims=True)
                y = y * lax.rsqrt(ms + EPS) * gf_ref[...]
            o_ref[rows, :] = y


def _mlp(x, norm_g, mod, w_up, w_down, layer, norm_final, *, tm, tf, per_row, seq_len):
    M, D = x.shape
    FF = w_up.shape[-1]
    final_norm = norm_final is not None
    cast_w = w_up.dtype != BF16
    assert FF // tf >= 2
    in_specs = [pl.BlockSpec((tm, D), lambda i, f: (i, 0)),
                _layer_spec(norm_g, layer),
                _mod_spec(per_row, layer, tm, D, seq_len, 4),
                _mod_spec(per_row, layer, tm, D, seq_len, 3),
                _mod_spec(per_row, layer, tm, D, seq_len, 5),
                _w_spec(w_up, layer, (D, tf), lambda i, f: (0, f)),
                _w_spec(w_down, layer, (tf, D), lambda i, f: (f, 0))]
    args = [x, norm_g, mod, mod, mod, w_up, w_down]
    if final_norm:
        in_specs.append(_const_spec(norm_final))
        args.append(norm_final)
    out_specs = [pl.BlockSpec((tm, D), lambda i, f: (i, 0))]
    out_shape = [jax.ShapeDtypeStruct((M, D), F32)]
    if cast_w:
        assert M == tm
        out_specs += [pl.BlockSpec((D, tf), lambda i, f: (0, f)), pl.BlockSpec((tf, D), lambda i, f: (f, 0))]
        out_shape += [jax.ShapeDtypeStruct((D, FF), BF16), jax.ShapeDtypeStruct((FF, D), BF16)]
    return pl.pallas_call(
        functools.partial(_mlp_kernel, final_norm=final_norm, n_chunks=_row_chunks(tm), cast_w=cast_w),
        grid=(M // tm, FF // tf),
        in_specs=in_specs,
        out_specs=out_specs,
        out_shape=out_shape,
        scratch_shapes=[pltpu.VMEM((tm, D), BF16), pltpu.VMEM((tm, D), F32)],
        compiler_params=_params("parallel", "arbitrary"),
        name="mlp",
    )(*args)


def _gla_log_decay(alr, wg, bg):
    pre = jnp.dot(alr.astype(BF16), wg, preferred_element_type=F32) + bg
    return _log_sigmoid(pre) / GLA_GATE_TEMP


def _hgrn_lower_bound(lbraw_ref, layer):
    rows = [lbraw_ref[j:j + 1, :] for j in range(lbraw_ref.shape[0])]
    mx = functools.reduce(jnp.maximum, rows)
    es = [jnp.exp(r - mx) for r in rows]
    tot = functools.reduce(lambda a, b: a + b, es)
    sm = [e / tot for e in es]
    cum = functools.reduce(lambda a, b: a + b, sm[:layer + 1])
    return cum - sm[0]


def _hgrn_gates(fz, lb):
    sig, sig_neg = _sigmoid_pair(fz)
    f = lb + (1.0 - lb) * sig
    g = jnp.log(jnp.maximum(f, MIN_GATE))
    k = (1.0 - lb) * sig_neg
    return g, k


def _level_ref(b_scr, m, C, W):
    if 2 * m >= 2 * SUBLANES:
        pieces = [jnp.broadcast_to(b_scr[n * 2 * m + m - 1:n * 2 * m + m, :], (2 * m, W))
                  for n in range(C // (2 * m))]
    elif m == 4:
        pieces = [jnp.broadcast_to(b_scr[j * 8 + 3:j * 8 + 4, :], (8, W)) for j in range(C // 8)]
    else:
        sub = lax.broadcasted_iota(jnp.int32, (8, W), 0)
        pieces = [jnp.where(sub < 4,
                            jnp.broadcast_to(b_scr[j * 8 + 1:j * 8 + 2, :], (8, W)),
                            jnp.broadcast_to(b_scr[j * 8 + 5:j * 8 + 6, :], (8, W)))
                  for j in range(C // 8)]
    return jnp.concatenate(pieces, axis=0)


def _rec_chunk(q, k, g, v_ref, r_ref, nw, o_scr, s_ref, st_scr, b_scr, *, H, dk, dv, C):
    W = H * dk
    c = pl.program_id(1)

    @pl.when(c == 0)
    def _():
        st_scr[...] = jnp.zeros_like(st_scr)

    row = lax.broadcasted_iota(jnp.int32, (C, C), 0)
    col = lax.broadcasted_iota(jnp.int32, (C, C), 1)
    tril = jnp.where(row >= col, 1.0, 0.0).astype(BF16)
    g1 = g.astype(BF16)
    rem = g - g1.astype(F32)
    g2 = rem.astype(BF16)
    g3 = (rem - g2.astype(F32)).astype(BF16)
    b = (jnp.dot(tril, g1, preferred_element_type=F32)
         + jnp.dot(tril, g2, preferred_element_type=F32)
         + jnp.dot(tril, g3, preferred_element_type=F32))
    b_scr[...] = b

    scores = [None] * H
    levels = []
    m = C // 2
    while m >= 1:
        levels.append(m)
        m //= 2
    sub = lax.broadcasted_iota(jnp.int32, (SUBLANES, W), 0)
    for m in levels + [0]:
        if m == 0:
            qe, ke = q.astype(BF16), k.astype(BF16)
            mask = row == col
        else:
            if m >= SUBLANES:
                src = jnp.concatenate(
                    [a[n * 2 * m + o:n * 2 * m + o + m] for n in range(C // (2 * m))
                     for a, o in ((k, 0), (q, m))], axis=0)
            else:
                upper = (sub & (2 * m - 1)) >= m
                src = jnp.concatenate(
                    [jnp.where(upper, q[j * SUBLANES:(j + 1) * SUBLANES],
                               k[j * SUBLANES:(j + 1) * SUBLANES])
                     for j in range(C // SUBLANES)], axis=0)
            if m == 1:
                odd = (sub & 1) == 1
                e = jnp.concatenate(
                    [jnp.where(odd, jnp.exp(g[j * SUBLANES:(j + 1) * SUBLANES]), 1.0)
                     for j in range(C // SUBLANES)], axis=0)
            else:
                e = jnp.exp(-jnp.abs(b - _level_ref(b_scr, m, C, W)))
            qe = ke = (src * e).astype(BF16)
            sh = (2 * m).bit_length() - 1
            mask = (((row >> sh) == (col >> sh))
                    & ((row & (2 * m - 1)) >= m) & ((col & (2 * m - 1)) < m))
        for h in range(H):
            hs = slice(h * dk, (h + 1) * dk)
            p = lax.dot_general(qe[:, hs], ke[:, hs], _NT, preferred_element_type=F32)
            scores[h] = jnp.where(mask, p, 0.0 if scores[h] is None else scores[h])

    b_end = b_scr[C - 1:C, :]
    qi = (q * jnp.exp(b)).astype(BF16)
    kend = (k * jnp.exp(b_end - b)).astype(BF16)
    a_end = jnp.exp(b_end)
    for h in range(H):
        hs = slice(h * dk, (h + 1) * dk)
        vs = slice(h * dv, (h + 1) * dv)
        v_h = v_ref[:, vs]
        st = st_scr[h]
        o = lax.dot_general(qi[:, hs], st.astype(BF16), _NT, preferred_element_type=F32)
        o = o + jnp.dot(scores[h].astype(BF16), v_h, preferred_element_type=F32)
        st_scr[h] = st * a_end[:, hs] + jnp.dot(v_h.astype(F32).T.astype(BF16), kend[:, hs],
                                                preferred_element_type=F32)
        o_scr[:, vs] = _head_norm_gate(o, nw, r_ref[:, vs].astype(F32)).astype(o_scr.dtype)

    @pl.when(c == pl.num_programs(1) - 1)
    def _():
        for h in range(H):
            s_ref[h] = st_scr[h].T


def _pool_chunk(u, pw_ref, ps_ref, p_scr, s_ref, buf, *, C, gc):
    c = pl.program_id(1)
    PB = POOL_PAST + 1

    @pl.when(c == 0)
    def _():
        buf[0:PB, :] = jnp.zeros((PB, buf.shape[1]), F32)

    buf[PB:PB + C, :] = u
    pos = c * C + lax.broadcasted_iota(jnp.int32, (C, 1), 0)
    for g, w in enumerate(POOL_WINDOWS):
        cs = slice(g * gc, (g + 1) * gc)
        cur = u[:, cs]
        win = cur
        for i in range(1, w):
            win = win + buf[PB - i:PB - i + C, cs]
        cnt = jnp.minimum(pos + 1, w).astype(F32)
        d = win / cnt - cur
        y = jnp.dot(d.astype(BF16), pw_ref[g], preferred_element_type=F32) * ps_ref[:, cs]
        p_scr[:, cs] = y.astype(p_scr.dtype)
    buf[0:PB, :] = buf[C:C + PB, :]

    @pl.when(c == pl.num_programs(1) - 1)
    def _():
        s_ref[...] = buf[C + 1:C + PB, :]


def _ln_silu(x, g, b):
    mu = jnp.mean(x, axis=-1, keepdims=True)
    xc = x - mu
    var = jnp.mean(xc * xc, axis=-1, keepdims=True)
    return _silu(xc * lax.rsqrt(var + EPS) * g + b)


def _conv_chunk(cv_in, cw_ref, cb_ref, lg_ref, lb_ref, p_scr, s_ref, buf, ph, *, C):
    c = pl.program_id(1)
    HB = CONV_HIST
    S = SUBLANES
    rt = CONV_ROWS

    @pl.when(c == 0)
    def _():
        buf[0:HB, :] = jnp.zeros((HB, buf.shape[1]), F32)

    buf[HB:HB + C, :] = cv_in
    for p in range(1, S):
        ph[p - 1, S:HB + C, :] = buf[S - p:HB + C - p, :]
    n_sub = rt // S
    for t in range(C // rt):
        accs = [None] * n_sub
        for back in range(CONV_W):
            a, p = divmod(back, S)
            j = CONV_PAST - back
            wt = cw_ref[j * S:(j + 1) * S, :]
            for q in range(n_sub):
                r0 = HB + t * rt + q * S - a * S
                src = buf[r0:r0 + S, :] if p == 0 else ph[p - 1, r0:r0 + S, :]
                term = src * wt
                accs[q] = term if accs[q] is None else accs[q] + term
        acc = jnp.concatenate(accs, axis=0)
        y = _ln_silu(acc + cb_ref[...], lg_ref[...], lb_ref[...])
        p_scr[t * rt:(t + 1) * rt, :] = y.astype(p_scr.dtype)
    buf[0:HB, :] = buf[C:C + HB, :]

    @pl.when(c == pl.num_programs(1) - 1)
    def _():
        s_ref[...] = buf[C + HB - CONV_PAST:C + HB, :]


def _mixer_kernel(*refs, mode, H, dk, dv, C, layer, gc, u_off):
    if mode == "gla":
        (q_ref, k_ref, v_ref, r_ref, alr_ref, u_ref, wg_ref, bg_ref, nw_ref, pw_ref, ps_ref,
         wa_ref, wb_ref, x_ref, g1_ref, xo_ref, s_ref, sh_ref,
         st_scr, b_scr, o_scr, p_scr, buf) = refs
        q = q_ref[...].astype(F32) * (dk ** -0.5)
        k = k_ref[...].astype(F32)
        g = _gla_log_decay(alr_ref[...], wg_ref[...], bg_ref[...])
    else:
        (q_ref, fz_ref, v_ref, r_ref, ga_ref, gb_ref, lbraw_ref, nw_ref, cw_ref, cb_ref, lg_ref, lb_ref,
         wa_ref, wb_ref, x_ref, g1_ref, xo_ref, s_ref, sh_ref,
         st_scr, b_scr, o_scr, p_scr, buf, ph) = refs
        q = q_ref[...].astype(F32)
        g, k = _hgrn_gates(fz_ref[...], _hgrn_lower_bound(lbraw_ref, layer))
    _rec_chunk(q, k, g, v_ref, r_ref, nw_ref[...], o_scr, s_ref, st_scr, b_scr, H=H, dk=dk, dv=dv, C=C)
    if mode == "gla":
        u = _shift_lanes(u_ref[...].astype(F32), u_off, p_scr.shape[1])
        _pool_chunk(u, pw_ref, ps_ref, p_scr, sh_ref, buf, C=C, gc=gc)
    else:
        cv_in = ga_ref[...].astype(F32) * _sigmoid(gb_ref[...].astype(F32))
        _conv_chunk(cv_in, cw_ref, cb_ref, lg_ref, lb_ref, p_scr, sh_ref, buf, ph, C=C)
    y = jnp.dot(o_scr[...], wa_ref[...], preferred_element_type=F32)
    y = y + jnp.dot(p_scr[...], wb_ref[...], preferred_element_type=F32)
    xo_ref[...] = x_ref[...] + g1_ref[...] * y


def _mixer_prompt(mode, z, zg, cols, widths, consts, w_out, x, mod, layer, el, *, H, dk, dv, hist_len, u_off=0):
    B, T, _ = z.shape
    D = x.shape[-1]
    C = REC_CHUNK
    V = H * dv
    PW = D - V
    assert T % C == 0

    def zspec(width, start):
        return pl.BlockSpec((None, C, width), lambda b, c: (b, c, start // width))

    in_specs = [zspec(w, s) for w, s in zip(widths, cols)]
    args = [z] * len(cols)
    if mode == "hgrn":
        in_specs[1] = pl.BlockSpec((None, C, widths[1]), lambda b, c: (b, c, 0))
        args[1] = zg
    for a in consts:
        in_specs.append(_layer_spec(a, el) if a.ndim >= 3 else _const_spec(a))
        args.append(a)
    in_specs += [pl.BlockSpec((V, D), lambda b, c: (0, 0)),
                 pl.BlockSpec((PW, D), lambda b, c: (V // PW, 0)),
                 pl.BlockSpec((None, C, D), lambda b, c: (b, c, 0)),
                 pl.BlockSpec((None, None, 1, D), lambda b, c: (layer, b, 0, 2))]
    args += [w_out, w_out, x, mod]
    scratch = [pltpu.VMEM((H, dv, dk), F32), pltpu.VMEM((C, H * dk), F32),
               pltpu.VMEM((C, V), BF16), pltpu.VMEM((C, PW), BF16)]
    if mode == "gla":
        scratch.append(pltpu.VMEM((POOL_PAST + 1 + C, PW), F32))
    else:
        scratch += [pltpu.VMEM((CONV_HIST + C, PW), F32),
                    pltpu.VMEM((SUBLANES - 1, CONV_HIST + C, PW), F32)]
    return pl.pallas_call(
        functools.partial(_mixer_kernel, mode=mode, H=H, dk=dk, dv=dv, C=C, layer=el,
                          gc=PW // len(POOL_WINDOWS), u_off=u_off),
        grid=(B, T // C),
        in_specs=in_specs,
        out_specs=[pl.BlockSpec((None, C, D), lambda b, c: (b, c, 0)),
                   pl.BlockSpec((None, H, dk, dv), lambda b, c: (b, 0, 0, 0)),
                   pl.BlockSpec((None, hist_len, PW), lambda b, c: (b, 0, 0))],
        out_shape=[jax.ShapeDtypeStruct((B, T, D), F32),
                   jax.ShapeDtypeStruct((B, H, dk, dv), F32),
                   jax.ShapeDtypeStruct((B, hist_len, PW), F32)],
        scratch_shapes=scratch,
        compiler_params=_params("parallel", "arbitrary"),
        name="mixer_" + mode,
    )(*args)


def _step_recurrence(a, k, q, v_ref, s_ref, so_ref, tile_scr, row_scr, *, H, dk, dv, bb):
    for h in range(H):
        hs = slice(h * dk, (h + 1) * dk)
        tile_scr[...] = jnp.zeros_like(tile_scr)
        tile_scr[0:bb, :] = a[:, hs]
        tile_scr[bb:2 * bb, :] = k[:, hs]
        tile_scr[2 * bb:3 * bb, :] = q[:, hs]
        cols = tile_scr[...].T
        for s in range(bb):
            a_col = cols[:, s:s + 1]
            k_col = cols[:, bb + s:bb + s + 1]
            q_col = cols[:, 2 * bb + s:2 * bb + s + 1]
            v_row = v_ref[s:s + 1, h * dv:(h + 1) * dv]
            s_new = a_col * s_ref[s, h] + k_col * v_row
            so_ref[s, h] = s_new
            row_scr[s:s + 1, h * dv:(h + 1) * dv] = jnp.sum(q_col * s_new, axis=0, keepdims=True)


def _even1_kernel(*refs, H, dk, dv, bb, gc, n_alias, u_off):
    (q_ref, k_ref, v_ref, r_ref, alr_ref, ub_ref, sg_ref, sp_ref,
     wg_ref, bg_ref, nw_ref, pw_ref, ps_ref) = refs[:13]
    o_ref, p_ref, sgo_ref, spo_ref, tile_scr, row_scr, d_scr, u_scr = refs[13 + n_alias:]
    g = _gla_log_decay(alr_ref[...], wg_ref[...], bg_ref[...])
    _step_recurrence(jnp.exp(g), k_ref[...], q_ref[...] * (dk ** -0.5), v_ref, sg_ref, sgo_ref,
                     tile_scr, row_scr, H=H, dk=dk, dv=dv, bb=bb)
    nw = nw_ref[...]
    for h in range(H):
        vs = slice(h * dv, (h + 1) * dv)
        o_ref[:, vs] = _head_norm_gate(row_scr[:, vs], nw, r_ref[:, vs]).astype(o_ref.dtype)

    u_scr[...] = _shift_lanes(ub_ref[...], u_off, u_scr.shape[1])
    for s in range(bb):
        for g_i, w in enumerate(POOL_WINDOWS):
            cs = slice(g_i * gc, (g_i + 1) * gc)
            cur = u_scr[s:s + 1, cs]
            win = cur + jnp.sum(sp_ref[s, POOL_PAST - (w - 1):POOL_PAST, cs], axis=0, keepdims=True)
            cnt = float(min(PAST_LEN + 1, w))
            d_scr[s:s + 1, cs] = win / cnt - cur
        spo_ref[s, 0:POOL_PAST - 1, :] = sp_ref[s, 1:POOL_PAST, :]
        spo_ref[s, POOL_PAST - 1:POOL_PAST, :] = u_scr[s:s + 1, :]
    for g_i in range(len(POOL_WINDOWS)):
        cs = slice(g_i * gc, (g_i + 1) * gc)
        y = jnp.dot(d_scr[:, cs].astype(BF16), pw_ref[g_i], preferred_element_type=F32) * ps_ref[:, cs]
        p_ref[:, cs] = y.astype(p_ref.dtype)


def _odd1_kernel(*refs, H, dk, dv, bb, layer, n_alias):
    (q_ref, fz_ref, v_ref, r_ref, ga_ref, gb_ref, sh_ref, sc_ref,
     lbraw_ref, nw_ref, cw_ref, cb_ref, lg_ref, lb_ref) = refs[:14]
    o_ref, cv_ref, sho_ref, sco_ref, tile_scr, row_scr, d_scr = refs[14 + n_alias:]
    g, k = _hgrn_gates(fz_ref[...], _hgrn_lower_bound(lbraw_ref, layer))
    _step_recurrence(jnp.exp(g), k, q_ref[...], v_ref, sh_ref, sho_ref,
                     tile_scr, row_scr, H=H, dk=dk, dv=dv, bb=bb)
    nw = nw_ref[...]
    for h in range(H):
        vs = slice(h * dv, (h + 1) * dv)
        o_ref[:, vs] = _head_norm_gate(row_scr[:, vs], nw, r_ref[:, vs]).astype(o_ref.dtype)

    cv_in = ga_ref[...] * _sigmoid(gb_ref[...])
    d_scr[...] = cv_in * cw_ref[CONV_PAST:CONV_W, :]
    for s in range(bb):
        hist = sc_ref[s]
        d_scr[s:s + 1, :] += jnp.sum(hist * cw_ref[0:CONV_PAST, :], axis=0, keepdims=True)
        sco_ref[s, 0:CONV_PAST - 1, :] = sc_ref[s, 1:CONV_PAST, :]
        sco_ref[s, CONV_PAST - 1:CONV_PAST, :] = cv_in[s:s + 1, :]
    y = _ln_silu(d_scr[...] + cb_ref[...], lg_ref[...], lb_ref[...])
    cv_ref[...] = y.astype(cv_ref.dtype)


def _sample_mixer(kernel_fn, name, z, cols, widths, states, prev, consts, el, *, bb, extra_scratch=()):
    Bs = z.shape[0]
    s_mat, s_hist = states
    _, _, H, dk, dv = s_mat.shape
    hist_len, HW = s_hist.shape[2:]
    V = H * dv

    def zspec(width, start):
        return pl.BlockSpec((bb, width), lambda i: (i, start // width))

    mat_spec = pl.BlockSpec((None, bb, H, dk, dv), lambda i: (el, i, 0, 0, 0))
    hist_spec = pl.BlockSpec((None, bb, hist_len, HW), lambda i: (el, i, 0, 0))
    in_specs = [zspec(w, c) for w, c in zip(widths, cols)] + [mat_spec, hist_spec]
    args = [z] * len(cols) + [s_mat, s_hist]
    for a in consts:
        in_specs.append(_layer_spec(a, el) if a.ndim >= 3 else _const_spec(a))
        args.append(a)
    aliases = {}
    if prev is not None:
        for n, p in enumerate(prev):
            aliases[len(args)] = 2 + n
            in_specs.append(pl.BlockSpec(memory_space=pl.ANY))
            args.append(p)
    n_alias = 0 if prev is None else len(prev)
    return pl.pallas_call(
        functools.partial(kernel_fn, H=H, dk=dk, dv=dv, bb=bb, n_alias=n_alias),
        grid=(Bs // bb,),
        in_specs=in_specs,
        out_specs=[pl.BlockSpec((bb, V), lambda i: (i, 0)),
                   pl.BlockSpec((bb, HW), lambda i: (i, 0)),
                   mat_spec, hist_spec],
        out_shape=[jax.ShapeDtypeStruct((Bs, V), BF16), jax.ShapeDtypeStruct((Bs, HW), BF16),
                   jax.ShapeDtypeStruct(s_mat.shape, F32), jax.ShapeDtypeStruct(s_hist.shape, F32)],
        scratch_shapes=[pltpu.VMEM((LANES, dk), F32), pltpu.VMEM((bb, V), F32),
                        pltpu.VMEM((bb, HW), F32)] + list(extra_scratch),
        input_output_aliases=aliases,
        compiler_params=_params("parallel"),
        name=name,
    )(*args)


def _pick(m, pref):
    return pref if m % pref == 0 else m


def kernel(x_prompt, x_sample, c_prompt, c_sample, state_gla, state_pool, state_hgrn, state_conv,
           w_ada, b_ada, norm_mix, norm_mlp, norm_final,
           ev_w_in, ev_w_gate_up, ev_b_gate, ev_gla_norm, ev_pool_w, ev_pool_scale, ev_w_out,
           od_w_in, od_lb_raw, od_hgrn_norm, od_conv_w, od_conv_b, od_ln_g, od_ln_b, od_w_out,
           mlp_w_up, mlp_w_down):
    B, T, D = x_prompt.shape
    Bs = x_sample.shape[0]
    depth = w_ada.shape[0]
    gla_w = D // 2
    gla_key_w = gla_w // 2
    gla_dk = gla_key_w // GLA_HEADS
    gla_dv = gla_w // GLA_HEADS
    pool_w = D - gla_w
    hgrn_w = D // 2
    hgrn_heads = hgrn_w // HGRN_DK
    hgrn_dv = hgrn_w // hgrn_heads
    conf_w = D - hgrn_w

    n_main = 2 * gla_key_w + 2 * gla_w
    ub_w = n_main // 2
    assert GLA_GATE_RANK + pool_w <= ub_w
    ev_w = jnp.pad(ev_w_in, ((0, 0), (0, 0), (0, n_main + ub_w - ev_w_in.shape[-1])))
    ev_cols = (0, gla_key_w, 2 * gla_key_w, 2 * gla_key_w + gla_w, n_main, n_main)
    ev_widths = (gla_key_w, gla_key_w, gla_w, gla_w, LANES, ub_w)
    wg = jnp.pad(ev_w_gate_up, ((0, 0), (0, LANES - GLA_GATE_RANK), (0, 0))).astype(BF16)
    od_cols = (0, hgrn_w, 2 * hgrn_w, 3 * hgrn_w, 4 * hgrn_w, 4 * hgrn_w + conf_w)
    od_widths = (hgrn_w, hgrn_w, hgrn_w, hgrn_w, conf_w, conf_w)
    pool_wb = ev_pool_w.astype(BF16)
    conv_wr = jnp.repeat(od_conv_w, SUBLANES, axis=1)
    conv_wp = jnp.pad(od_conv_w, ((0, 0), (0, 1), (0, 0)))

    lay3 = lambda a: a.reshape(a.shape[0], 1, a.shape[1])
    norm_mix3, norm_mlp3 = lay3(norm_mix), lay3(norm_mlp)
    bg3, gn3, ps3 = lay3(ev_b_gate), lay3(ev_gla_norm), lay3(ev_pool_scale)
    hn3, cb3, lg3, lb3 = lay3(od_hgrn_norm), lay3(od_conv_b), lay3(od_ln_g), lay3(od_ln_b)

    pad = (-(B + Bs)) % 16
    c_all = jnp.concatenate([c_sample, c_prompt, jnp.zeros((pad, D), F32)], axis=0)
    mod_s, mod_p = _ada(c_all, w_ada, b_ada, Bs, B)

    xp = x_prompt.reshape(B * T, D)
    xs = x_sample.reshape(Bs, D)
    Mp = B * T
    tm_in = _pick(Mp, 1024)
    tm_p = _pick(Mp, 512)
    bb = 8

    new = {k: [] for k in ("gla_p", "pool_p", "hgrn_p", "conv_p")}
    ev_prev = od_prev = None
    for l in range(depth):
        e = l // 2
        nf = norm_final.reshape(1, D) if l == depth - 1 else None
        kw_p = dict(per_row=False, seq_len=T)
        kw_s = dict(per_row=True, seq_len=1)
        if l % 2 == 0:
            n_in = ev_w.shape[-1]
            tn = ub_w
            zs, w_in_b = _in_proj(xs, norm_mix3, mod_s, ev_w, l, e, tm=Bs, tn=tn, out_dtype=F32, **kw_s)
            a_s, b_s, sgs, sps = _sample_mixer(
                functools.partial(_even1_kernel, gc=pool_w // len(POOL_WINDOWS), u_off=GLA_GATE_RANK),
                "even_sample", zs, ev_cols, ev_widths, (state_gla, state_pool), ev_prev,
                (wg, bg3, gn3, pool_wb, ps3), e, bb=bb, extra_scratch=[pltpu.VMEM((bb, pool_w), F32)])
            ev_prev = (sgs, sps)
            w_out = ev_w_out
        else:
            n_in = od_w_in.shape[-1]
            tn = _pick(n_in, 1024)
            zs, w_in_b = _in_proj(xs, norm_mix3, mod_s, od_w_in, l, e, tm=Bs, tn=tn, out_dtype=F32, **kw_s)
            a_s, b_s, shs, scs = _sample_mixer(
                functools.partial(_odd1_kernel, layer=e), "odd_sample",
                zs, od_cols, od_widths, (state_hgrn, state_conv), od_prev,
                (od_lb_raw, hn3, conv_wp, cb3, lg3, lb3), e, bb=bb)
            od_prev = (shs, scs)
            w_out = od_w_out
        xs, w_out_b = _out_proj_sample(a_s, b_s, w_out, e, xs, mod_s, l, tn=_pick(D, 512))
        xs, w_up_b, w_down_b = _mlp(xs, norm_mlp3, mod_s, mlp_w_up, mlp_w_down, l, nf,
                                    tm=Bs, tf=512, **kw_s)

        xp3 = xp.reshape(B, T, D)
        if l % 2 == 0:
            (zp,) = _in_proj(xp, norm_mix3, mod_p, w_in_b, l, None, tm=tm_in, tn=tn, out_dtype=BF16, **kw_p)
            xp3, sg, sp = _mixer_prompt("gla", zp.reshape(B, T, n_in), None, ev_cols, ev_widths,
                                        (wg, bg3, gn3, pool_wb, ps3), w_out_b, xp3, mod_p, l, e,
                                        H=GLA_HEADS, dk=gla_dk, dv=gla_dv, hist_len=POOL_PAST,
                                        u_off=GLA_GATE_RANK)
            new["gla_p"].append(sg)
            new["pool_p"].append(sp)
        else:
            zp, zg = _in_proj(xp, norm_mix3, mod_p, w_in_b, l, None, tm=tm_in, tn=tn, out_dtype=BF16,
                              gate_block=od_cols[1] // tn, **kw_p)
            xp3, shg, scv = _mixer_prompt("hgrn", zp.reshape(B, T, n_in), zg.reshape(B, T, tn),
                                          od_cols, od_widths, (od_lb_raw, hn3, conv_wr, cb3, lg3, lb3),
                                          w_out_b, xp3, mod_p, l, e,
                                          H=hgrn_heads, dk=HGRN_DK, dv=hgrn_dv, hist_len=CONV_PAST)
            new["hgrn_p"].append(shg)
            new["conv_p"].append(scv)
        (xp,) = _mlp(xp3.reshape(Mp, D), norm_mlp3, mod_p, w_up_b, w_down_b, l, nf,
                     tm=tm_p, tf=1024, **kw_p)

    st = lambda k: jnp.stack(new[k])
    return (xp.reshape(B, T, D), xs.reshape(Bs, 1, D),
            st("gla_p"), st("pool_p"), st("hgrn_p"), st("conv_p"),
            ev_prev[0], ev_prev[1], od_prev[0], od_prev[1])
```

```python
import functools

import jax
import jax.numpy as jnp
from jax import lax
from jax.experimental import pallas as pl
from jax.experimental.pallas import tpu as pltpu

F32 = jnp.float32
BF16 = jnp.bfloat16

EPS = 1e-6
MIN_GATE = 1e-30
PAST_LEN = 16384
GLA_HEADS = 4
GLA_GATE_RANK = 16
GLA_GATE_TEMP = 16.0
POOL_WINDOWS = (2, 4, 8, 16)
POOL_PAST = max(POOL_WINDOWS) - 1
CONV_W = 31
CONV_PAST = CONV_W - 1
HGRN_DK = 128
LANES = 128
SUBLANES = 8
REC_CHUNK = 128
CONV_HIST = 32
CONV_ROWS = 32
VMEM_LIMIT = 56 * 1024 * 1024

_NT = (((1,), (1,)), ((), ()))


def _params(*sem):
    return pltpu.CompilerParams(dimension_semantics=sem, vmem_limit_bytes=VMEM_LIMIT)


def _sigmoid(x):
    return 0.5 * jnp.tanh(0.5 * x) + 0.5


def _sigmoid_pair(x):
    t = jnp.exp(-jnp.abs(x))
    r = 1.0 / (1.0 + t)
    tr = t * r
    pos = x >= 0.0
    return jnp.where(pos, r, tr), jnp.where(pos, tr, r)


def _silu(x):
    return x * _sigmoid(x)


def _log_sigmoid(x):
    return jnp.minimum(x, 0.0) - jnp.log(1.0 + jnp.exp(-jnp.abs(x)))


def _rms_mod(x, g, sc, sh):
    ms = jnp.mean(x * x, axis=-1, keepdims=True)
    return (x * lax.rsqrt(ms + EPS) * g) * (1.0 + sc) + sh


def _head_norm_gate(o, nw, r):
    ms = jnp.mean(o * o, axis=-1, keepdims=True)
    return o * lax.rsqrt(ms + EPS) * nw * _silu(r)


def _shift_lanes(x, off, width):
    if off == 0:
        return x[:, :width]
    return pltpu.roll(x, x.shape[1] - off, 1)[:, :width]


def _const_spec(a):
    return pl.BlockSpec(a.shape, lambda *_: (0,) * a.ndim)


def _layer_spec(a, layer):
    return pl.BlockSpec((None,) + a.shape[1:], lambda *_: (layer,) + (0,) * (a.ndim - 1))


def _ada_kernel(c_ref, w_ref, b_ref, ms_ref, mp_ref, *, n_s, n_p):
    act = _silu(c_ref[...]).astype(BF16)
    res = jnp.dot(act, w_ref[...].astype(BF16), preferred_element_type=F32) + b_ref[...]
    ms_ref[...] = res[:n_s]
    for i in range(n_p):
        mp_ref[i] = res[n_s + i:n_s + i + 1]


def _ada(c_all, w_ada, b_ada, n_s, n_p, tn=1024):
    L, D, N = w_ada.shape
    R = c_all.shape[0]
    return pl.pallas_call(
        functools.partial(_ada_kernel, n_s=n_s, n_p=n_p),
        grid=(L, N // tn),
        in_specs=[pl.BlockSpec((R, D), lambda l, j: (0, 0)),
                  pl.BlockSpec((None, D, tn), lambda l, j: (l, 0, j)),
                  pl.BlockSpec((None, 1, tn), lambda l, j: (l, 0, j))],
        out_specs=[pl.BlockSpec((None, n_s, tn), lambda l, j: (l, 0, j)),
                   pl.BlockSpec((None, n_p, 1, tn), lambda l, j: (l, 0, 0, j))],
        out_shape=[jax.ShapeDtypeStruct((L, n_s, N), F32),
                   jax.ShapeDtypeStruct((L, n_p, 1, N), F32)],
        compiler_params=_params("parallel", "parallel"),
        name="ada_mod",
    )(c_all, w_ada, b_ada.reshape(L, 1, N))


def _mod_spec(per_row, layer, tm, D, seq_len, part):
    if per_row:
        return pl.BlockSpec((None, tm, D), lambda i, *_: (layer, i, part))
    return pl.BlockSpec((None, None, 1, D), lambda i, *_: (layer, (i * tm) // seq_len, 0, part))


def _row_chunks(tm):
    return 4 if tm % 512 == 0 else 1


def _rows(ref, rows):
    return ref[...] if ref.shape[0] == 1 else ref[rows, :]


def _w_spec(w, wl, block, index):
    if w.ndim == 2:
        return pl.BlockSpec(block, index)
    return pl.BlockSpec((None,) + block, lambda *g: (wl,) + index(*g))


def _in_kernel(*refs, gate_block, n_chunks, cast_w):
    x_ref, g_ref, sc_ref, sh_ref, w_ref, z_ref = refs[:6]
    rest = list(refs[6:])
    zg_ref = rest.pop(0) if gate_block is not None else None
    wb_ref = rest.pop(0) if cast_w else None
    (h_scr,) = rest
    j = pl.program_id(1)
    rc = x_ref.shape[0] // n_chunks
    if cast_w:
        wb_ref[...] = w_ref[...].astype(BF16)
        w_ref = wb_ref

    @pl.when(j == 0)
    def _():
        for r in range(n_chunks):
            rows = slice(r * rc, (r + 1) * rc)
            h = _rms_mod(x_ref[rows, :], g_ref[...], _rows(sc_ref, rows), _rows(sh_ref, rows)).astype(BF16)
            h_scr[rows, :] = h
            z_ref[rows, :] = jnp.dot(h, w_ref[...], preferred_element_type=F32).astype(z_ref.dtype)

    @pl.when(j != 0)
    def _():
        acc = jnp.dot(h_scr[...], w_ref[...], preferred_element_type=F32)
        z_ref[...] = acc.astype(z_ref.dtype)
        if gate_block is not None:
            @pl.when(j == gate_block)
            def _():
                zg_ref[...] = acc


def _in_proj(x, norm_g, mod, w, layer, wl, *, tm, tn, per_row, seq_len, out_dtype, gate_block=None):
    M, D = x.shape
    N = w.shape[-1]
    cast_w = w.dtype != BF16
    assert gate_block != 0 and N // tn >= 2
    out_specs = [pl.BlockSpec((tm, tn), lambda i, j: (i, j))]
    out_shape = [jax.ShapeDtypeStruct((M, N), out_dtype)]
    if gate_block is not None:
        out_specs.append(pl.BlockSpec((tm, tn), lambda i, j: (i, 0)))
        out_shape.append(jax.ShapeDtypeStruct((M, tn), F32))
    if cast_w:
        assert M == tm
        out_specs.append(pl.BlockSpec((D, tn), lambda i, j: (0, j)))
        out_shape.append(jax.ShapeDtypeStruct((D, N), BF16))
    return pl.pallas_call(
        functools.partial(_in_kernel, gate_block=gate_block, n_chunks=_row_chunks(tm), cast_w=cast_w),
        grid=(M // tm, N // tn),
        in_specs=[pl.BlockSpec((tm, D), lambda i, j: (i, 0)),
                  _layer_spec(norm_g, layer),
                  _mod_spec(per_row, layer, tm, D, seq_len, 1),
                  _mod_spec(per_row, layer, tm, D, seq_len, 0),
                  _w_spec(w, wl, (D, tn), lambda i, j: (0, j))],
        out_specs=out_specs,
        out_shape=out_shape,
        scratch_shapes=[pltpu.VMEM((tm, D), BF16)],
        compiler_params=_params("parallel", "arbitrary"),
        name="in_proj",
    )(x, norm_g, mod, mod, w)


def _out_kernel(a_ref, b_ref, w_ref, x_ref, g_ref, o_ref, *wb_ref):
    ka = a_ref.shape[1]
    w = w_ref[...]
    if wb_ref:
        w = w.astype(BF16)
        wb_ref[0][...] = w
    y = jnp.dot(a_ref[...], w[:ka], preferred_element_type=F32)
    y = y + jnp.dot(b_ref[...], w[ka:], preferred_element_type=F32)
    o_ref[...] = x_ref[...] + g_ref[...] * y


def _out_proj_sample(a, b, w_out, wl, x, mod, layer, *, tn):
    M, D = x.shape
    Ka, Kb = a.shape[1], b.shape[1]
    nj = D // tn
    out_specs = [pl.BlockSpec((M, tn), lambda j: (0, j))]
    out_shape = [jax.ShapeDtypeStruct((M, D), F32)]
    if w_out.dtype != BF16:
        out_specs.append(pl.BlockSpec((Ka + Kb, tn), lambda j: (0, j)))
        out_shape.append(jax.ShapeDtypeStruct((Ka + Kb, D), BF16))
    return pl.pallas_call(
        _out_kernel,
        grid=(nj,),
        in_specs=[pl.BlockSpec((M, Ka), lambda j: (0, 0)),
                  pl.BlockSpec((M, Kb), lambda j: (0, 0)),
                  _w_spec(w_out, wl, (Ka + Kb, tn), lambda j: (0, j)),
                  pl.BlockSpec((M, tn), lambda j: (0, j)),
                  pl.BlockSpec((None, M, tn), lambda j: (layer, 0, 2 * nj + j))],
        out_specs=out_specs,
        out_shape=out_shape,
        compiler_params=_params("parallel"),
        name="out_proj_sample",
    )(a, b, w_out, x, mod)


def _mlp_kernel(*refs, final_norm, n_chunks, cast_w, n_next):
    x_ref, g_ref, sc_ref, sh_ref, gate_ref, wu_ref, wd_ref = refs[:7]
    rest = list(refs[7:])
    gf_ref = rest.pop(0) if final_norm else None
    next_in = [rest.pop(0) for _ in range(n_next)]
    o_ref = rest.pop(0)
    wub_ref, wdb_ref = (rest.pop(0), rest.pop(0)) if cast_w else (None, None)
    next_out = [rest.pop(0) for _ in range(n_next)]
    h_scr, acc_scr = rest
    f = pl.program_id(1)
    last = pl.num_programs(1) - 1
    rc = x_ref.shape[0] // n_chunks
    if cast_w:
        wub_ref[...] = wu_ref[...].astype(BF16)
        wdb_ref[...] = wd_ref[...].astype(BF16)
        wu_ref, wd_ref = wub_ref, wdb_ref

    def cast_next():
        for src, dst in zip(next_in, next_out):
            dst[...] = src[...].astype(BF16)

    def up_down(h):
        a = jnp.dot(h, wu_ref[...], preferred_element_type=F32)
        a = jnp.square(jnp.maximum(a, 0.0)).astype(BF16)
        return jnp.dot(a, wd_ref[...], preferred_element_type=F32)

    @pl.when(f == 0)
    def _():
        cast_next()
        for r in range(n_chunks):
            rows = slice(r * rc, (r + 1) * rc)
            h = _rms_mod(x_ref[rows, :], g_ref[...], _rows(sc_ref, rows), _rows(sh_ref, rows)).astype(BF16)
            h_scr[rows, :] = h
            acc_scr[rows, :] = up_down(h)

    @pl.when((f != 0) & (f != last))
    def _():
        cast_next()
        acc_scr[...] += up_down(h_scr[...])

    @pl.when(f == last)
    def _():
        cast_next()
        for r in range(n_chunks):
            rows = slice(r * rc, (r + 1) * rc)
            y = x_ref[rows, :] + _rows(gate_ref, rows) * (acc_scr[rows, :] + up_down(h_scr[rows, :]))
            if final_norm:
                ms = jnp.mean(y * y, axis=-1, keepdims=True)
                y = y * lax.rsqrt(ms + EPS) * gf_ref[...]
            o_ref[rows, :] = y


def _mlp(x, norm_g, mod, w_up, w_down, layer, norm_final, *, tm, tf, per_row, seq_len, cast_next=()):
    M, D = x.shape
    FF = w_up.shape[-1]
    final_norm = norm_final is not None
    cast_w = w_up.dtype != BF16
    assert FF // tf >= 2
    nf = FF // tf
    n_steps = (M // tm) * nf
    in_specs = [pl.BlockSpec((tm, D), lambda i, f: (i, 0)),
                _layer_spec(norm_g, layer),
                _mod_spec(per_row, layer, tm, D, seq_len, 4),
                _mod_spec(per_row, layer, tm, D, seq_len, 3),
                _mod_spec(per_row, layer, tm, D, seq_len, 5),
                _w_spec(w_up, layer, (D, tf), lambda i, f: (0, f)),
                _w_spec(w_down, layer, (tf, D), lambda i, f: (f, 0))]
    args = [x, norm_g, mod, mod, mod, w_up, w_down]
    if final_norm:
        in_specs.append(_const_spec(norm_final))
        args.append(norm_final)
    out_specs = [pl.BlockSpec((tm, D), lambda i, f: (i, 0))]
    out_shape = [jax.ShapeDtypeStruct((M, D), F32)]
    if cast_w:
        assert M == tm
        out_specs += [pl.BlockSpec((D, tf), lambda i, f: (0, f)), pl.BlockSpec((tf, D), lambda i, f: (f, 0))]
        out_shape += [jax.ShapeDtypeStruct((D, FF), BF16), jax.ShapeDtypeStruct((FF, D), BF16)]
    for w_next, wl in cast_next:
        _, R, Cn = w_next.shape
        rows = R // n_steps
        assert rows * n_steps == R and rows % 16 == 0
        in_specs.append(pl.BlockSpec((None, rows, Cn), lambda i, f, wl=wl: (wl, i * nf + f, 0)))
        args.append(w_next)
        out_specs.append(pl.BlockSpec((rows, Cn), lambda i, f: (i * nf + f, 0)))
        out_shape.append(jax.ShapeDtypeStruct((R, Cn), BF16))
    return pl.pallas_call(
        functools.partial(_mlp_kernel, final_norm=final_norm, n_chunks=_row_chunks(tm), cast_w=cast_w,
                          n_next=len(cast_next)),
        grid=(M // tm, FF // tf),
        in_specs=in_specs,
        out_specs=out_specs,
        out_shape=out_shape,
        scratch_shapes=[pltpu.VMEM((tm, D), BF16), pltpu.VMEM((tm, D), F32)],
        compiler_params=_params("parallel", "arbitrary"),
        name="mlp",
    )(*args)


def _gla_log_decay(alr, wg, bg):
    pre = jnp.dot(alr.astype(BF16), wg, preferred_element_type=F32) + bg
    return _log_sigmoid(pre) / GLA_GATE_TEMP


def _hgrn_lower_bound(lbraw_ref, layer):
    rows = [lbraw_ref[j:j + 1, :] for j in range(lbraw_ref.shape[0])]
    mx = functools.reduce(jnp.maximum, rows)
    es = [jnp.exp(r - mx) for r in rows]
    tot = functools.reduce(lambda a, b: a + b, es)
    sm = [e / tot for e in es]
    cum = functools.reduce(lambda a, b: a + b, sm[:layer + 1])
    return cum - sm[0]


def _hgrn_gates(fz, lb):
    sig, sig_neg = _sigmoid_pair(fz)
    f = lb + (1.0 - lb) * sig
    g = jnp.log(jnp.maximum(f, MIN_GATE))
    k = (1.0 - lb) * sig_neg
    return g, k


def _level_ref(b_scr, ls, m, C):
    W = ls.stop - ls.start
    if 2 * m >= 2 * SUBLANES:
        pieces = [jnp.broadcast_to(b_scr[n * 2 * m + m - 1:n * 2 * m + m, ls], (2 * m, W))
                  for n in range(C // (2 * m))]
    elif m == 4:
        pieces = [jnp.broadcast_to(b_scr[j * 8 + 3:j * 8 + 4, ls], (8, W)) for j in range(C // 8)]
    else:
        sub = lax.broadcasted_iota(jnp.int32, (8, W), 0)
        pieces = [jnp.where(sub < 4,
                            jnp.broadcast_to(b_scr[j * 8 + 1:j * 8 + 2, ls], (8, W)),
                            jnp.broadcast_to(b_scr[j * 8 + 5:j * 8 + 6, ls], (8, W)))
                  for j in range(C // 8)]
    return jnp.concatenate(pieces, axis=0)


def _rec_chunk(q, k, g, v_ref, r_ref, nw, o_scr, st_scr, b_scr, *, heads, dk, dv, C):
    W = len(heads) * dk
    ls = slice(heads[0] * dk, heads[0] * dk + W)

    row = lax.broadcasted_iota(jnp.int32, (C, C), 0)
    col = lax.broadcasted_iota(jnp.int32, (C, C), 1)
    tril = jnp.where(row >= col, 1.0, 0.0).astype(BF16)
    g1 = g.astype(BF16)
    rem = g - g1.astype(F32)
    g2 = rem.astype(BF16)
    g3 = (rem - g2.astype(F32)).astype(BF16)
    b = (jnp.dot(tril, g1, preferred_element_type=F32)
         + jnp.dot(tril, g2, preferred_element_type=F32)
         + jnp.dot(tril, g3, preferred_element_type=F32))
    b_scr[:, ls] = b

    H = len(heads)
    scores = [None] * H
    levels = []
    m = C // 2
    while m >= 1:
        levels.append(m)
        m //= 2
    sub = lax.broadcasted_iota(jnp.int32, (SUBLANES, W), 0)
    for m in levels + [0]:
        if m == 0:
            qe, ke = q.astype(BF16), k.astype(BF16)
            mask = row == col
        else:
            if m >= SUBLANES:
                src = jnp.concatenate(
                    [a[n * 2 * m + o:n * 2 * m + o + m] for n in range(C // (2 * m))
                     for a, o in ((k, 0), (q, m))], axis=0)
            else:
                upper = (sub & (2 * m - 1)) >= m
                src = jnp.concatenate(
                    [jnp.where(upper, q[j * SUBLANES:(j + 1) * SUBLANES],
                               k[j * SUBLANES:(j + 1) * SUBLANES])
                     for j in range(C // SUBLANES)], axis=0)
            if m == 1:
                odd = (sub & 1) == 1
                e = jnp.concatenate(
                    [jnp.where(odd, jnp.exp(g[j * SUBLANES:(j + 1) * SUBLANES]), 1.0)
                     for j in range(C // SUBLANES)], axis=0)
            else:
                e = jnp.exp(-jnp.abs(b - _level_ref(b_scr, ls, m, C)))
            qe = ke = (src * e).astype(BF16)
            sh = (2 * m).bit_length() - 1
            mask = (((row >> sh) == (col >> sh))
                    & ((row & (2 * m - 1)) >= m) & ((col & (2 * m - 1)) < m))
        for h in range(H):
            hs = slice(h * dk, (h + 1) * dk)
            p = lax.dot_general(qe[:, hs], ke[:, hs], _NT, preferred_element_type=F32)
            scores[h] = jnp.where(mask, p, 0.0 if scores[h] is None else scores[h])

    b_end = b_scr[C - 1:C, ls]
    qi = (q * jnp.exp(b)).astype(BF16)
    kend = (k * jnp.exp(b_end - b)).astype(BF16)
    a_end = jnp.exp(b_end)
    for h, head in enumerate(heads):
        hs = slice(h * dk, (h + 1) * dk)
        vs = slice(head * dv, (head + 1) * dv)
        v_h = v_ref[:, vs]
        st = st_scr[head]
        o = lax.dot_general(qi[:, hs], st.astype(BF16), _NT, preferred_element_type=F32)
        o = o + jnp.dot(scores[h].astype(BF16), v_h, preferred_element_type=F32)
        st_scr[head] = st * a_end[:, hs] + jnp.dot(v_h.astype(F32).T.astype(BF16), kend[:, hs],
                                                   preferred_element_type=F32)
        o_scr[:, vs] = _head_norm_gate(o, nw, r_ref[:, vs].astype(F32)).astype(o_scr.dtype)


def _pool_chunk(u, pw_ref, ps_ref, p_scr, buf, *, C, gc):
    c = pl.program_id(1)
    PB = POOL_PAST + 1
    buf[PB:PB + C, :] = u
    pos = c * C + lax.broadcasted_iota(jnp.int32, (C, 1), 0)
    for g, w in enumerate(POOL_WINDOWS):
        cs = slice(g * gc, (g + 1) * gc)
        cur = u[:, cs]
        win = cur
        for i in range(1, w):
            win = win + buf[PB - i:PB - i + C, cs]
        cnt = jnp.minimum(pos + 1, w).astype(F32)
        d = win / cnt - cur
        y = jnp.dot(d.astype(BF16), pw_ref[g], preferred_element_type=F32) * ps_ref[:, cs]
        p_scr[:, cs] = y.astype(p_scr.dtype)
    buf[0:PB, :] = buf[C:C + PB, :]


def _ln_silu(x, g, b):
    mu = jnp.mean(x, axis=-1, keepdims=True)
    xc = x - mu
    var = jnp.mean(xc * xc, axis=-1, keepdims=True)
    return _silu(xc * lax.rsqrt(var + EPS) * g + b)


def _conv_chunk(cv_in, cw_ref, cb_ref, lg_ref, lb_ref, p_scr, buf, ph, *, C):
    HB = CONV_HIST
    S = SUBLANES
    rt = CONV_ROWS
    buf[HB:HB + C, :] = cv_in
    for p in range(1, S):
        ph[p - 1, S:HB + C, :] = buf[S - p:HB + C - p, :]
    n_sub = rt // S
    for t in range(C // rt):
        accs = [None] * n_sub
        for back in range(CONV_W):
            a, p = divmod(back, S)
            j = CONV_PAST - back
            wt = cw_ref[j * S:(j + 1) * S, :]
            for q in range(n_sub):
                r0 = HB + t * rt + q * S - a * S
                src = buf[r0:r0 + S, :] if p == 0 else ph[p - 1, r0:r0 + S, :]
                term = src * wt
                accs[q] = term if accs[q] is None else accs[q] + term
        acc = jnp.concatenate(accs, axis=0)
        y = _ln_silu(acc + cb_ref[...], lg_ref[...], lb_ref[...])
        p_scr[t * rt:(t + 1) * rt, :] = y.astype(p_scr.dtype)
    buf[0:HB, :] = buf[C:C + HB, :]


def _mixer_kernel(*refs, mode, H, dk, dv, C, layer, gc, u_off):
    if mode == "gla":
        (q_ref, k_ref, v_ref, r_ref, alr_ref, u_ref, wg_ref, bg_ref, nw_ref, pw_ref, ps_ref,
         wa_ref, wb_ref, x_ref, g1_ref, xo_ref, s_ref, sh_ref,
         st_scr, b_scr, o_scr, p_scr, buf) = refs
        hist0, hist_rows = POOL_PAST + 1, POOL_PAST
    else:
        (q_ref, fz_ref, v_ref, r_ref, ga_ref, gb_ref, lbraw_ref, nw_ref, cw_ref, cb_ref, lg_ref, lb_ref,
         wa_ref, wb_ref, x_ref, g1_ref, xo_ref, s_ref, sh_ref,
         st_scr, b_scr, o_scr, p_scr, buf, ph) = refs
        hist0, hist_rows = CONV_HIST, CONV_PAST
    c = pl.program_id(1)

    @pl.when(c == 0)
    def _():
        st_scr[...] = jnp.zeros_like(st_scr)
        buf[0:hist0, :] = jnp.zeros((hist0, buf.shape[1]), F32)

    if mode == "gla":
        u = _shift_lanes(u_ref[...].astype(F32), u_off, p_scr.shape[1])
        _pool_chunk(u, pw_ref, ps_ref, p_scr, buf, C=C, gc=gc)
    else:
        cv_in = ga_ref[...].astype(F32) * _sigmoid(gb_ref[...].astype(F32))
        _conv_chunk(cv_in, cw_ref, cb_ref, lg_ref, lb_ref, p_scr, buf, ph, C=C)
    y = jnp.dot(p_scr[...], wb_ref[...], preferred_element_type=F32)
    hg = H // 2
    for grp in range(2):
        heads = tuple(range(grp * hg, (grp + 1) * hg))
        ls = slice(heads[0] * dk, (heads[-1] + 1) * dk)
        vs = slice(heads[0] * dv, (heads[-1] + 1) * dv)
        if mode == "gla":
            q = q_ref[:, ls].astype(F32) * (dk ** -0.5)
            k = k_ref[:, ls].astype(F32)
            g = _gla_log_decay(alr_ref[...], wg_ref[:, ls], bg_ref[:, ls])
        else:
            q = q_ref[:, ls].astype(F32)
            g, k = _hgrn_gates(fz_ref[:, ls], _hgrn_lower_bound(lbraw_ref, layer)[:, ls])
        _rec_chunk(q, k, g, v_ref, r_ref, nw_ref[...], o_scr, st_scr, b_scr,
                   heads=heads, dk=dk, dv=dv, C=C)
        y = y + jnp.dot(o_scr[:, vs], wa_ref[vs, :], preferred_element_type=F32)
    xo_ref[...] = x_ref[...] + g1_ref[...] * y

    @pl.when(c == pl.num_programs(1) - 1)
    def _():
        for h in range(H):
            s_ref[h] = st_scr[h].T
        sh_ref[...] = buf[C + hist0 - hist_rows:C + hist0, :]


def _mixer_prompt(mode, z, zg, cols, widths, consts, w_out, x, mod, layer, el, *, H, dk, dv, hist_len, u_off=0):
    B, T, _ = z.shape
    D = x.shape[-1]
    C = REC_CHUNK
    V = H * dv
    PW = D - V
    assert T % C == 0

    def zspec(width, start):
        return pl.BlockSpec((None, C, width), lambda b, c: (b, c, start // width))

    in_specs = [zspec(w, s) for w, s in zip(widths, cols)]
    args = [z] * len(cols)
    if mode == "hgrn":
        in_specs[1] = pl.BlockSpec((None, C, widths[1]), lambda b, c: (b, c, 0))
        args[1] = zg
    for a in consts:
        in_specs.append(_layer_spec(a, el) if a.ndim >= 3 else _const_spec(a))
        args.append(a)
    in_specs += [pl.BlockSpec((V, D), lambda b, c: (0, 0)),
                 pl.BlockSpec((PW, D), lambda b, c: (V // PW, 0)),
                 pl.BlockSpec((None, C, D), lambda b, c: (b, c, 0)),
                 pl.BlockSpec((None, None, 1, D), lambda b, c: (layer, b, 0, 2))]
    args += [w_out, w_out, x, mod]
    scratch = [pltpu.VMEM((H, dv, dk), F32), pltpu.VMEM((C, H * dk), F32),
               pltpu.VMEM((C, V), BF16), pltpu.VMEM((C, PW), BF16)]
    if mode == "gla":
        scratch.append(pltpu.VMEM((POOL_PAST + 1 + C, PW), F32))
    else:
        scratch += [pltpu.VMEM((CONV_HIST + C, PW), F32),
                    pltpu.VMEM((SUBLANES - 1, CONV_HIST + C, PW), F32)]
    return pl.pallas_call(
        functools.partial(_mixer_kernel, mode=mode, H=H, dk=dk, dv=dv, C=C, layer=el,
                          gc=PW // len(POOL_WINDOWS), u_off=u_off),
        grid=(B, T // C),
        in_specs=in_specs,
        out_specs=[pl.BlockSpec((None, C, D), lambda b, c: (b, c, 0)),
                   pl.BlockSpec((None, H, dk, dv), lambda b, c: (b, 0, 0, 0)),
                   pl.BlockSpec((None, hist_len, PW), lambda b, c: (b, 0, 0))],
        out_shape=[jax.ShapeDtypeStruct((B, T, D), F32),
                   jax.ShapeDtypeStruct((B, H, dk, dv), F32),
                   jax.ShapeDtypeStruct((B, hist_len, PW), F32)],
        scratch_shapes=scratch,
        compiler_params=_params("parallel", "arbitrary"),
        name="mixer_" + mode,
    )(*args)


def _step_recurrence(a, k, q, v_ref, s_ref, so_ref, tile_scr, row_scr, *, H, dk, dv, bb):
    for h in range(H):
        hs = slice(h * dk, (h + 1) * dk)
        tile_scr[...] = jnp.zeros_like(tile_scr)
        tile_scr[0:bb, :] = a[:, hs]
        tile_scr[bb:2 * bb, :] = k[:, hs]
        tile_scr[2 * bb:3 * bb, :] = q[:, hs]
        cols = tile_scr[...].T
        for s in range(bb):
            a_col = cols[:, s:s + 1]
            k_col = cols[:, bb + s:bb + s + 1]
            q_col = cols[:, 2 * bb + s:2 * bb + s + 1]
            v_row = v_ref[s:s + 1, h * dv:(h + 1) * dv]
            s_new = a_col * s_ref[s, h] + k_col * v_row
            so_ref[s, h] = s_new
            row_scr[s:s + 1, h * dv:(h + 1) * dv] = jnp.sum(q_col * s_new, axis=0, keepdims=True)


def _even1_kernel(*refs, H, dk, dv, bb, gc, n_alias, u_off):
    (q_ref, k_ref, v_ref, r_ref, alr_ref, ub_ref, sg_ref, sp_ref,
     wg_ref, bg_ref, nw_ref, pw_ref, ps_ref) = refs[:13]
    o_ref, p_ref, sgo_ref, spo_ref, tile_scr, row_scr, d_scr, u_scr = refs[13 + n_alias:]
    g = _gla_log_decay(alr_ref[...], wg_ref[...], bg_ref[...])
    _step_recurrence(jnp.exp(g), k_ref[...], q_ref[...] * (dk ** -0.5), v_ref, sg_ref, sgo_ref,
                     tile_scr, row_scr, H=H, dk=dk, dv=dv, bb=bb)
    nw = nw_ref[...]
    for h in range(H):
        vs = slice(h * dv, (h + 1) * dv)
        o_ref[:, vs] = _head_norm_gate(row_scr[:, vs], nw, r_ref[:, vs]).astype(o_ref.dtype)

    u_scr[...] = _shift_lanes(ub_ref[...], u_off, u_scr.shape[1])
    for s in range(bb):
        for g_i, w in enumerate(POOL_WINDOWS):
            cs = slice(g_i * gc, (g_i + 1) * gc)
            cur = u_scr[s:s + 1, cs]
            win = cur + jnp.sum(sp_ref[s, POOL_PAST - (w - 1):POOL_PAST, cs], axis=0, keepdims=True)
            cnt = float(min(PAST_LEN + 1, w))
            d_scr[s:s + 1, cs] = win / cnt - cur
        spo_ref[s, 0:POOL_PAST - 1, :] = sp_ref[s, 1:POOL_PAST, :]
        spo_ref[s, POOL_PAST - 1:POOL_PAST, :] = u_scr[s:s + 1, :]
    for g_i in range(len(POOL_WINDOWS)):
        cs = slice(g_i * gc, (g_i + 1) * gc)
        y = jnp.dot(d_scr[:, cs].astype(BF16), pw_ref[g_i], preferred_element_type=F32) * ps_ref[:, cs]
        p_ref[:, cs] = y.astype(p_ref.dtype)


def _odd1_kernel(*refs, H, dk, dv, bb, layer, n_alias):
    (q_ref, fz_ref, v_ref, r_ref, ga_ref, gb_ref, sh_ref, sc_ref,
     lbraw_ref, nw_ref, cw_ref, cb_ref, lg_ref, lb_ref) = refs[:14]
    o_ref, cv_ref, sho_ref, sco_ref, tile_scr, row_scr, d_scr = refs[14 + n_alias:]
    g, k = _hgrn_gates(fz_ref[...], _hgrn_lower_bound(lbraw_ref, layer))
    _step_recurrence(jnp.exp(g), k, q_ref[...], v_ref, sh_ref, sho_ref,
                     tile_scr, row_scr, H=H, dk=dk, dv=dv, bb=bb)
    nw = nw_ref[...]
    for h in range(H):
        vs = slice(h * dv, (h + 1) * dv)
        o_ref[:, vs] = _head_norm_gate(row_scr[:, vs], nw, r_ref[:, vs]).astype(o_ref.dtype)

    cv_in = ga_ref[...] * _sigmoid(gb_ref[...])
    d_scr[...] = cv_in * cw_ref[CONV_PAST:CONV_W, :]
    for s in range(bb):
        hist = sc_ref[s]
        d_scr[s:s + 1, :] += jnp.sum(hist * cw_ref[0:CONV_PAST, :], axis=0, keepdims=True)
        sco_ref[s, 0:CONV_PAST - 1, :] = sc_ref[s, 1:CONV_PAST, :]
        sco_ref[s, CONV_PAST - 1:CONV_PAST, :] = cv_in[s:s + 1, :]
    y = _ln_silu(d_scr[...] + cb_ref[...], lg_ref[...], lb_ref[...])
    cv_ref[...] = y.astype(cv_ref.dtype)


def _sample_mixer(kernel_fn, name, z, cols, widths, states, prev, consts, el, *, bb, extra_scratch=()):
    Bs = z.shape[0]
    s_mat, s_hist = states
    _, _, H, dk, dv = s_mat.shape
    hist_len, HW = s_hist.shape[2:]
    V = H * dv

    def zspec(width, start):
        return pl.BlockSpec((bb, width), lambda i: (i, start // width))

    mat_spec = pl.BlockSpec((None, bb, H, dk, dv), lambda i: (el, i, 0, 0, 0))
    hist_spec = pl.BlockSpec((None, bb, hist_len, HW), lambda i: (el, i, 0, 0))
    in_specs = [zspec(w, c) for w, c in zip(widths, cols)] + [mat_spec, hist_spec]
    args = [z] * len(cols) + [s_mat, s_hist]
    for a in consts:
        in_specs.append(_layer_spec(a, el) if a.ndim >= 3 else _const_spec(a))
        args.append(a)
    aliases = {}
    if prev is not None:
        for n, p in enumerate(prev):
            aliases[len(args)] = 2 + n
            in_specs.append(pl.BlockSpec(memory_space=pl.ANY))
            args.append(p)
    n_alias = 0 if prev is None else len(prev)
    return pl.pallas_call(
        functools.partial(kernel_fn, H=H, dk=dk, dv=dv, bb=bb, n_alias=n_alias),
        grid=(Bs // bb,),
        in_specs=in_specs,
        out_specs=[pl.BlockSpec((bb, V), lambda i: (i, 0)),
                   pl.BlockSpec((bb, HW), lambda i: (i, 0)),
                   mat_spec, hist_spec],
        out_shape=[jax.ShapeDtypeStruct((Bs, V), BF16), jax.ShapeDtypeStruct((Bs, HW), BF16),
                   jax.ShapeDtypeStruct(s_mat.shape, F32), jax.ShapeDtypeStruct(s_hist.shape, F32)],
        scratch_shapes=[pltpu.VMEM((LANES, dk), F32), pltpu.VMEM((bb, V), F32),
                        pltpu.VMEM((bb, HW), F32)] + list(extra_scratch),
        input_output_aliases=aliases,
        compiler_params=_params("parallel"),
        name=name,
    )(*args)


def _pick(m, pref):
    return pref if m % pref == 0 else m


def kernel(x_prompt, x_sample, c_prompt, c_sample, state_gla, state_pool, state_hgrn, state_conv,
           w_ada, b_ada, norm_mix, norm_mlp, norm_final,
           ev_w_in, ev_w_gate_up, ev_b_gate, ev_gla_norm, ev_pool_w, ev_pool_scale, ev_w_out,
           od_w_in, od_lb_raw, od_hgrn_norm, od_conv_w, od_conv_b, od_ln_g, od_ln_b, od_w_out,
           mlp_w_up, mlp_w_down):
    B, T, D = x_prompt.shape
    Bs = x_sample.shape[0]
    depth = w_ada.shape[0]
    gla_w = D // 2
    gla_key_w = gla_w // 2
    gla_dk = gla_key_w // GLA_HEADS
    gla_dv = gla_w // GLA_HEADS
    pool_w = D - gla_w
    hgrn_w = D // 2
    hgrn_heads = hgrn_w // HGRN_DK
    hgrn_dv = hgrn_w // hgrn_heads
    conf_w = D - hgrn_w

    n_main = 2 * gla_key_w + 2 * gla_w
    ub_w = n_main // 2
    assert GLA_GATE_RANK + pool_w <= ub_w
    ev_w = jnp.pad(ev_w_in, ((0, 0), (0, 0), (0, n_main + ub_w - ev_w_in.shape[-1])))
    ev_cols = (0, gla_key_w, 2 * gla_key_w, 2 * gla_key_w + gla_w, n_main, n_main)
    ev_widths = (gla_key_w, gla_key_w, gla_w, gla_w, LANES, ub_w)
    wg = jnp.pad(ev_w_gate_up, ((0, 0), (0, LANES - GLA_GATE_RANK), (0, 0))).astype(BF16)
    od_cols = (0, hgrn_w, 2 * hgrn_w, 3 * hgrn_w, 4 * hgrn_w, 4 * hgrn_w + conf_w)
    od_widths = (hgrn_w, hgrn_w, hgrn_w, hgrn_w, conf_w, conf_w)
    pool_wb = ev_pool_w.astype(BF16)
    conv_wr = jnp.repeat(od_conv_w, SUBLANES, axis=1)
    conv_wp = jnp.pad(od_conv_w, ((0, 0), (0, 1), (0, 0)))

    lay3 = lambda a: a.reshape(a.shape[0], 1, a.shape[1])
    norm_mix3, norm_mlp3 = lay3(norm_mix), lay3(norm_mlp)
    bg3, gn3, ps3 = lay3(ev_b_gate), lay3(ev_gla_norm), lay3(ev_pool_scale)
    hn3, cb3, lg3, lb3 = lay3(od_hgrn_norm), lay3(od_conv_b), lay3(od_ln_g), lay3(od_ln_b)

    pad = (-(B + Bs)) % 16
    c_all = jnp.concatenate([c_sample, c_prompt, jnp.zeros((pad, D), F32)], axis=0)
    mod_s, mod_p = _ada(c_all, w_ada, b_ada, Bs, B)

    xp = x_prompt.reshape(B * T, D)
    xs = x_sample.reshape(Bs, D)
    Mp = B * T
    tm_in = _pick(Mp, 1024)
    tm_p = _pick(Mp, 512)
    bb = 8

    new = {k: [] for k in ("gla_p", "pool_p", "hgrn_p", "conv_p")}
    ev_prev = od_prev = cast = None
    for l in range(depth):
        e = l // 2
        nf = norm_final.reshape(1, D) if l == depth - 1 else None
        kw_p = dict(per_row=False, seq_len=T)
        kw_s = dict(per_row=True, seq_len=1)
        w_in_f, w_out_f = (ev_w, ev_w_out) if l % 2 == 0 else (od_w_in, od_w_out)
        if cast is None:
            w_in_s, w_out_s, w_up_s, w_down_s, wl = w_in_f, w_out_f, mlp_w_up, mlp_w_down, e
        else:
            (w_up_s, w_down_s, w_in_s, w_out_s), wl = cast, None
        if l % 2 == 0:
            n_in = ev_w.shape[-1]
            tn = ub_w
            zs, *w_in_b = _in_proj(xs, norm_mix3, mod_s, w_in_s, l, wl, tm=Bs, tn=tn, out_dtype=F32, **kw_s)
            a_s, b_s, sgs, sps = _sample_mixer(
                functools.partial(_even1_kernel, gc=pool_w // len(POOL_WINDOWS), u_off=GLA_GATE_RANK),
                "even_sample", zs, ev_cols, ev_widths, (state_gla, state_pool), ev_prev,
                (wg, bg3, gn3, pool_wb, ps3), e, bb=bb, extra_scratch=[pltpu.VMEM((bb, pool_w), F32)])
            ev_prev = (sgs, sps)
        else:
            n_in = od_w_in.shape[-1]
            tn = _pick(n_in, 1024)
            zs, *w_in_b = _in_proj(xs, norm_mix3, mod_s, w_in_s, l, wl, tm=Bs, tn=tn, out_dtype=F32, **kw_s)
            a_s, b_s, shs, scs = _sample_mixer(
                functools.partial(_odd1_kernel, layer=e), "odd_sample",
                zs, od_cols, od_widths, (state_hgrn, state_conv), od_prev,
                (od_lb_raw, hn3, conv_wp, cb3, lg3, lb3), e, bb=bb)
            od_prev = (shs, scs)
        xs, *w_out_b = _out_proj_sample(a_s, b_s, w_out_s, wl, xs, mod_s, l, tn=_pick(D, 512))
        xs, *w_mlp_b = _mlp(xs, norm_mlp3, mod_s, w_up_s, w_down_s, l, nf, tm=Bs, tf=512, **kw_s)
        if cast is None:
            (w_in_b,), (w_out_b,), (w_up_b, w_down_b) = w_in_b, w_out_b, w_mlp_b
        else:
            w_up_b, w_down_b, w_in_b, w_out_b = cast

        xp3 = xp.reshape(B, T, D)
        if l % 2 == 0:
            (zp,) = _in_proj(xp, norm_mix3, mod_p, w_in_b, l, None, tm=tm_in, tn=tn, out_dtype=BF16, **kw_p)
            xp3, sg, sp = _mixer_prompt("gla", zp.reshape(B, T, n_in), None, ev_cols, ev_widths,
                                        (wg, bg3, gn3, pool_wb, ps3), w_out_b, xp3, mod_p, l, e,
                                        H=GLA_HEADS, dk=gla_dk, dv=gla_dv, hist_len=POOL_PAST,
                                        u_off=GLA_GATE_RANK)
            new["gla_p"].append(sg)
            new["pool_p"].append(sp)
        else:
            zp, zg = _in_proj(xp, norm_mix3, mod_p, w_in_b, l, None, tm=tm_in, tn=tn, out_dtype=BF16,
                              gate_block=od_cols[1] // tn, **kw_p)
            xp3, shg, scv = _mixer_prompt("hgrn", zp.reshape(B, T, n_in), zg.reshape(B, T, tn),
                                          od_cols, od_widths, (od_lb_raw, hn3, conv_wr, cb3, lg3, lb3),
                                          w_out_b, xp3, mod_p, l, e,
                                          H=hgrn_heads, dk=HGRN_DK, dv=hgrn_dv, hist_len=CONV_PAST)
            new["hgrn_p"].append(shg)
            new["conv_p"].append(scv)
        nxt = ()
        if l + 1 < depth:
            e_n = (l + 1) // 2
            w_in_n, w_out_n = (ev_w, ev_w_out) if (l + 1) % 2 == 0 else (od_w_in, od_w_out)
            nxt = ((mlp_w_up, l + 1), (mlp_w_down, l + 1), (w_in_n, e_n), (w_out_n, e_n))
        xp, *cast = _mlp(xp3.reshape(Mp, D), norm_mlp3, mod_p, w_up_b, w_down_b, l, nf,
                         tm=tm_p, tf=1024, cast_next=nxt, **kw_p)
        cast = cast or None

    st = lambda k: jnp.stack(new[k])
    return (xp.reshape(B, T, D), xs.reshape(Bs, 1, D),
            st("gla_p"), st("pool_p"), st("hgrn_p"), st("conv_p"),
            ev_prev[0], ev_prev[1], od_prev[0], od_prev[1])
```

```python
import functools

import jax
import jax.numpy as jnp
from jax import lax
from jax.experimental import pallas as pl
from jax.experimental.pallas import tpu as pltpu

F32 = jnp.float32
BF16 = jnp.bfloat16

EPS = 1e-6
MIN_GATE = 1e-30
PAST_LEN = 16384
GLA_HEADS = 4
GLA_GATE_RANK = 16
GLA_GATE_TEMP = 16.0
POOL_WINDOWS = (2, 4, 8, 16)
POOL_PAST = max(POOL_WINDOWS) - 1
CONV_W = 31
CONV_PAST = CONV_W - 1
HGRN_DK = 128
LANES = 128
SUBLANES = 8
REC_CHUNK = 128
CONV_HIST = 32
CONV_ROWS = 32
VMEM_LIMIT = 56 * 1024 * 1024

_NT = (((1,), (1,)), ((), ()))


def _params(*sem):
    return pltpu.CompilerParams(dimension_semantics=sem, vmem_limit_bytes=VMEM_LIMIT)


def _sigmoid(x):
    return 0.5 * jnp.tanh(0.5 * x) + 0.5


def _sigmoid_pair(x):
    t = jnp.exp(-jnp.abs(x))
    r = 1.0 / (1.0 + t)
    tr = t * r
    pos = x >= 0.0
    return jnp.where(pos, r, tr), jnp.where(pos, tr, r)


def _silu(x):
    return x * _sigmoid(x)


def _log_sigmoid(x):
    return jnp.minimum(x, 0.0) - jnp.log(1.0 + jnp.exp(-jnp.abs(x)))


def _rms_mod(x, g, sc, sh):
    ms = jnp.mean(x * x, axis=-1, keepdims=True)
    return (x * lax.rsqrt(ms + EPS) * g) * (1.0 + sc) + sh


def _head_norm_gate(o, nw, r):
    ms = jnp.mean(o * o, axis=-1, keepdims=True)
    return o * lax.rsqrt(ms + EPS) * nw * _silu(r)


def _shift_lanes(x, off, width):
    if off == 0:
        return x[:, :width]
    return pltpu.roll(x, x.shape[1] - off, 1)[:, :width]


def _const_spec(a):
    return pl.BlockSpec(a.shape, lambda *_: (0,) * a.ndim)


def _layer_spec(a, layer):
    return pl.BlockSpec((None,) + a.shape[1:], lambda *_: (layer,) + (0,) * (a.ndim - 1))


def _ada_kernel(c_ref, w_ref, b_ref, ms_ref, mp_ref, *, n_s, n_p):
    act = _silu(c_ref[...]).astype(BF16)
    res = jnp.dot(act, w_ref[...].astype(BF16), preferred_element_type=F32) + b_ref[...]
    ms_ref[...] = res[:n_s]
    for i in range(n_p):
        mp_ref[i] = res[n_s + i:n_s + i + 1]


def _ada(c_all, w_ada, b_ada, n_s, n_p, tn=1024):
    L, D, N = w_ada.shape
    R = c_all.shape[0]
    return pl.pallas_call(
        functools.partial(_ada_kernel, n_s=n_s, n_p=n_p),
        grid=(L, N // tn),
        in_specs=[pl.BlockSpec((R, D), lambda l, j: (0, 0)),
                  pl.BlockSpec((None, D, tn), lambda l, j: (l, 0, j)),
                  pl.BlockSpec((None, 1, tn), lambda l, j: (l, 0, j))],
        out_specs=[pl.BlockSpec((None, n_s, tn), lambda l, j: (l, 0, j)),
                   pl.BlockSpec((None, n_p, 1, tn), lambda l, j: (l, 0, 0, j))],
        out_shape=[jax.ShapeDtypeStruct((L, n_s, N), F32),
                   jax.ShapeDtypeStruct((L, n_p, 1, N), F32)],
        compiler_params=_params("parallel", "parallel"),
        name="ada_mod",
    )(c_all, w_ada, b_ada.reshape(L, 1, N))


def _mod_spec(per_row, layer, tm, D, seq_len, part):
    if per_row:
        return pl.BlockSpec((None, tm, D), lambda i, *_: (layer, i, part))
    return pl.BlockSpec((None, None, 1, D), lambda i, *_: (layer, (i * tm) // seq_len, 0, part))


def _row_chunks(tm):
    return 4 if tm % 512 == 0 else 1


def _rows(ref, rows):
    return ref[...] if ref.shape[0] == 1 else ref[rows, :]


def _w_spec(w, wl, block, index):
    if w.ndim == 2:
        return pl.BlockSpec(block, index)
    return pl.BlockSpec((None,) + block, lambda *g: (wl,) + index(*g))


def _in_kernel(*refs, gate_block, n_chunks, cast_w):
    x_ref, g_ref, sc_ref, sh_ref, w_ref, z_ref = refs[:6]
    rest = list(refs[6:])
    zg_ref = rest.pop(0) if gate_block is not None else None
    wb_ref = rest.pop(0) if cast_w else None
    (h_scr,) = rest
    j = pl.program_id(1)
    rc = x_ref.shape[0] // n_chunks
    if cast_w:
        wb_ref[...] = w_ref[...].astype(BF16)
        w_ref = wb_ref

    @pl.when(j == 0)
    def _():
        for r in range(n_chunks):
            rows = slice(r * rc, (r + 1) * rc)
            h = _rms_mod(x_ref[rows, :], g_ref[...], _rows(sc_ref, rows), _rows(sh_ref, rows)).astype(BF16)
            h_scr[rows, :] = h
            z_ref[rows, :] = jnp.dot(h, w_ref[...], preferred_element_type=F32).astype(z_ref.dtype)

    @pl.when(j != 0)
    def _():
        acc = jnp.dot(h_scr[...], w_ref[...], preferred_element_type=F32)
        z_ref[...] = acc.astype(z_ref.dtype)
        if gate_block is not None:
            @pl.when(j == gate_block)
            def _():
                zg_ref[...] = acc


def _in_proj(x, norm_g, mod, w, layer, wl, *, tm, tn, per_row, seq_len, out_dtype, gate_block=None):
    M, D = x.shape
    N = w.shape[-1]
    cast_w = w.dtype != BF16
    assert gate_block != 0 and N // tn >= 2
    out_specs = [pl.BlockSpec((tm, tn), lambda i, j: (i, j))]
    out_shape = [jax.ShapeDtypeStruct((M, N), out_dtype)]
    if gate_block is not None:
        out_specs.append(pl.BlockSpec((tm, tn), lambda i, j: (i, 0)))
        out_shape.append(jax.ShapeDtypeStruct((M, tn), F32))
    if cast_w:
        assert M == tm
        out_specs.append(pl.BlockSpec((D, tn), lambda i, j: (0, j)))
        out_shape.append(jax.ShapeDtypeStruct((D, N), BF16))
    return pl.pallas_call(
        functools.partial(_in_kernel, gate_block=gate_block, n_chunks=_row_chunks(tm), cast_w=cast_w),
        grid=(M // tm, N // tn),
        in_specs=[pl.BlockSpec((tm, D), lambda i, j: (i, 0)),
                  _layer_spec(norm_g, layer),
                  _mod_spec(per_row, layer, tm, D, seq_len, 1),
                  _mod_spec(per_row, layer, tm, D, seq_len, 0),
                  _w_spec(w, wl, (D, tn), lambda i, j: (0, j))],
        out_specs=out_specs,
        out_shape=out_shape,
        scratch_shapes=[pltpu.VMEM((tm, D), BF16)],
        compiler_params=_params("parallel", "arbitrary"),
        name="in_proj",
    )(x, norm_g, mod, mod, w)


def _out_kernel(a_ref, b_ref, w_ref, x_ref, g_ref, o_ref, *wb_ref):
    ka = a_ref.shape[1]
    w = w_ref[...]
    if wb_ref:
        w = w.astype(BF16)
        wb_ref[0][...] = w
    y = jnp.dot(a_ref[...], w[:ka], preferred_element_type=F32)
    y = y + jnp.dot(b_ref[...], w[ka:], preferred_element_type=F32)
    o_ref[...] = x_ref[...] + g_ref[...] * y


def _out_proj_sample(a, b, w_out, wl, x, mod, layer, *, tn):
    M, D = x.shape
    Ka, Kb = a.shape[1], b.shape[1]
    nj = D // tn
    out_specs = [pl.BlockSpec((M, tn), lambda j: (0, j))]
    out_shape = [jax.ShapeDtypeStruct((M, D), F32)]
    if w_out.dtype != BF16:
        out_specs.append(pl.BlockSpec((Ka + Kb, tn), lambda j: (0, j)))
        out_shape.append(jax.ShapeDtypeStruct((Ka + Kb, D), BF16))
    return pl.pallas_call(
        _out_kernel,
        grid=(nj,),
        in_specs=[pl.BlockSpec((M, Ka), lambda j: (0, 0)),
                  pl.BlockSpec((M, Kb), lambda j: (0, 0)),
                  _w_spec(w_out, wl, (Ka + Kb, tn), lambda j: (0, j)),
                  pl.BlockSpec((M, tn), lambda j: (0, j)),
                  pl.BlockSpec((None, M, tn), lambda j: (layer, 0, 2 * nj + j))],
        out_specs=out_specs,
        out_shape=out_shape,
        compiler_params=_params("parallel"),
        name="out_proj_sample",
    )(a, b, w_out, x, mod)


def _mlp_kernel(*refs, final_norm, n_chunks, cast_w, n_next):
    x_ref, g_ref, sc_ref, sh_ref, gate_ref, wu_ref, wd_ref = refs[:7]
    rest = list(refs[7:])
    gf_ref = rest.pop(0) if final_norm else None
    next_in = [rest.pop(0) for _ in range(n_next)]
    o_ref = rest.pop(0)
    wub_ref, wdb_ref = (rest.pop(0), rest.pop(0)) if cast_w else (None, None)
    next_out = [rest.pop(0) for _ in range(n_next)]
    h_scr, acc_scr = rest
    f = pl.program_id(1)
    last = pl.num_programs(1) - 1
    rc = x_ref.shape[0] // n_chunks
    if cast_w:
        wub_ref[...] = wu_ref[...].astype(BF16)
        wdb_ref[...] = wd_ref[...].astype(BF16)
        wu_ref, wd_ref = wub_ref, wdb_ref

    def cast_next():
        for src, dst in zip(next_in, next_out):
            dst[...] = src[...].astype(BF16)

    def up_down(h):
        a = jnp.dot(h, wu_ref[...], preferred_element_type=F32)
        a = jnp.square(jnp.maximum(a, 0.0)).astype(BF16)
        return jnp.dot(a, wd_ref[...], preferred_element_type=F32)

    @pl.when(f == 0)
    def _():
        cast_next()
        for r in range(n_chunks):
            rows = slice(r * rc, (r + 1) * rc)
            h = _rms_mod(x_ref[rows, :], g_ref[...], _rows(sc_ref, rows), _rows(sh_ref, rows)).astype(BF16)
            h_scr[rows, :] = h
            acc_scr[rows, :] = up_down(h)

    @pl.when((f != 0) & (f != last))
    def _():
        cast_next()
        acc_scr[...] += up_down(h_scr[...])

    @pl.when(f == last)
    def _():
        cast_next()
        for r in range(n_chunks):
            rows = slice(r * rc, (r + 1) * rc)
            y = x_ref[rows, :] + _rows(gate_ref, rows) * (acc_scr[rows, :] + up_down(h_scr[rows, :]))
            if final_norm:
                ms = jnp.mean(y * y, axis=-1, keepdims=True)
                y = y * lax.rsqrt(ms + EPS) * gf_ref[...]
            o_ref[rows, :] = y


def _mlp(x, norm_g, mod, w_up, w_down, layer, norm_final, *, tm, tf, per_row, seq_len, cast_next=()):
    M, D = x.shape
    FF = w_up.shape[-1]
    final_norm = norm_final is not None
    cast_w = w_up.dtype != BF16
    assert FF // tf >= 2
    nf = FF // tf
    n_steps = (M // tm) * nf
    in_specs = [pl.BlockSpec((tm, D), lambda i, f: (i, 0)),
                _layer_spec(norm_g, layer),
                _mod_spec(per_row, layer, tm, D, seq_len, 4),
                _mod_spec(per_row, layer, tm, D, seq_len, 3),
                _mod_spec(per_row, layer, tm, D, seq_len, 5),
                _w_spec(w_up, layer, (D, tf), lambda i, f: (0, f)),
                _w_spec(w_down, layer, (tf, D), lambda i, f: (f, 0))]
    args = [x, norm_g, mod, mod, mod, w_up, w_down]
    if final_norm:
        in_specs.append(_const_spec(norm_final))
        args.append(norm_final)
    out_specs = [pl.BlockSpec((tm, D), lambda i, f: (i, 0))]
    out_shape = [jax.ShapeDtypeStruct((M, D), F32)]
    if cast_w:
        assert M == tm
        out_specs += [pl.BlockSpec((D, tf), lambda i, f: (0, f)), pl.BlockSpec((tf, D), lambda i, f: (f, 0))]
        out_shape += [jax.ShapeDtypeStruct((D, FF), BF16), jax.ShapeDtypeStruct((FF, D), BF16)]
    for w_next, wl in cast_next:
        _, R, Cn = w_next.shape
        rows = R // n_steps
        assert rows * n_steps == R and rows % 16 == 0
        in_specs.append(pl.BlockSpec((None, rows, Cn), lambda i, f, wl=wl: (wl, i * nf + f, 0)))
        args.append(w_next)
        out_specs.append(pl.BlockSpec((rows, Cn), lambda i, f: (i * nf + f, 0)))
        out_shape.append(jax.ShapeDtypeStruct((R, Cn), BF16))
    return pl.pallas_call(
        functools.partial(_mlp_kernel, final_norm=final_norm, n_chunks=_row_chunks(tm), cast_w=cast_w,
                          n_next=len(cast_next)),
        grid=(M // tm, FF // tf),
        in_specs=in_specs,
        out_specs=out_specs,
        out_shape=out_shape,
        scratch_shapes=[pltpu.VMEM((tm, D), BF16), pltpu.VMEM((tm, D), F32)],
        compiler_params=_params("parallel", "arbitrary"),
        name="mlp",
    )(*args)


def _gla_log_decay(alr, wg, bg):
    pre = jnp.dot(alr.astype(BF16), wg, preferred_element_type=F32) + bg
    return _log_sigmoid(pre) / GLA_GATE_TEMP


def _hgrn_lower_bound(lbraw_ref, layer):
    rows = [lbraw_ref[j:j + 1, :] for j in range(lbraw_ref.shape[0])]
    mx = functools.reduce(jnp.maximum, rows)
    es = [jnp.exp(r - mx) for r in rows]
    tot = functools.reduce(lambda a, b: a + b, es)
    sm = [e / tot for e in es]
    cum = functools.reduce(lambda a, b: a + b, sm[:layer + 1])
    return cum - sm[0]


def _hgrn_gates(fz, lb):
    sig, sig_neg = _sigmoid_pair(fz)
    f = lb + (1.0 - lb) * sig
    g = jnp.log(jnp.maximum(f, MIN_GATE))
    k = (1.0 - lb) * sig_neg
    return g, k


def _level_ref(b_scr, ls, m, C):
    W = ls.stop - ls.start
    if 2 * m >= 2 * SUBLANES:
        pieces = [jnp.broadcast_to(b_scr[n * 2 * m + m - 1:n * 2 * m + m, ls], (2 * m, W))
                  for n in range(C // (2 * m))]
    elif m == 4:
        pieces = [jnp.broadcast_to(b_scr[j * 8 + 3:j * 8 + 4, ls], (8, W)) for j in range(C // 8)]
    else:
        sub = lax.broadcasted_iota(jnp.int32, (8, W), 0)
        pieces = [jnp.where(sub < 4,
                            jnp.broadcast_to(b_scr[j * 8 + 1:j * 8 + 2, ls], (8, W)),
                            jnp.broadcast_to(b_scr[j * 8 + 5:j * 8 + 6, ls], (8, W)))
                  for j in range(C // 8)]
    return jnp.concatenate(pieces, axis=0)


def _rec_chunk(q, k, g, v_ref, r_ref, nw, o_scr, st_scr, b_scr, *, heads, dk, dv, C):
    W = len(heads) * dk
    ls = slice(heads[0] * dk, heads[0] * dk + W)

    row = lax.broadcasted_iota(jnp.int32, (C, C), 0)
    col = lax.broadcasted_iota(jnp.int32, (C, C), 1)
    tril = jnp.where(row >= col, 1.0, 0.0).astype(BF16)
    g1 = g.astype(BF16)
    rem = g - g1.astype(F32)
    g2 = rem.astype(BF16)
    g3 = (rem - g2.astype(F32)).astype(BF16)
    b = (jnp.dot(tril, g1, preferred_element_type=F32)
         + jnp.dot(tril, g2, preferred_element_type=F32)
         + jnp.dot(tril, g3, preferred_element_type=F32))
    b_scr[:, ls] = b

    H = len(heads)
    scores = [None] * H
    levels = []
    m = C // 2
    while m >= 1:
        levels.append(m)
        m //= 2
    sub = lax.broadcasted_iota(jnp.int32, (SUBLANES, W), 0)
    for m in levels + [0]:
        if m == 0:
            qe, ke = q.astype(BF16), k.astype(BF16)
            mask = row == col
        else:
            if m >= SUBLANES:
                src = jnp.concatenate(
                    [a[n * 2 * m + o:n * 2 * m + o + m] for n in range(C // (2 * m))
                     for a, o in ((k, 0), (q, m))], axis=0)
            else:
                upper = (sub & (2 * m - 1)) >= m
                src = jnp.concatenate(
                    [jnp.where(upper, q[j * SUBLANES:(j + 1) * SUBLANES],
                               k[j * SUBLANES:(j + 1) * SUBLANES])
                     for j in range(C // SUBLANES)], axis=0)
            if m == 1:
                odd = (sub & 1) == 1
                e = jnp.concatenate(
                    [jnp.where(odd, jnp.exp(g[j * SUBLANES:(j + 1) * SUBLANES]), 1.0)
                     for j in range(C // SUBLANES)], axis=0)
            else:
                e = jnp.exp(-jnp.abs(b - _level_ref(b_scr, ls, m, C)))
            qe = ke = (src * e).astype(BF16)
            sh = (2 * m).bit_length() - 1
            mask = (((row >> sh) == (col >> sh))
                    & ((row & (2 * m - 1)) >= m) & ((col & (2 * m - 1)) < m))
        for h in range(H):
            hs = slice(h * dk, (h + 1) * dk)
            p = lax.dot_general(qe[:, hs], ke[:, hs], _NT, preferred_element_type=F32)
            scores[h] = jnp.where(mask, p, 0.0 if scores[h] is None else scores[h])

    b_end = b_scr[C - 1:C, ls]
    qi = (q * jnp.exp(b)).astype(BF16)
    kend = (k * jnp.exp(b_end - b)).astype(BF16)
    a_end = jnp.exp(b_end)
    for h, head in enumerate(heads):
        hs = slice(h * dk, (h + 1) * dk)
        vs = slice(head * dv, (head + 1) * dv)
        v_h = v_ref[:, vs]
        st = st_scr[head]
        o = lax.dot_general(qi[:, hs], st.astype(BF16), _NT, preferred_element_type=F32)
        o = o + jnp.dot(scores[h].astype(BF16), v_h, preferred_element_type=F32)
        st_scr[head] = st * a_end[:, hs] + jnp.dot(v_h.astype(F32).T.astype(BF16), kend[:, hs],
                                                   preferred_element_type=F32)
        o_scr[:, vs] = _head_norm_gate(o, nw, r_ref[:, vs].astype(F32)).astype(o_scr.dtype)


def _pool_chunk(u, pw_ref, ps_ref, p_scr, buf, *, C, gc):
    c = pl.program_id(1)
    PB = POOL_PAST + 1
    buf[PB:PB + C, :] = u
    pos = c * C + lax.broadcasted_iota(jnp.int32, (C, 1), 0)
    for g, w in enumerate(POOL_WINDOWS):
        cs = slice(g * gc, (g + 1) * gc)
        cur = u[:, cs]
        win = cur
        for i in range(1, w):
            win = win + buf[PB - i:PB - i + C, cs]
        cnt = jnp.minimum(pos + 1, w).astype(F32)
        d = win / cnt - cur
        y = jnp.dot(d.astype(BF16), pw_ref[g], preferred_element_type=F32) * ps_ref[:, cs]
        p_scr[:, cs] = y.astype(p_scr.dtype)
    buf[0:PB, :] = buf[C:C + PB, :]


def _ln_silu(x, g, b):
    mu = jnp.mean(x, axis=-1, keepdims=True)
    xc = x - mu
    var = jnp.mean(xc * xc, axis=-1, keepdims=True)
    return _silu(xc * lax.rsqrt(var + EPS) * g + b)


def _conv_chunk(cv_in, cw_ref, cb_ref, lg_ref, lb_ref, p_scr, buf, ph, *, C):
    HB = CONV_HIST
    S = SUBLANES
    rt = CONV_ROWS
    buf[HB:HB + C, :] = cv_in
    for p in range(1, S):
        ph[p - 1, S:HB + C, :] = buf[S - p:HB + C - p, :]
    n_sub = rt // S
    for t in range(C // rt):
        accs = [None] * n_sub
        for back in range(CONV_W):
            a, p = divmod(back, S)
            j = CONV_PAST - back
            wt = cw_ref[j * S:(j + 1) * S, :]
            for q in range(n_sub):
                r0 = HB + t * rt + q * S - a * S
                src = buf[r0:r0 + S, :] if p == 0 else ph[p - 1, r0:r0 + S, :]
                term = src * wt
                accs[q] = term if accs[q] is None else accs[q] + term
        acc = jnp.concatenate(accs, axis=0)
        y = _ln_silu(acc + cb_ref[...], lg_ref[...], lb_ref[...])
        p_scr[t * rt:(t + 1) * rt, :] = y.astype(p_scr.dtype)
    buf[0:HB, :] = buf[C:C + HB, :]


def _mixer_kernel(*refs, mode, H, dk, dv, C, layer, gc, u_off):
    if mode == "gla":
        (q_ref, k_ref, v_ref, r_ref, alr_ref, u_ref, wg_ref, bg_ref, nw_ref, pw_ref, ps_ref,
         wa_ref, wb_ref, x_ref, g1_ref, xo_ref, s_ref, sh_ref,
         st_scr, b_scr, o_scr, p_scr, buf) = refs
        hist0, hist_rows = POOL_PAST + 1, POOL_PAST
    else:
        (q_ref, fz_ref, v_ref, r_ref, ga_ref, gb_ref, lbraw_ref, nw_ref, cw_ref, cb_ref, lg_ref, lb_ref,
         wa_ref, wb_ref, x_ref, g1_ref, xo_ref, s_ref, sh_ref,
         st_scr, b_scr, o_scr, p_scr, buf, ph) = refs
        hist0, hist_rows = CONV_HIST, CONV_PAST
    c = pl.program_id(1)

    @pl.when(c == 0)
    def _():
        st_scr[...] = jnp.zeros_like(st_scr)
        buf[0:hist0, :] = jnp.zeros((hist0, buf.shape[1]), F32)

    if mode == "gla":
        u = _shift_lanes(u_ref[...].astype(F32), u_off, p_scr.shape[1])
        _pool_chunk(u, pw_ref, ps_ref, p_scr, buf, C=C, gc=gc)
    else:
        cv_in = ga_ref[...].astype(F32) * _sigmoid(gb_ref[...].astype(F32))
        _conv_chunk(cv_in, cw_ref, cb_ref, lg_ref, lb_ref, p_scr, buf, ph, C=C)
    y = jnp.dot(p_scr[...], wb_ref[...], preferred_element_type=F32)
    n_grp = 2 if mode == "hgrn" else 1
    hg = H // n_grp
    for grp in range(n_grp):
        heads = tuple(range(grp * hg, (grp + 1) * hg))
        ls = slice(heads[0] * dk, (heads[-1] + 1) * dk)
        vs = slice(heads[0] * dv, (heads[-1] + 1) * dv)
        if mode == "gla":
            q = q_ref[:, ls].astype(F32) * (dk ** -0.5)
            k = k_ref[:, ls].astype(F32)
            g = _gla_log_decay(alr_ref[...], wg_ref[:, ls], bg_ref[:, ls])
        else:
            q = q_ref[:, ls].astype(F32)
            g, k = _hgrn_gates(fz_ref[:, ls], _hgrn_lower_bound(lbraw_ref, layer)[:, ls])
        _rec_chunk(q, k, g, v_ref, r_ref, nw_ref[...], o_scr, st_scr, b_scr,
                   heads=heads, dk=dk, dv=dv, C=C)
        y = y + jnp.dot(o_scr[:, vs], wa_ref[vs, :], preferred_element_type=F32)
    xo_ref[...] = x_ref[...] + g1_ref[...] * y

    @pl.when(c == pl.num_programs(1) - 1)
    def _():
        for h in range(H):
            s_ref[h] = st_scr[h].T
        sh_ref[...] = buf[C + hist0 - hist_rows:C + hist0, :]


def _mixer_prompt(mode, z, zg, cols, widths, consts, w_out, x, mod, layer, el, *, H, dk, dv, hist_len, u_off=0):
    B, T, _ = z.shape
    D = x.shape[-1]
    C = REC_CHUNK
    V = H * dv
    PW = D - V
    assert T % C == 0

    def zspec(width, start):
        return pl.BlockSpec((None, C, width), lambda b, c: (b, c, start // width))

    in_specs = [zspec(w, s) for w, s in zip(widths, cols)]
    args = [z] * len(cols)
    if mode == "hgrn":
        in_specs[1] = pl.BlockSpec((None, C, widths[1]), lambda b, c: (b, c, 0))
        args[1] = zg
    for a in consts:
        in_specs.append(_layer_spec(a, el) if a.ndim >= 3 else _const_spec(a))
        args.append(a)
    in_specs += [pl.BlockSpec((V, D), lambda b, c: (0, 0)),
                 pl.BlockSpec((PW, D), lambda b, c: (V // PW, 0)),
                 pl.BlockSpec((None, C, D), lambda b, c: (b, c, 0)),
                 pl.BlockSpec((None, None, 1, D), lambda b, c: (layer, b, 0, 2))]
    args += [w_out, w_out, x, mod]
    scratch = [pltpu.VMEM((H, dv, dk), F32), pltpu.VMEM((C, H * dk), F32),
               pltpu.VMEM((C, V), BF16), pltpu.VMEM((C, PW), BF16)]
    if mode == "gla":
        scratch.append(pltpu.VMEM((POOL_PAST + 1 + C, PW), F32))
    else:
        scratch += [pltpu.VMEM((CONV_HIST + C, PW), F32),
                    pltpu.VMEM((SUBLANES - 1, CONV_HIST + C, PW), F32)]
    return pl.pallas_call(
        functools.partial(_mixer_kernel, mode=mode, H=H, dk=dk, dv=dv, C=C, layer=el,
                          gc=PW // len(POOL_WINDOWS), u_off=u_off),
        grid=(B, T // C),
        in_specs=in_specs,
        out_specs=[pl.BlockSpec((None, C, D), lambda b, c: (b, c, 0)),
                   pl.BlockSpec((None, H, dk, dv), lambda b, c: (b, 0, 0, 0)),
                   pl.BlockSpec((None, hist_len, PW), lambda b, c: (b, 0, 0))],
        out_shape=[jax.ShapeDtypeStruct((B, T, D), F32),
                   jax.ShapeDtypeStruct((B, H, dk, dv), F32),
                   jax.ShapeDtypeStruct((B, hist_len, PW), F32)],
        scratch_shapes=scratch,
        compiler_params=_params("parallel", "arbitrary"),
        name="mixer_" + mode,
    )(*args)


def _step_recurrence(a, k, q, v_ref, s_ref, so_ref, tile_scr, row_scr, *, H, dk, dv, bb):
    for h in range(H):
        hs = slice(h * dk, (h + 1) * dk)
        tile_scr[...] = jnp.zeros_like(tile_scr)
        tile_scr[0:bb, :] = a[:, hs]
        tile_scr[bb:2 * bb, :] = k[:, hs]
        tile_scr[2 * bb:3 * bb, :] = q[:, hs]
        cols = tile_scr[...].T
        for s in range(bb):
            a_col = cols[:, s:s + 1]
            k_col = cols[:, bb + s:bb + s + 1]
            q_col = cols[:, 2 * bb + s:2 * bb + s + 1]
            v_row = v_ref[s:s + 1, h * dv:(h + 1) * dv]
            s_new = a_col * s_ref[s, h] + k_col * v_row
            so_ref[s, h] = s_new
            row_scr[s:s + 1, h * dv:(h + 1) * dv] = jnp.sum(q_col * s_new, axis=0, keepdims=True)


def _own_slab(ref, el, n_alias):
    if n_alias:
        return ref
    for j in range(ref.shape[0]):
        if j != el:
            ref[j] = jnp.zeros(ref.shape[1:], ref.dtype)
    return ref.at[el]


def _even1_kernel(*refs, H, dk, dv, bb, gc, n_alias, u_off, el):
    (q_ref, k_ref, v_ref, r_ref, alr_ref, ub_ref, sg_ref, sp_ref,
     wg_ref, bg_ref, nw_ref, pw_ref, ps_ref) = refs[:13]
    o_ref, p_ref, sgo_ref, spo_ref, tile_scr, row_scr, d_scr, u_scr = refs[13 + n_alias:]
    sgo_ref, spo_ref = _own_slab(sgo_ref, el, n_alias), _own_slab(spo_ref, el, n_alias)
    g = _gla_log_decay(alr_ref[...], wg_ref[...], bg_ref[...])
    _step_recurrence(jnp.exp(g), k_ref[...], q_ref[...] * (dk ** -0.5), v_ref, sg_ref, sgo_ref,
                     tile_scr, row_scr, H=H, dk=dk, dv=dv, bb=bb)
    nw = nw_ref[...]
    for h in range(H):
        vs = slice(h * dv, (h + 1) * dv)
        o_ref[:, vs] = _head_norm_gate(row_scr[:, vs], nw, r_ref[:, vs]).astype(o_ref.dtype)

    u_scr[...] = _shift_lanes(ub_ref[...], u_off, u_scr.shape[1])
    for s in range(bb):
        for g_i, w in enumerate(POOL_WINDOWS):
            cs = slice(g_i * gc, (g_i + 1) * gc)
            cur = u_scr[s:s + 1, cs]
            win = cur + jnp.sum(sp_ref[s, POOL_PAST - (w - 1):POOL_PAST, cs], axis=0, keepdims=True)
            cnt = float(min(PAST_LEN + 1, w))
            d_scr[s:s + 1, cs] = win / cnt - cur
        spo_ref[s, 0:POOL_PAST - 1, :] = sp_ref[s, 1:POOL_PAST, :]
        spo_ref[s, POOL_PAST - 1:POOL_PAST, :] = u_scr[s:s + 1, :]
    for g_i in range(len(POOL_WINDOWS)):
        cs = slice(g_i * gc, (g_i + 1) * gc)
        y = jnp.dot(d_scr[:, cs].astype(BF16), pw_ref[g_i], preferred_element_type=F32) * ps_ref[:, cs]
        p_ref[:, cs] = y.astype(p_ref.dtype)


def _odd1_kernel(*refs, H, dk, dv, bb, n_alias, el):
    (q_ref, fz_ref, v_ref, r_ref, ga_ref, gb_ref, sh_ref, sc_ref,
     lbraw_ref, nw_ref, cw_ref, cb_ref, lg_ref, lb_ref) = refs[:14]
    o_ref, cv_ref, sho_ref, sco_ref, tile_scr, row_scr, d_scr = refs[14 + n_alias:]
    sho_ref, sco_ref = _own_slab(sho_ref, el, n_alias), _own_slab(sco_ref, el, n_alias)
    g, k = _hgrn_gates(fz_ref[...], _hgrn_lower_bound(lbraw_ref, el))
    _step_recurrence(jnp.exp(g), k, q_ref[...], v_ref, sh_ref, sho_ref,
                     tile_scr, row_scr, H=H, dk=dk, dv=dv, bb=bb)
    nw = nw_ref[...]
    for h in range(H):
        vs = slice(h * dv, (h + 1) * dv)
        o_ref[:, vs] = _head_norm_gate(row_scr[:, vs], nw, r_ref[:, vs]).astype(o_ref.dtype)

    cv_in = ga_ref[...] * _sigmoid(gb_ref[...])
    d_scr[...] = cv_in * cw_ref[CONV_PAST:CONV_W, :]
    for s in range(bb):
        hist = sc_ref[s]
        d_scr[s:s + 1, :] += jnp.sum(hist * cw_ref[0:CONV_PAST, :], axis=0, keepdims=True)
        sco_ref[s, 0:CONV_PAST - 1, :] = sc_ref[s, 1:CONV_PAST, :]
        sco_ref[s, CONV_PAST - 1:CONV_PAST, :] = cv_in[s:s + 1, :]
    y = _ln_silu(d_scr[...] + cb_ref[...], lg_ref[...], lb_ref[...])
    cv_ref[...] = y.astype(cv_ref.dtype)


def _sample_mixer(kernel_fn, name, z, cols, widths, states, prev, consts, el, *, bb, extra_scratch=()):
    Bs = z.shape[0]
    s_mat, s_hist = states
    _, _, H, dk, dv = s_mat.shape
    hist_len, HW = s_hist.shape[2:]
    V = H * dv

    def zspec(width, start):
        return pl.BlockSpec((bb, width), lambda i: (i, start // width))

    mat_spec = pl.BlockSpec((None, bb, H, dk, dv), lambda i: (el, i, 0, 0, 0))
    hist_spec = pl.BlockSpec((None, bb, hist_len, HW), lambda i: (el, i, 0, 0))
    in_specs = [zspec(w, c) for w, c in zip(widths, cols)] + [mat_spec, hist_spec]
    args = [z] * len(cols) + [s_mat, s_hist]
    for a in consts:
        in_specs.append(_layer_spec(a, el) if a.ndim >= 3 else _const_spec(a))
        args.append(a)
    aliases = {}
    if prev is not None:
        for n, p in enumerate(prev):
            aliases[len(args)] = 2 + n
            in_specs.append(pl.BlockSpec(memory_space=pl.ANY))
            args.append(p)
    n_alias = 0 if prev is None else len(prev)
    if prev is None:
        n_lay = s_mat.shape[0]
        mat_out = pl.BlockSpec((n_lay, bb, H, dk, dv), lambda i: (0, i, 0, 0, 0))
        hist_out = pl.BlockSpec((n_lay, bb, hist_len, HW), lambda i: (0, i, 0, 0))
    else:
        mat_out, hist_out = mat_spec, hist_spec
    return pl.pallas_call(
        functools.partial(kernel_fn, H=H, dk=dk, dv=dv, bb=bb, n_alias=n_alias, el=el),
        grid=(Bs // bb,),
        in_specs=in_specs,
        out_specs=[pl.BlockSpec((bb, V), lambda i: (i, 0)),
                   pl.BlockSpec((bb, HW), lambda i: (i, 0)),
                   mat_out, hist_out],
        out_shape=[jax.ShapeDtypeStruct((Bs, V), BF16), jax.ShapeDtypeStruct((Bs, HW), BF16),
                   jax.ShapeDtypeStruct(s_mat.shape, F32), jax.ShapeDtypeStruct(s_hist.shape, F32)],
        scratch_shapes=[pltpu.VMEM((LANES, dk), F32), pltpu.VMEM((bb, V), F32),
                        pltpu.VMEM((bb, HW), F32)] + list(extra_scratch),
        input_output_aliases=aliases,
        compiler_params=_params("parallel"),
        name=name,
    )(*args)


def _pick(m, pref):
    return pref if m % pref == 0 else m


def kernel(x_prompt, x_sample, c_prompt, c_sample, state_gla, state_pool, state_hgrn, state_conv,
           w_ada, b_ada, norm_mix, norm_mlp, norm_final,
           ev_w_in, ev_w_gate_up, ev_b_gate, ev_gla_norm, ev_pool_w, ev_pool_scale, ev_w_out,
           od_w_in, od_lb_raw, od_hgrn_norm, od_conv_w, od_conv_b, od_ln_g, od_ln_b, od_w_out,
           mlp_w_up, mlp_w_down):
    B, T, D = x_prompt.shape
    Bs = x_sample.shape[0]
    depth = w_ada.shape[0]
    gla_w = D // 2
    gla_key_w = gla_w // 2
    gla_dk = gla_key_w // GLA_HEADS
    gla_dv = gla_w // GLA_HEADS
    pool_w = D - gla_w
    hgrn_w = D // 2
    hgrn_heads = hgrn_w // HGRN_DK
    hgrn_dv = hgrn_w // hgrn_heads
    conf_w = D - hgrn_w

    n_main = 2 * gla_key_w + 2 * gla_w
    ub_w = n_main // 2
    assert GLA_GATE_RANK + pool_w <= ub_w
    ev_w = jnp.pad(ev_w_in, ((0, 0), (0, 0), (0, n_main + ub_w - ev_w_in.shape[-1])))
    ev_cols = (0, gla_key_w, 2 * gla_key_w, 2 * gla_key_w + gla_w, n_main, n_main)
    ev_widths = (gla_key_w, gla_key_w, gla_w, gla_w, LANES, ub_w)
    wg = jnp.pad(ev_w_gate_up, ((0, 0), (0, LANES - GLA_GATE_RANK), (0, 0))).astype(BF16)
    od_cols = (0, hgrn_w, 2 * hgrn_w, 3 * hgrn_w, 4 * hgrn_w, 4 * hgrn_w + conf_w)
    od_widths = (hgrn_w, hgrn_w, hgrn_w, hgrn_w, conf_w, conf_w)
    pool_wb = ev_pool_w.astype(BF16)
    conv_wr = jnp.repeat(od_conv_w, SUBLANES, axis=1)
    conv_wp = jnp.pad(od_conv_w, ((0, 0), (0, 1), (0, 0)))

    lay3 = lambda a: a.reshape(a.shape[0], 1, a.shape[1])
    norm_mix3, norm_mlp3 = lay3(norm_mix), lay3(norm_mlp)
    bg3, gn3, ps3 = lay3(ev_b_gate), lay3(ev_gla_norm), lay3(ev_pool_scale)
    hn3, cb3, lg3, lb3 = lay3(od_hgrn_norm), lay3(od_conv_b), lay3(od_ln_g), lay3(od_ln_b)

    pad = (-(B + Bs)) % 16
    c_all = jnp.concatenate([c_sample, c_prompt, jnp.zeros((pad, D), F32)], axis=0)
    mod_s, mod_p = _ada(c_all, w_ada, b_ada, Bs, B)

    xp = x_prompt.reshape(B * T, D)
    xs = x_sample.reshape(Bs, D)
    Mp = B * T
    tm_in = _pick(Mp, 1024)
    tm_p = _pick(Mp, 512)
    bb = 8

    new = {k: [] for k in ("gla_p", "pool_p", "hgrn_p", "conv_p")}
    ev_prev = od_prev = cast = None
    for l in range(depth):
        e = l // 2
        nf = norm_final.reshape(1, D) if l == depth - 1 else None
        kw_p = dict(per_row=False, seq_len=T)
        kw_s = dict(per_row=True, seq_len=1)
        w_in_f, w_out_f = (ev_w, ev_w_out) if l % 2 == 0 else (od_w_in, od_w_out)
        if cast is None:
            w_in_s, w_out_s, w_up_s, w_down_s, wl = w_in_f, w_out_f, mlp_w_up, mlp_w_down, e
        else:
            (w_up_s, w_down_s, w_in_s, w_out_s), wl = cast, None
        if l % 2 == 0:
            n_in = ev_w.shape[-1]
            tn = ub_w
            zs, *w_in_b = _in_proj(xs, norm_mix3, mod_s, w_in_s, l, wl, tm=Bs, tn=tn, out_dtype=F32, **kw_s)
            a_s, b_s, sgs, sps = _sample_mixer(
                functools.partial(_even1_kernel, gc=pool_w // len(POOL_WINDOWS), u_off=GLA_GATE_RANK),
                "even_sample", zs, ev_cols, ev_widths, (state_gla, state_pool), ev_prev,
                (wg, bg3, gn3, pool_wb, ps3), e, bb=bb, extra_scratch=[pltpu.VMEM((bb, pool_w), F32)])
            ev_prev = (sgs, sps)
        else:
            n_in = od_w_in.shape[-1]
            tn = _pick(n_in, 1024)
            zs, *w_in_b = _in_proj(xs, norm_mix3, mod_s, w_in_s, l, wl, tm=Bs, tn=tn, out_dtype=F32, **kw_s)
            a_s, b_s, shs, scs = _sample_mixer(
                _odd1_kernel, "odd_sample",
                zs, od_cols, od_widths, (state_hgrn, state_conv), od_prev,
                (od_lb_raw, hn3, conv_wp, cb3, lg3, lb3), e, bb=bb)
            od_prev = (shs, scs)
        xs, *w_out_b = _out_proj_sample(a_s, b_s, w_out_s, wl, xs, mod_s, l, tn=_pick(D, 512))
        xs, *w_mlp_b = _mlp(xs, norm_mlp3, mod_s, w_up_s, w_down_s, l, nf, tm=Bs, tf=512, **kw_s)
        if cast is None:
            (w_in_b,), (w_out_b,), (w_up_b, w_down_b) = w_in_b, w_out_b, w_mlp_b
        else:
            w_up_b, w_down_b, w_in_b, w_out_b = cast

        xp3 = xp.reshape(B, T, D)
        if l % 2 == 0:
            (zp,) = _in_proj(xp, norm_mix3, mod_p, w_in_b, l, None, tm=tm_in, tn=tn, out_dtype=BF16, **kw_p)
            xp3, sg, sp = _mixer_prompt("gla", zp.reshape(B, T, n_in), None, ev_cols, ev_widths,
                                        (wg, bg3, gn3, pool_wb, ps3), w_out_b, xp3, mod_p, l, e,
                                        H=GLA_HEADS, dk=gla_dk, dv=gla_dv, hist_len=POOL_PAST,
                                        u_off=GLA_GATE_RANK)
            new["gla_p"].append(sg)
            new["pool_p"].append(sp)
        else:
            zp, zg = _in_proj(xp, norm_mix3, mod_p, w_in_b, l, None, tm=tm_in, tn=tn, out_dtype=BF16,
                              gate_block=od_cols[1] // tn, **kw_p)
            xp3, shg, scv = _mixer_prompt("hgrn", zp.reshape(B, T, n_in), zg.reshape(B, T, tn),
                                          od_cols, od_widths, (od_lb_raw, hn3, conv_wr, cb3, lg3, lb3),
                                          w_out_b, xp3, mod_p, l, e,
                                          H=hgrn_heads, dk=HGRN_DK, dv=hgrn_dv, hist_len=CONV_PAST)
            new["hgrn_p"].append(shg)
            new["conv_p"].append(scv)
        nxt = ()
        if l + 1 < depth:
            e_n = (l + 1) // 2
            w_in_n, w_out_n = (ev_w, ev_w_out) if (l + 1) % 2 == 0 else (od_w_in, od_w_out)
            nxt = ((mlp_w_up, l + 1), (mlp_w_down, l + 1), (w_in_n, e_n), (w_out_n, e_n))
        xp, *cast = _mlp(xp3.reshape(Mp, D), norm_mlp3, mod_p, w_up_b, w_down_b, l, nf,
                         tm=tm_p, tf=1024, cast_next=nxt, **kw_p)
        cast = cast or None

    st = lambda k: jnp.stack(new[k])
    return (xp.reshape(B, T, D), xs.reshape(Bs, 1, D),
            st("gla_p"), st("pool_p"), st("hgrn_p"), st("conv_p"),
            ev_prev[0], ev_prev[1], od_prev[0], od_prev[1])
```

```python
import functools

import jax
import jax.numpy as jnp
from jax import lax
from jax.experimental import pallas as pl
from jax.experimental.pallas import tpu as pltpu

F32 = jnp.float32
BF16 = jnp.bfloat16

EPS = 1e-6
MIN_GATE = 1e-30
PAST_LEN = 16384
GLA_HEADS = 4
GLA_GATE_RANK = 16
GLA_GATE_TEMP = 16.0
POOL_WINDOWS = (2, 4, 8, 16)
POOL_PAST = max(POOL_WINDOWS) - 1
CONV_W = 31
CONV_PAST = CONV_W - 1
HGRN_DK = 128
LANES = 128
SUBLANES = 8
REC_CHUNK = 128
CONV_HIST = 32
CONV_ROWS = 32
VMEM_LIMIT = 56 * 1024 * 1024

_NT = (((1,), (1,)), ((), ()))


def _params(*sem):
    return pltpu.CompilerParams(dimension_semantics=sem, vmem_limit_bytes=VMEM_LIMIT)


def _sigmoid(x):
    return 0.5 * jnp.tanh(0.5 * x) + 0.5


def _sigmoid_pair(x):
    t = jnp.exp(-jnp.abs(x))
    r = 1.0 / (1.0 + t)
    tr = t * r
    pos = x >= 0.0
    return jnp.where(pos, r, tr), jnp.where(pos, tr, r)


def _silu(x):
    return x * _sigmoid(x)


def _log_sigmoid(x):
    return jnp.minimum(x, 0.0) - jnp.log(1.0 + jnp.exp(-jnp.abs(x)))


def _rms_mod(x, g, sc, sh):
    ms = jnp.mean(x * x, axis=-1, keepdims=True)
    return (x * lax.rsqrt(ms + EPS) * g) * (1.0 + sc) + sh


def _head_norm_gate(o, nw, r):
    ms = jnp.mean(o * o, axis=-1, keepdims=True)
    return o * lax.rsqrt(ms + EPS) * nw * _silu(r)


def _shift_lanes(x, off, width):
    if off == 0:
        return x[:, :width]
    return pltpu.roll(x, x.shape[1] - off, 1)[:, :width]


def _const_spec(a):
    return pl.BlockSpec(a.shape, lambda *_: (0,) * a.ndim)


def _layer_spec(a, layer):
    return pl.BlockSpec((None,) + a.shape[1:], lambda *_: (layer,) + (0,) * (a.ndim - 1))


def _ada_kernel(c_ref, w_ref, b_ref, ms_ref, mp_ref, *, n_s, n_p):
    act = _silu(c_ref[...]).astype(BF16)
    res = jnp.dot(act, w_ref[...].astype(BF16), preferred_element_type=F32) + b_ref[...]
    ms_ref[...] = res[:n_s]
    for i in range(n_p):
        mp_ref[i] = res[n_s + i:n_s + i + 1]


def _ada(c_all, w_ada, b_ada, n_s, n_p, tn=1024):
    L, D, N = w_ada.shape
    R = c_all.shape[0]
    return pl.pallas_call(
        functools.partial(_ada_kernel, n_s=n_s, n_p=n_p),
        grid=(L, N // tn),
        in_specs=[pl.BlockSpec((R, D), lambda l, j: (0, 0)),
                  pl.BlockSpec((None, D, tn), lambda l, j: (l, 0, j)),
                  pl.BlockSpec((None, 1, tn), lambda l, j: (l, 0, j))],
        out_specs=[pl.BlockSpec((None, n_s, tn), lambda l, j: (l, 0, j)),
                   pl.BlockSpec((None, n_p, 1, tn), lambda l, j: (l, 0, 0, j))],
        out_shape=[jax.ShapeDtypeStruct((L, n_s, N), F32),
                   jax.ShapeDtypeStruct((L, n_p, 1, N), F32)],
        compiler_params=_params("parallel", "parallel"),
        name="ada_mod",
    )(c_all, w_ada, b_ada.reshape(L, 1, N))


def _mod_spec(per_row, layer, tm, D, seq_len, part):
    if per_row:
        return pl.BlockSpec((None, tm, D), lambda i, *_: (layer, i, part))
    return pl.BlockSpec((None, None, 1, D), lambda i, *_: (layer, (i * tm) // seq_len, 0, part))


def _row_chunks(tm):
    return 4 if tm % 512 == 0 else 1


def _rows(ref, rows):
    return ref[...] if ref.shape[0] == 1 else ref[rows, :]


def _w_spec(w, wl, block, index):
    if w.ndim == 2:
        return pl.BlockSpec(block, index)
    return pl.BlockSpec((None,) + block, lambda *g: (wl,) + index(*g))


def _in_kernel(*refs, gate_block, n_chunks, cast_w):
    x_ref, g_ref, sc_ref, sh_ref, w_ref, z_ref = refs[:6]
    rest = list(refs[6:])
    zg_ref = rest.pop(0) if gate_block is not None else None
    wb_ref = rest.pop(0) if cast_w else None
    (h_scr,) = rest
    j = pl.program_id(1)
    rc = x_ref.shape[0] // n_chunks
    if cast_w:
        wb_ref[...] = w_ref[...].astype(BF16)
        w_ref = wb_ref

    @pl.when(j == 0)
    def _():
        for r in range(n_chunks):
            rows = slice(r * rc, (r + 1) * rc)
            h = _rms_mod(x_ref[rows, :], g_ref[...], _rows(sc_ref, rows), _rows(sh_ref, rows)).astype(BF16)
            h_scr[rows, :] = h
            z_ref[rows, :] = jnp.dot(h, w_ref[...], preferred_element_type=F32).astype(z_ref.dtype)

    @pl.when(j != 0)
    def _():
        acc = jnp.dot(h_scr[...], w_ref[...], preferred_element_type=F32)
        z_ref[...] = acc.astype(z_ref.dtype)
        if gate_block is not None:
            @pl.when(j == gate_block)
            def _():
                zg_ref[...] = acc


def _in_proj(x, norm_g, mod, w, layer, wl, *, tm, tn, per_row, seq_len, out_dtype, gate_block=None):
    M, D = x.shape
    N = w.shape[-1]
    cast_w = w.dtype != BF16
    assert gate_block != 0 and N // tn >= 2
    out_specs = [pl.BlockSpec((tm, tn), lambda i, j: (i, j))]
    out_shape = [jax.ShapeDtypeStruct((M, N), out_dtype)]
    if gate_block is not None:
        out_specs.append(pl.BlockSpec((tm, tn), lambda i, j: (i, 0)))
        out_shape.append(jax.ShapeDtypeStruct((M, tn), F32))
    if cast_w:
        assert M == tm
        out_specs.append(pl.BlockSpec((D, tn), lambda i, j: (0, j)))
        out_shape.append(jax.ShapeDtypeStruct((D, N), BF16))
    return pl.pallas_call(
        functools.partial(_in_kernel, gate_block=gate_block, n_chunks=_row_chunks(tm), cast_w=cast_w),
        grid=(M // tm, N // tn),
        in_specs=[pl.BlockSpec((tm, D), lambda i, j: (i, 0)),
                  _layer_spec(norm_g, layer),
                  _mod_spec(per_row, layer, tm, D, seq_len, 1),
                  _mod_spec(per_row, layer, tm, D, seq_len, 0),
                  _w_spec(w, wl, (D, tn), lambda i, j: (0, j))],
        out_specs=out_specs,
        out_shape=out_shape,
        scratch_shapes=[pltpu.VMEM((tm, D), BF16)],
        compiler_params=_params("parallel", "arbitrary"),
        name="in_proj",
    )(x, norm_g, mod, mod, w)


def _out_kernel(a_ref, b_ref, w_ref, x_ref, g_ref, o_ref, *wb_ref):
    ka = a_ref.shape[1]
    w = w_ref[...]
    if wb_ref:
        w = w.astype(BF16)
        wb_ref[0][...] = w
    y = jnp.dot(a_ref[...], w[:ka], preferred_element_type=F32)
    y = y + jnp.dot(b_ref[...], w[ka:], preferred_element_type=F32)
    o_ref[...] = x_ref[...] + g_ref[...] * y


def _out_proj_sample(a, b, w_out, wl, x, mod, layer, *, tn):
    M, D = x.shape
    Ka, Kb = a.shape[1], b.shape[1]
    nj = D // tn
    out_specs = [pl.BlockSpec((M, tn), lambda j: (0, j))]
    out_shape = [jax.ShapeDtypeStruct((M, D), F32)]
    if w_out.dtype != BF16:
        out_specs.append(pl.BlockSpec((Ka + Kb, tn), lambda j: (0, j)))
        out_shape.append(jax.ShapeDtypeStruct((Ka + Kb, D), BF16))
    return pl.pallas_call(
        _out_kernel,
        grid=(nj,),
        in_specs=[pl.BlockSpec((M, Ka), lambda j: (0, 0)),
                  pl.BlockSpec((M, Kb), lambda j: (0, 0)),
                  _w_spec(w_out, wl, (Ka + Kb, tn), lambda j: (0, j)),
                  pl.BlockSpec((M, tn), lambda j: (0, j)),
                  pl.BlockSpec((None, M, tn), lambda j: (layer, 0, 2 * nj + j))],
        out_specs=out_specs,
        out_shape=out_shape,
        compiler_params=_params("parallel"),
        name="out_proj_sample",
    )(a, b, w_out, x, mod)


def _mlp_kernel(*refs, final_norm, n_chunks, cast_w, n_next):
    x_ref, g_ref, sc_ref, sh_ref, gate_ref, wu_ref, wd_ref = refs[:7]
    rest = list(refs[7:])
    gf_ref = rest.pop(0) if final_norm else None
    next_in = [rest.pop(0) for _ in range(n_next)]
    o_ref = rest.pop(0)
    wub_ref, wdb_ref = (rest.pop(0), rest.pop(0)) if cast_w else (None, None)
    next_out = [rest.pop(0) for _ in range(n_next)]
    h_scr, acc_scr = rest
    f = pl.program_id(1)
    last = pl.num_programs(1) - 1
    rc = x_ref.shape[0] // n_chunks
    if cast_w:
        wub_ref[...] = wu_ref[...].astype(BF16)
        wdb_ref[...] = wd_ref[...].astype(BF16)
        wu_ref, wd_ref = wub_ref, wdb_ref

    def cast_next():
        for src, dst in zip(next_in, next_out):
            dst[...] = src[...].astype(BF16)

    def up_down(h):
        a = jnp.dot(h, wu_ref[...], preferred_element_type=F32)
        a = jnp.square(jnp.maximum(a, 0.0)).astype(BF16)
        return jnp.dot(a, wd_ref[...], preferred_element_type=F32)

    @pl.when(f == 0)
    def _():
        cast_next()
        for r in range(n_chunks):
            rows = slice(r * rc, (r + 1) * rc)
            h = _rms_mod(x_ref[rows, :], g_ref[...], _rows(sc_ref, rows), _rows(sh_ref, rows)).astype(BF16)
            h_scr[rows, :] = h
            acc_scr[rows, :] = up_down(h)

    @pl.when((f != 0) & (f != last))
    def _():
        cast_next()
        acc_scr[...] += up_down(h_scr[...])

    @pl.when(f == last)
    def _():
        cast_next()
        for r in range(n_chunks):
            rows = slice(r * rc, (r + 1) * rc)
            y = x_ref[rows, :] + _rows(gate_ref, rows) * (acc_scr[rows, :] + up_down(h_scr[rows, :]))
            if final_norm:
                ms = jnp.mean(y * y, axis=-1, keepdims=True)
                y = y * lax.rsqrt(ms + EPS) * gf_ref[...]
            o_ref[rows, :] = y


def _mlp(x, norm_g, mod, w_up, w_down, layer, norm_final, *, tm, tf, per_row, seq_len, cast_next=()):
    M, D = x.shape
    FF = w_up.shape[-1]
    final_norm = norm_final is not None
    cast_w = w_up.dtype != BF16
    assert FF // tf >= 2
    nf = FF // tf
    n_steps = (M // tm) * nf
    in_specs = [pl.BlockSpec((tm, D), lambda i, f: (i, 0)),
                _layer_spec(norm_g, layer),
                _mod_spec(per_row, layer, tm, D, seq_len, 4),
                _mod_spec(per_row, layer, tm, D, seq_len, 3),
                _mod_spec(per_row, layer, tm, D, seq_len, 5),
                _w_spec(w_up, layer, (D, tf), lambda i, f: (0, f)),
                _w_spec(w_down, layer, (tf, D), lambda i, f: (f, 0))]
    args = [x, norm_g, mod, mod, mod, w_up, w_down]
    if final_norm:
        in_specs.append(_const_spec(norm_final))
        args.append(norm_final)
    out_specs = [pl.BlockSpec((tm, D), lambda i, f: (i, 0))]
    out_shape = [jax.ShapeDtypeStruct((M, D), F32)]
    if cast_w:
        assert M == tm
        out_specs += [pl.BlockSpec((D, tf), lambda i, f: (0, f)), pl.BlockSpec((tf, D), lambda i, f: (f, 0))]
        out_shape += [jax.ShapeDtypeStruct((D, FF), BF16), jax.ShapeDtypeStruct((FF, D), BF16)]
    for w_next, wl in cast_next:
        _, R, Cn = w_next.shape
        rows = R // n_steps
        assert rows * n_steps == R and rows % 16 == 0
        in_specs.append(pl.BlockSpec((None, rows, Cn), lambda i, f, wl=wl: (wl, i * nf + f, 0)))
        args.append(w_next)
        out_specs.append(pl.BlockSpec((rows, Cn), lambda i, f: (i * nf + f, 0)))
        out_shape.append(jax.ShapeDtypeStruct((R, Cn), BF16))
    return pl.pallas_call(
        functools.partial(_mlp_kernel, final_norm=final_norm, n_chunks=_row_chunks(tm), cast_w=cast_w,
                          n_next=len(cast_next)),
        grid=(M // tm, FF // tf),
        in_specs=in_specs,
        out_specs=out_specs,
        out_shape=out_shape,
        scratch_shapes=[pltpu.VMEM((tm, D), BF16), pltpu.VMEM((tm, D), F32)],
        compiler_params=_params("parallel", "arbitrary"),
        name="mlp",
    )(*args)


def _gla_log_decay(alr, wg, bg):
    pre = jnp.dot(alr.astype(BF16), wg, preferred_element_type=F32) + bg
    return _log_sigmoid(pre) / GLA_GATE_TEMP


def _hgrn_lower_bound(lbraw_ref, layer):
    rows = [lbraw_ref[j:j + 1, :] for j in range(lbraw_ref.shape[0])]
    mx = functools.reduce(jnp.maximum, rows)
    es = [jnp.exp(r - mx) for r in rows]
    tot = functools.reduce(lambda a, b: a + b, es)
    sm = [e / tot for e in es]
    cum = functools.reduce(lambda a, b: a + b, sm[:layer + 1])
    return cum - sm[0]


def _hgrn_gates(fz, lb):
    sig, sig_neg = _sigmoid_pair(fz)
    f = lb + (1.0 - lb) * sig
    g = jnp.log(jnp.maximum(f, MIN_GATE))
    k = (1.0 - lb) * sig_neg
    return g, k


def _level_ref(b_scr, ls, m, C):
    W = ls.stop - ls.start
    if 2 * m >= 2 * SUBLANES:
        pieces = [jnp.broadcast_to(b_scr[n * 2 * m + m - 1:n * 2 * m + m, ls], (2 * m, W))
                  for n in range(C // (2 * m))]
    elif m == 4:
        pieces = [jnp.broadcast_to(b_scr[j * 8 + 3:j * 8 + 4, ls], (8, W)) for j in range(C // 8)]
    else:
        sub = lax.broadcasted_iota(jnp.int32, (8, W), 0)
        pieces = [jnp.where(sub < 4,
                            jnp.broadcast_to(b_scr[j * 8 + 1:j * 8 + 2, ls], (8, W)),
                            jnp.broadcast_to(b_scr[j * 8 + 5:j * 8 + 6, ls], (8, W)))
                  for j in range(C // 8)]
    return jnp.concatenate(pieces, axis=0)


def _rec_chunk(q, k, g, v_ref, r_ref, nw, o_scr, st_scr, b_scr, *, heads, dk, dv, C):
    W = len(heads) * dk
    ls = slice(heads[0] * dk, heads[0] * dk + W)

    row = lax.broadcasted_iota(jnp.int32, (C, C), 0)
    col = lax.broadcasted_iota(jnp.int32, (C, C), 1)
    tril = jnp.where(row >= col, 1.0, 0.0).astype(BF16)
    g1 = g.astype(BF16)
    rem = g - g1.astype(F32)
    g2 = rem.astype(BF16)
    g3 = (rem - g2.astype(F32)).astype(BF16)
    b = (jnp.dot(tril, g1, preferred_element_type=F32)
         + jnp.dot(tril, g2, preferred_element_type=F32)
         + jnp.dot(tril, g3, preferred_element_type=F32))
    b_scr[:, ls] = b

    H = len(heads)
    scores = [None] * H
    levels = []
    m = C // 2
    while m >= 1:
        levels.append(m)
        m //= 2
    sub = lax.broadcasted_iota(jnp.int32, (SUBLANES, W), 0)
    for m in levels + [0]:
        if m == 0:
            qe, ke = q.astype(BF16), k.astype(BF16)
            mask = row == col
        else:
            if m >= SUBLANES:
                src = jnp.concatenate(
                    [a[n * 2 * m + o:n * 2 * m + o + m] for n in range(C // (2 * m))
                     for a, o in ((k, 0), (q, m))], axis=0)
            else:
                upper = (sub & (2 * m - 1)) >= m
                src = jnp.concatenate(
                    [jnp.where(upper, q[j * SUBLANES:(j + 1) * SUBLANES],
                               k[j * SUBLANES:(j + 1) * SUBLANES])
                     for j in range(C // SUBLANES)], axis=0)
            if m == 1:
                odd = (sub & 1) == 1
                e = jnp.concatenate(
                    [jnp.where(odd, jnp.exp(g[j * SUBLANES:(j + 1) * SUBLANES]), 1.0)
                     for j in range(C // SUBLANES)], axis=0)
            else:
                e = jnp.exp(-jnp.abs(b - _level_ref(b_scr, ls, m, C)))
            qe = ke = (src * e).astype(BF16)
            sh = (2 * m).bit_length() - 1
            mask = (((row >> sh) == (col >> sh))
                    & ((row & (2 * m - 1)) >= m) & ((col & (2 * m - 1)) < m))
        for h in range(H):
            hs = slice(h * dk, (h + 1) * dk)
            p = lax.dot_general(qe[:, hs], ke[:, hs], _NT, preferred_element_type=F32)
            scores[h] = jnp.where(mask, p, 0.0 if scores[h] is None else scores[h])

    b_end = b_scr[C - 1:C, ls]
    qi = (q * jnp.exp(b)).astype(BF16)
    kend = (k * jnp.exp(b_end - b)).astype(BF16)
    a_end = jnp.exp(b_end)
    for h, head in enumerate(heads):
        hs = slice(h * dk, (h + 1) * dk)
        vs = slice(head * dv, (head + 1) * dv)
        v_h = v_ref[:, vs]
        st = st_scr[head]
        o = lax.dot_general(qi[:, hs], st.astype(BF16), _NT, preferred_element_type=F32)
        o = o + jnp.dot(scores[h].astype(BF16), v_h, preferred_element_type=F32)
        st_scr[head] = st * a_end[:, hs] + jnp.dot(v_h.astype(F32).T.astype(BF16), kend[:, hs],
                                                   preferred_element_type=F32)
        o_scr[:, vs] = _head_norm_gate(o, nw, r_ref[:, vs].astype(F32)).astype(o_scr.dtype)


def _pool_chunk(u, pw_ref, ps_ref, p_scr, buf, *, C, gc, chunk):
    PB = POOL_PAST + 1
    buf[PB:PB + C, :] = u
    pos = chunk * C + lax.broadcasted_iota(jnp.int32, (C, 1), 0)
    for g, w in enumerate(POOL_WINDOWS):
        cs = slice(g * gc, (g + 1) * gc)
        cur = u[:, cs]
        win = cur
        for i in range(1, w):
            win = win + buf[PB - i:PB - i + C, cs]
        cnt = jnp.minimum(pos + 1, w).astype(F32)
        d = win / cnt - cur
        y = jnp.dot(d.astype(BF16), pw_ref[g], preferred_element_type=F32) * ps_ref[:, cs]
        p_scr[:, cs] = y.astype(p_scr.dtype)
    buf[0:PB, :] = buf[C:C + PB, :]


def _ln_silu(x, g, b):
    mu = jnp.mean(x, axis=-1, keepdims=True)
    xc = x - mu
    var = jnp.mean(xc * xc, axis=-1, keepdims=True)
    return _silu(xc * lax.rsqrt(var + EPS) * g + b)


def _conv_chunk(cv_in, cw_ref, cb_ref, lg_ref, lb_ref, p_scr, buf, ph, *, C):
    HB = CONV_HIST
    S = SUBLANES
    rt = CONV_ROWS
    buf[HB:HB + C, :] = cv_in
    for p in range(1, S):
        ph[p - 1, S:HB + C, :] = buf[S - p:HB + C - p, :]
    n_sub = rt // S
    for t in range(C // rt):
        accs = [None] * n_sub
        for back in range(CONV_W):
            a, p = divmod(back, S)
            j = CONV_PAST - back
            wt = cw_ref[j * S:(j + 1) * S, :]
            for q in range(n_sub):
                r0 = HB + t * rt + q * S - a * S
                src = buf[r0:r0 + S, :] if p == 0 else ph[p - 1, r0:r0 + S, :]
                term = src * wt
                accs[q] = term if accs[q] is None else accs[q] + term
        acc = jnp.concatenate(accs, axis=0)
        y = _ln_silu(acc + cb_ref[...], lg_ref[...], lb_ref[...])
        p_scr[t * rt:(t + 1) * rt, :] = y.astype(p_scr.dtype)
    buf[0:HB, :] = buf[C:C + HB, :]


def _mixer_kernel(*refs, mode, H, dk, dv, C, n_sub, layer, gc, u_off, cols, widths):
    if mode == "gla":
        (z_ref, wg_ref, bg_ref, nw_ref, pw_ref, ps_ref,
         wa_ref, wb_ref, x_ref, g1_ref, xo_ref, s_ref, sh_ref,
         st_scr, b_scr, o_scr, p_scr, buf) = refs
        hist0, hist_rows = POOL_PAST + 1, POOL_PAST
    else:
        (z_ref, zg_ref, lbraw_ref, nw_ref, cw_ref, cb_ref, lg_ref, lb_ref,
         wa_ref, wb_ref, x_ref, g1_ref, xo_ref, s_ref, sh_ref,
         st_scr, b_scr, o_scr, p_scr, buf, ph) = refs
        hist0, hist_rows = CONV_HIST, CONV_PAST
    c = pl.program_id(1)

    @pl.when(c == 0)
    def _():
        st_scr[...] = jnp.zeros_like(st_scr)
        buf[0:hist0, :] = jnp.zeros((hist0, buf.shape[1]), F32)

    for sub in range(n_sub):
        rows = slice(sub * C, (sub + 1) * C)
        zc = [z_ref.at[rows, s:s + w] for s, w in zip(cols, widths)]
        if mode == "gla":
            q_ref, k_ref, v_ref, r_ref, alr_ref, u_ref = zc
            u = _shift_lanes(u_ref[...].astype(F32), u_off, p_scr.shape[1])
            _pool_chunk(u, pw_ref, ps_ref, p_scr, buf, C=C, gc=gc, chunk=c * n_sub + sub)
        else:
            q_ref, _, v_ref, r_ref, ga_ref, gb_ref = zc
            fz_ref = zg_ref.at[rows, :]
            cv_in = ga_ref[...].astype(F32) * _sigmoid(gb_ref[...].astype(F32))
            _conv_chunk(cv_in, cw_ref, cb_ref, lg_ref, lb_ref, p_scr, buf, ph, C=C)
        y = jnp.dot(p_scr[...], wb_ref[...], preferred_element_type=F32)
        n_grp = 2 if mode == "hgrn" else 1
        hg = H // n_grp
        for grp in range(n_grp):
            heads = tuple(range(grp * hg, (grp + 1) * hg))
            ls = slice(heads[0] * dk, (heads[-1] + 1) * dk)
            vs = slice(heads[0] * dv, (heads[-1] + 1) * dv)
            if mode == "gla":
                q = q_ref[:, ls].astype(F32) * (dk ** -0.5)
                k = k_ref[:, ls].astype(F32)
                g = _gla_log_decay(alr_ref[...], wg_ref[:, ls], bg_ref[:, ls])
            else:
                q = q_ref[:, ls].astype(F32)
                g, k = _hgrn_gates(fz_ref[:, ls], _hgrn_lower_bound(lbraw_ref, layer)[:, ls])
            _rec_chunk(q, k, g, v_ref, r_ref, nw_ref[...], o_scr, st_scr, b_scr,
                       heads=heads, dk=dk, dv=dv, C=C)
            y = y + jnp.dot(o_scr[:, vs], wa_ref[vs, :], preferred_element_type=F32)
        xo_ref[rows, :] = x_ref[rows, :] + g1_ref[...] * y

    @pl.when(c == pl.num_programs(1) - 1)
    def _():
        for h in range(H):
            s_ref[h] = st_scr[h].T
        sh_ref[...] = buf[C + hist0 - hist_rows:C + hist0, :]


def _mixer_prompt(mode, z, zg, cols, widths, consts, w_out, x, mod, layer, el, *, H, dk, dv, hist_len, u_off=0):
    B, T, _ = z.shape
    D = x.shape[-1]
    C = REC_CHUNK
    V = H * dv
    PW = D - V
    assert T % C == 0
    n_sub = 2 if T % (2 * C) == 0 else 1
    R = C * n_sub

    in_specs = [pl.BlockSpec((None, R, z.shape[-1]), lambda b, c: (b, c, 0))]
    args = [z]
    if mode == "hgrn":
        in_specs.append(pl.BlockSpec((None, R, zg.shape[-1]), lambda b, c: (b, c, 0)))
        args.append(zg)
    for a in consts:
        in_specs.append(_layer_spec(a, el) if a.ndim >= 3 else _const_spec(a))
        args.append(a)
    in_specs += [pl.BlockSpec((V, D), lambda b, c: (0, 0)),
                 pl.BlockSpec((PW, D), lambda b, c: (V // PW, 0)),
                 pl.BlockSpec((None, R, D), lambda b, c: (b, c, 0)),
                 pl.BlockSpec((None, None, 1, D), lambda b, c: (layer, b, 0, 2))]
    args += [w_out, w_out, x, mod]
    scratch = [pltpu.VMEM((H, dv, dk), F32), pltpu.VMEM((C, H * dk), F32),
               pltpu.VMEM((C, V), BF16), pltpu.VMEM((C, PW), BF16)]
    if mode == "gla":
        scratch.append(pltpu.VMEM((POOL_PAST + 1 + C, PW), F32))
    else:
        scratch += [pltpu.VMEM((CONV_HIST + C, PW), F32),
                    pltpu.VMEM((SUBLANES - 1, CONV_HIST + C, PW), F32)]
    return pl.pallas_call(
        functools.partial(_mixer_kernel, mode=mode, H=H, dk=dk, dv=dv, C=C, n_sub=n_sub, layer=el,
                          gc=PW // len(POOL_WINDOWS), u_off=u_off, cols=cols, widths=widths),
        grid=(B, T // R),
        in_specs=in_specs,
        out_specs=[pl.BlockSpec((None, R, D), lambda b, c: (b, c, 0)),
                   pl.BlockSpec((None, H, dk, dv), lambda b, c: (b, 0, 0, 0)),
                   pl.BlockSpec((None, hist_len, PW), lambda b, c: (b, 0, 0))],
        out_shape=[jax.ShapeDtypeStruct((B, T, D), F32),
                   jax.ShapeDtypeStruct((B, H, dk, dv), F32),
                   jax.ShapeDtypeStruct((B, hist_len, PW), F32)],
        scratch_shapes=scratch,
        compiler_params=_params("parallel", "arbitrary"),
        name="mixer_" + mode,
    )(*args)


def _step_recurrence(a, k, q, v_ref, s_ref, so_ref, tile_scr, row_scr, *, H, dk, dv, bb):
    for h in range(H):
        hs = slice(h * dk, (h + 1) * dk)
        tile_scr[...] = jnp.zeros_like(tile_scr)
        tile_scr[0:bb, :] = a[:, hs]
        tile_scr[bb:2 * bb, :] = k[:, hs]
        tile_scr[2 * bb:3 * bb, :] = q[:, hs]
        cols = tile_scr[...].T
        for s in range(bb):
            a_col = cols[:, s:s + 1]
            k_col = cols[:, bb + s:bb + s + 1]
            q_col = cols[:, 2 * bb + s:2 * bb + s + 1]
            v_row = v_ref[s:s + 1, h * dv:(h + 1) * dv]
            s_new = a_col * s_ref[s, h] + k_col * v_row
            so_ref[s, h] = s_new
            row_scr[s:s + 1, h * dv:(h + 1) * dv] = jnp.sum(q_col * s_new, axis=0, keepdims=True)


def _own_slab(ref, el, n_alias):
    if n_alias:
        return ref
    for j in range(ref.shape[0]):
        if j != el:
            ref[j] = jnp.zeros(ref.shape[1:], ref.dtype)
    return ref.at[el]


def _even1_kernel(*refs, H, dk, dv, bb, gc, n_alias, u_off, el):
    (q_ref, k_ref, v_ref, r_ref, alr_ref, ub_ref, sg_ref, sp_ref,
     wg_ref, bg_ref, nw_ref, pw_ref, ps_ref) = refs[:13]
    o_ref, p_ref, sgo_ref, spo_ref, tile_scr, row_scr, d_scr, u_scr = refs[13 + n_alias:]
    sgo_ref, spo_ref = _own_slab(sgo_ref, el, n_alias), _own_slab(spo_ref, el, n_alias)
    g = _gla_log_decay(alr_ref[...], wg_ref[...], bg_ref[...])
    _step_recurrence(jnp.exp(g), k_ref[...], q_ref[...] * (dk ** -0.5), v_ref, sg_ref, sgo_ref,
                     tile_scr, row_scr, H=H, dk=dk, dv=dv, bb=bb)
    nw = nw_ref[...]
    for h in range(H):
        vs = slice(h * dv, (h + 1) * dv)
        o_ref[:, vs] = _head_norm_gate(row_scr[:, vs], nw, r_ref[:, vs]).astype(o_ref.dtype)

    u_scr[...] = _shift_lanes(ub_ref[...], u_off, u_scr.shape[1])
    for s in range(bb):
        for g_i, w in enumerate(POOL_WINDOWS):
            cs = slice(g_i * gc, (g_i + 1) * gc)
            cur = u_scr[s:s + 1, cs]
            win = cur + jnp.sum(sp_ref[s, POOL_PAST - (w - 1):POOL_PAST, cs], axis=0, keepdims=True)
            cnt = float(min(PAST_LEN + 1, w))
            d_scr[s:s + 1, cs] = win / cnt - cur
        spo_ref[s, 0:POOL_PAST - 1, :] = sp_ref[s, 1:POOL_PAST, :]
        spo_ref[s, POOL_PAST - 1:POOL_PAST, :] = u_scr[s:s + 1, :]
    for g_i in range(len(POOL_WINDOWS)):
        cs = slice(g_i * gc, (g_i + 1) * gc)
        y = jnp.dot(d_scr[:, cs].astype(BF16), pw_ref[g_i], preferred_element_type=F32) * ps_ref[:, cs]
        p_ref[:, cs] = y.astype(p_ref.dtype)


def _odd1_kernel(*refs, H, dk, dv, bb, n_alias, el):
    (q_ref, fz_ref, v_ref, r_ref, ga_ref, gb_ref, sh_ref, sc_ref,
     lbraw_ref, nw_ref, cw_ref, cb_ref, lg_ref, lb_ref) = refs[:14]
    o_ref, cv_ref, sho_ref, sco_ref, tile_scr, row_scr, d_scr = refs[14 + n_alias:]
    sho_ref, sco_ref = _own_slab(sho_ref, el, n_alias), _own_slab(sco_ref, el, n_alias)
    g, k = _hgrn_gates(fz_ref[...], _hgrn_lower_bound(lbraw_ref, el))
    _step_recurrence(jnp.exp(g), k, q_ref[...], v_ref, sh_ref, sho_ref,
                     tile_scr, row_scr, H=H, dk=dk, dv=dv, bb=bb)
    nw = nw_ref[...]
    for h in range(H):
        vs = slice(h * dv, (h + 1) * dv)
        o_ref[:, vs] = _head_norm_gate(row_scr[:, vs], nw, r_ref[:, vs]).astype(o_ref.dtype)

    cv_in = ga_ref[...] * _sigmoid(gb_ref[...])
    d_scr[...] = cv_in * cw_ref[CONV_PAST:CONV_W, :]
    for s in range(bb):
        hist = sc_ref[s]
        d_scr[s:s + 1, :] += jnp.sum(hist * cw_ref[0:CONV_PAST, :], axis=0, keepdims=True)
        sco_ref[s, 0:CONV_PAST - 1, :] = sc_ref[s, 1:CONV_PAST, :]
        sco_ref[s, CONV_PAST - 1:CONV_PAST, :] = cv_in[s:s + 1, :]
    y = _ln_silu(d_scr[...] + cb_ref[...], lg_ref[...], lb_ref[...])
    cv_ref[...] = y.astype(cv_ref.dtype)


def _sample_mixer(kernel_fn, name, z, cols, widths, states, prev, consts, el, *, bb, extra_scratch=()):
    Bs = z.shape[0]
    s_mat, s_hist = states
    _, _, H, dk, dv = s_mat.shape
    hist_len, HW = s_hist.shape[2:]
    V = H * dv

    def zspec(width, start):
        return pl.BlockSpec((bb, width), lambda i: (i, start // width))

    mat_spec = pl.BlockSpec((None, bb, H, dk, dv), lambda i: (el, i, 0, 0, 0))
    hist_spec = pl.BlockSpec((None, bb, hist_len, HW), lambda i: (el, i, 0, 0))
    in_specs = [zspec(w, c) for w, c in zip(widths, cols)] + [mat_spec, hist_spec]
    args = [z] * len(cols) + [s_mat, s_hist]
    for a in consts:
        in_specs.append(_layer_spec(a, el) if a.ndim >= 3 else _const_spec(a))
        args.append(a)
    aliases = {}
    if prev is not None:
        for n, p in enumerate(prev):
            aliases[len(args)] = 2 + n
            in_specs.append(pl.BlockSpec(memory_space=pl.ANY))
            args.append(p)
    n_alias = 0 if prev is None else len(prev)
    if prev is None:
        n_lay = s_mat.shape[0]
        mat_out = pl.BlockSpec((n_lay, bb, H, dk, dv), lambda i: (0, i, 0, 0, 0))
        hist_out = pl.BlockSpec((n_lay, bb, hist_len, HW), lambda i: (0, i, 0, 0))
    else:
        mat_out, hist_out = mat_spec, hist_spec
    return pl.pallas_call(
        functools.partial(kernel_fn, H=H, dk=dk, dv=dv, bb=bb, n_alias=n_alias, el=el),
        grid=(Bs // bb,),
        in_specs=in_specs,
        out_specs=[pl.BlockSpec((bb, V), lambda i: (i, 0)),
                   pl.BlockSpec((bb, HW), lambda i: (i, 0)),
                   mat_out, hist_out],
        out_shape=[jax.ShapeDtypeStruct((Bs, V), BF16), jax.ShapeDtypeStruct((Bs, HW), BF16),
                   jax.ShapeDtypeStruct(s_mat.shape, F32), jax.ShapeDtypeStruct(s_hist.shape, F32)],
        scratch_shapes=[pltpu.VMEM((LANES, dk), F32), pltpu.VMEM((bb, V), F32),
                        pltpu.VMEM((bb, HW), F32)] + list(extra_scratch),
        input_output_aliases=aliases,
        compiler_params=_params("parallel"),
        name=name,
    )(*args)


def _pick(m, pref):
    return pref if m % pref == 0 else m


def kernel(x_prompt, x_sample, c_prompt, c_sample, state_gla, state_pool, state_hgrn, state_conv,
           w_ada, b_ada, norm_mix, norm_mlp, norm_final,
           ev_w_in, ev_w_gate_up, ev_b_gate, ev_gla_norm, ev_pool_w, ev_pool_scale, ev_w_out,
           od_w_in, od_lb_raw, od_hgrn_norm, od_conv_w, od_conv_b, od_ln_g, od_ln_b, od_w_out,
           mlp_w_up, mlp_w_down):
    B, T, D = x_prompt.shape
    Bs = x_sample.shape[0]
    depth = w_ada.shape[0]
    gla_w = D // 2
    gla_key_w = gla_w // 2
    gla_dk = gla_key_w // GLA_HEADS
    gla_dv = gla_w // GLA_HEADS
    pool_w = D - gla_w
    hgrn_w = D // 2
    hgrn_heads = hgrn_w // HGRN_DK
    hgrn_dv = hgrn_w // hgrn_heads
    conf_w = D - hgrn_w

    n_main = 2 * gla_key_w + 2 * gla_w
    ub_w = n_main // 2
    assert GLA_GATE_RANK + pool_w <= ub_w
    ev_w = jnp.pad(ev_w_in, ((0, 0), (0, 0), (0, n_main + ub_w - ev_w_in.shape[-1])))
    ev_cols = (0, gla_key_w, 2 * gla_key_w, 2 * gla_key_w + gla_w, n_main, n_main)
    ev_widths = (gla_key_w, gla_key_w, gla_w, gla_w, LANES, ub_w)
    wg = jnp.pad(ev_w_gate_up, ((0, 0), (0, LANES - GLA_GATE_RANK), (0, 0))).astype(BF16)
    od_cols = (0, hgrn_w, 2 * hgrn_w, 3 * hgrn_w, 4 * hgrn_w, 4 * hgrn_w + conf_w)
    od_widths = (hgrn_w, hgrn_w, hgrn_w, hgrn_w, conf_w, conf_w)
    pool_wb = ev_pool_w.astype(BF16)
    conv_wr = jnp.repeat(od_conv_w, SUBLANES, axis=1)
    conv_wp = jnp.pad(od_conv_w, ((0, 0), (0, 1), (0, 0)))

    lay3 = lambda a: a.reshape(a.shape[0], 1, a.shape[1])
    norm_mix3, norm_mlp3 = lay3(norm_mix), lay3(norm_mlp)
    bg3, gn3, ps3 = lay3(ev_b_gate), lay3(ev_gla_norm), lay3(ev_pool_scale)
    hn3, cb3, lg3, lb3 = lay3(od_hgrn_norm), lay3(od_conv_b), lay3(od_ln_g), lay3(od_ln_b)

    pad = (-(B + Bs)) % 16
    c_all = jnp.concatenate([c_sample, c_prompt, jnp.zeros((pad, D), F32)], axis=0)
    mod_s, mod_p = _ada(c_all, w_ada, b_ada, Bs, B)

    xp = x_prompt.reshape(B * T, D)
    xs = x_sample.reshape(Bs, D)
    Mp = B * T
    tm_in = _pick(Mp, 1024)
    tm_p = _pick(Mp, 512)
    bb = 8

    new = {k: [] for k in ("gla_p", "pool_p", "hgrn_p", "conv_p")}
    ev_prev = od_prev = cast = None
    for l in range(depth):
        e = l // 2
        nf = norm_final.reshape(1, D) if l == depth - 1 else None
        kw_p = dict(per_row=False, seq_len=T)
        kw_s = dict(per_row=True, seq_len=1)
        w_in_f, w_out_f = (ev_w, ev_w_out) if l % 2 == 0 else (od_w_in, od_w_out)
        if cast is None:
            w_in_s, w_out_s, w_up_s, w_down_s, wl = w_in_f, w_out_f, mlp_w_up, mlp_w_down, e
        else:
            (w_up_s, w_down_s, w_in_s, w_out_s), wl = cast, None
        if l % 2 == 0:
            n_in = ev_w.shape[-1]
            tn = ub_w
            zs, *w_in_b = _in_proj(xs, norm_mix3, mod_s, w_in_s, l, wl, tm=Bs, tn=tn, out_dtype=F32, **kw_s)
            a_s, b_s, sgs, sps = _sample_mixer(
                functools.partial(_even1_kernel, gc=pool_w // len(POOL_WINDOWS), u_off=GLA_GATE_RANK),
                "even_sample", zs, ev_cols, ev_widths, (state_gla, state_pool), ev_prev,
                (wg, bg3, gn3, pool_wb, ps3), e, bb=bb, extra_scratch=[pltpu.VMEM((bb, pool_w), F32)])
            ev_prev = (sgs, sps)
        else:
            n_in = od_w_in.shape[-1]
            tn = _pick(n_in, 1024)
            zs, *w_in_b = _in_proj(xs, norm_mix3, mod_s, w_in_s, l, wl, tm=Bs, tn=tn, out_dtype=F32, **kw_s)
            a_s, b_s, shs, scs = _sample_mixer(
                _odd1_kernel, "odd_sample",
                zs, od_cols, od_widths, (state_hgrn, state_conv), od_prev,
                (od_lb_raw, hn3, conv_wp, cb3, lg3, lb3), e, bb=bb)
            od_prev = (shs, scs)
        xs, *w_out_b = _out_proj_sample(a_s, b_s, w_out_s, wl, xs, mod_s, l, tn=_pick(D, 512))
        xs, *w_mlp_b = _mlp(xs, norm_mlp3, mod_s, w_up_s, w_down_s, l, nf, tm=Bs, tf=512, **kw_s)
        if cast is None:
            (w_in_b,), (w_out_b,), (w_up_b, w_down_b) = w_in_b, w_out_b, w_mlp_b
        else:
            w_up_b, w_down_b, w_in_b, w_out_b = cast

        xp3 = xp.reshape(B, T, D)
        if l % 2 == 0:
            (zp,) = _in_proj(xp, norm_mix3, mod_p, w_in_b, l, None, tm=tm_in, tn=tn, out_dtype=BF16, **kw_p)
            xp3, sg, sp = _mixer_prompt("gla", zp.reshape(B, T, n_in), None, ev_cols, ev_widths,
                                        (wg, bg3, gn3, pool_wb, ps3), w_out_b, xp3, mod_p, l, e,
                                        H=GLA_HEADS, dk=gla_dk, dv=gla_dv, hist_len=POOL_PAST,
                                        u_off=GLA_GATE_RANK)
            new["gla_p"].append(sg)
            new["pool_p"].append(sp)
        else:
            zp, zg = _in_proj(xp, norm_mix3, mod_p, w_in_b, l, None, tm=tm_in, tn=tn, out_dtype=BF16,
                              gate_block=od_cols[1] // tn, **kw_p)
            xp3, shg, scv = _mixer_prompt("hgrn", zp.reshape(B, T, n_in), zg.reshape(B, T, tn),
                                          od_cols, od_widths, (od_lb_raw, hn3, conv_wr, cb3, lg3, lb3),
                                          w_out_b, xp3, mod_p, l, e,
                                          H=hgrn_heads, dk=HGRN_DK, dv=hgrn_dv, hist_len=CONV_PAST)
            new["hgrn_p"].append(shg)
            new["conv_p"].append(scv)
        nxt = ()
        if l + 1 < depth:
            e_n = (l + 1) // 2
            w_in_n, w_out_n = (ev_w, ev_w_out) if (l + 1) % 2 == 0 else (od_w_in, od_w_out)
            nxt = ((mlp_w_up, l + 1), (mlp_w_down, l + 1), (w_in_n, e_n), (w_out_n, e_n))
        xp, *cast = _mlp(xp3.reshape(Mp, D), norm_mlp3, mod_p, w_up_b, w_down_b, l, nf,
                         tm=tm_p, tf=1024, cast_next=nxt, **kw_p)
        cast = cast or None

    st = lambda k: jnp.stack(new[k])
    return (xp.reshape(B, T, D), xs.reshape(Bs, 1, D),
            st("gla_p"), st("pool_p"), st("hgrn_p"), st("conv_p"),
            ev_prev[0], ev_prev[1], od_prev[0], od_prev[1])
```

```python
import functools

import jax
import jax.numpy as jnp
from jax import lax
from jax.experimental import pallas as pl
from jax.experimental.pallas import tpu as pltpu

F32 = jnp.float32
BF16 = jnp.bfloat16

EPS = 1e-6
MIN_GATE = 1e-30
PAST_LEN = 16384
GLA_HEADS = 4
GLA_GATE_RANK = 16
GLA_GATE_TEMP = 16.0
POOL_WINDOWS = (2, 4, 8, 16)
POOL_PAST = max(POOL_WINDOWS) - 1
CONV_W = 31
CONV_PAST = CONV_W - 1
HGRN_DK = 128
LANES = 128
SUBLANES = 8
REC_CHUNK = 128
CONV_HIST = 32
CONV_ROWS = 32
VMEM_LIMIT = 56 * 1024 * 1024

_NT = (((1,), (1,)), ((), ()))


def _params(*sem):
    return pltpu.CompilerParams(dimension_semantics=sem, vmem_limit_bytes=VMEM_LIMIT)


def _sigmoid(x):
    return 0.5 * jnp.tanh(0.5 * x) + 0.5


def _sigmoid_pair(x):
    t = jnp.exp(-jnp.abs(x))
    r = 1.0 / (1.0 + t)
    tr = t * r
    pos = x >= 0.0
    return jnp.where(pos, r, tr), jnp.where(pos, tr, r)


def _silu(x):
    return x * _sigmoid(x)


def _log_sigmoid(x):
    return jnp.minimum(x, 0.0) - jnp.log(1.0 + jnp.exp(-jnp.abs(x)))


def _rms_mod(x, g, sc, sh):
    ms = jnp.mean(x * x, axis=-1, keepdims=True)
    return (x * lax.rsqrt(ms + EPS) * g) * (1.0 + sc) + sh


def _head_norm_gate(o, nw, r):
    ms = jnp.mean(o * o, axis=-1, keepdims=True)
    return o * lax.rsqrt(ms + EPS) * nw * _silu(r)


def _shift_lanes(x, off, width):
    if off == 0:
        return x[:, :width]
    return pltpu.roll(x, x.shape[1] - off, 1)[:, :width]


def _const_spec(a):
    return pl.BlockSpec(a.shape, lambda *_: (0,) * a.ndim)


def _layer_spec(a, layer):
    return pl.BlockSpec((None,) + a.shape[1:], lambda *_: (layer,) + (0,) * (a.ndim - 1))


def _ada_kernel(c_ref, w_ref, b_ref, ms_ref, mp_ref, *, n_s, n_p):
    act = _silu(c_ref[...]).astype(BF16)
    res = jnp.dot(act, w_ref[...].astype(BF16), preferred_element_type=F32) + b_ref[...]
    ms_ref[...] = res[:n_s]
    for i in range(n_p):
        mp_ref[i] = res[n_s + i:n_s + i + 1]


def _ada(c_all, w_ada, b_ada, n_s, n_p, tn=1024):
    L, D, N = w_ada.shape
    R = c_all.shape[0]
    return pl.pallas_call(
        functools.partial(_ada_kernel, n_s=n_s, n_p=n_p),
        grid=(L, N // tn),
        in_specs=[pl.BlockSpec((R, D), lambda l, j: (0, 0)),
                  pl.BlockSpec((None, D, tn), lambda l, j: (l, 0, j)),
                  pl.BlockSpec((None, 1, tn), lambda l, j: (l, 0, j))],
        out_specs=[pl.BlockSpec((None, n_s, tn), lambda l, j: (l, 0, j)),
                   pl.BlockSpec((None, n_p, 1, tn), lambda l, j: (l, 0, 0, j))],
        out_shape=[jax.ShapeDtypeStruct((L, n_s, N), F32),
                   jax.ShapeDtypeStruct((L, n_p, 1, N), F32)],
        compiler_params=_params("parallel", "parallel"),
        name="ada_mod",
    )(c_all, w_ada, b_ada.reshape(L, 1, N))


def _mod_spec(per_row, layer, tm, D, seq_len, part):
    if per_row:
        return pl.BlockSpec((None, tm, D), lambda i, *_: (layer, i, part))
    return pl.BlockSpec((None, None, 1, D), lambda i, *_: (layer, (i * tm) // seq_len, 0, part))


def _row_chunks(tm):
    return 4 if tm % 512 == 0 else 1


def _rows(ref, rows):
    return ref[...] if ref.shape[0] == 1 else ref[rows, :]


def _w_spec(w, wl, block, index):
    if w.ndim == 2:
        return pl.BlockSpec(block, index)
    return pl.BlockSpec((None,) + block, lambda *g: (wl,) + index(*g))


def _in_kernel(*refs, gate_block, n_chunks, cast_w):
    x_ref, g_ref, sc_ref, sh_ref, w_ref, z_ref = refs[:6]
    rest = list(refs[6:])
    zg_ref = rest.pop(0) if gate_block is not None else None
    wb_ref = rest.pop(0) if cast_w else None
    (h_scr,) = rest
    j = pl.program_id(1)
    rc = x_ref.shape[0] // n_chunks
    if cast_w:
        wb_ref[...] = w_ref[...].astype(BF16)
        w_ref = wb_ref

    @pl.when(j == 0)
    def _():
        for r in range(n_chunks):
            rows = slice(r * rc, (r + 1) * rc)
            h = _rms_mod(x_ref[rows, :], g_ref[...], _rows(sc_ref, rows), _rows(sh_ref, rows)).astype(BF16)
            h_scr[rows, :] = h
            z_ref[rows, :] = jnp.dot(h, w_ref[...], preferred_element_type=F32).astype(z_ref.dtype)

    @pl.when(j != 0)
    def _():
        acc = jnp.dot(h_scr[...], w_ref[...], preferred_element_type=F32)
        z_ref[...] = acc.astype(z_ref.dtype)
        if gate_block is not None:
            @pl.when(j == gate_block)
            def _():
                zg_ref[...] = acc


def _in_proj(x, norm_g, mod, w, layer, wl, *, tm, tn, per_row, seq_len, out_dtype, gate_block=None):
    M, D = x.shape
    N = w.shape[-1]
    cast_w = w.dtype != BF16
    assert gate_block != 0 and N // tn >= 2
    out_specs = [pl.BlockSpec((tm, tn), lambda i, j: (i, j))]
    out_shape = [jax.ShapeDtypeStruct((M, N), out_dtype)]
    if gate_block is not None:
        out_specs.append(pl.BlockSpec((tm, tn), lambda i, j: (i, 0)))
        out_shape.append(jax.ShapeDtypeStruct((M, tn), F32))
    if cast_w:
        assert M == tm
        out_specs.append(pl.BlockSpec((D, tn), lambda i, j: (0, j)))
        out_shape.append(jax.ShapeDtypeStruct((D, N), BF16))
    return pl.pallas_call(
        functools.partial(_in_kernel, gate_block=gate_block, n_chunks=_row_chunks(tm), cast_w=cast_w),
        grid=(M // tm, N // tn),
        in_specs=[pl.BlockSpec((tm, D), lambda i, j: (i, 0)),
                  _layer_spec(norm_g, layer),
                  _mod_spec(per_row, layer, tm, D, seq_len, 1),
                  _mod_spec(per_row, layer, tm, D, seq_len, 0),
                  _w_spec(w, wl, (D, tn), lambda i, j: (0, j))],
        out_specs=out_specs,
        out_shape=out_shape,
        scratch_shapes=[pltpu.VMEM((tm, D), BF16)],
        compiler_params=_params("parallel", "arbitrary"),
        name="in_proj",
    )(x, norm_g, mod, mod, w)


def _out_kernel(a_ref, b_ref, w_ref, x_ref, g_ref, o_ref, *wb_ref):
    ka = a_ref.shape[1]
    w = w_ref[...]
    if wb_ref:
        w = w.astype(BF16)
        wb_ref[0][...] = w
    y = jnp.dot(a_ref[...], w[:ka], preferred_element_type=F32)
    y = y + jnp.dot(b_ref[...], w[ka:], preferred_element_type=F32)
    o_ref[...] = x_ref[...] + g_ref[...] * y


def _out_proj_sample(a, b, w_out, wl, x, mod, layer, *, tn):
    M, D = x.shape
    Ka, Kb = a.shape[1], b.shape[1]
    nj = D // tn
    out_specs = [pl.BlockSpec((M, tn), lambda j: (0, j))]
    out_shape = [jax.ShapeDtypeStruct((M, D), F32)]
    if w_out.dtype != BF16:
        out_specs.append(pl.BlockSpec((Ka + Kb, tn), lambda j: (0, j)))
        out_shape.append(jax.ShapeDtypeStruct((Ka + Kb, D), BF16))
    return pl.pallas_call(
        _out_kernel,
        grid=(nj,),
        in_specs=[pl.BlockSpec((M, Ka), lambda j: (0, 0)),
                  pl.BlockSpec((M, Kb), lambda j: (0, 0)),
                  _w_spec(w_out, wl, (Ka + Kb, tn), lambda j: (0, j)),
                  pl.BlockSpec((M, tn), lambda j: (0, j)),
                  pl.BlockSpec((None, M, tn), lambda j: (layer, 0, 2 * nj + j))],
        out_specs=out_specs,
        out_shape=out_shape,
        compiler_params=_params("parallel"),
        name="out_proj_sample",
    )(a, b, w_out, x, mod)


def _mlp_kernel(*refs, final_norm, n_chunks, cast_w, n_next):
    x_ref, g_ref, sc_ref, sh_ref, gate_ref, wu_ref, wd_ref = refs[:7]
    rest = list(refs[7:])
    gf_ref = rest.pop(0) if final_norm else None
    next_in = [rest.pop(0) for _ in range(n_next)]
    o_ref = rest.pop(0)
    wub_ref, wdb_ref = (rest.pop(0), rest.pop(0)) if cast_w else (None, None)
    next_out = [rest.pop(0) for _ in range(n_next)]
    h_scr, acc_scr = rest
    f = pl.program_id(1)
    last = pl.num_programs(1) - 1
    rc = x_ref.shape[0] // n_chunks
    if cast_w:
        wub_ref[...] = wu_ref[...].astype(BF16)
        wdb_ref[...] = wd_ref[...].astype(BF16)
        wu_ref, wd_ref = wub_ref, wdb_ref

    def cast_next():
        for src, dst in zip(next_in, next_out):
            dst[...] = src[...].astype(BF16)

    def up_down(h):
        a = jnp.dot(h, wu_ref[...], preferred_element_type=F32)
        a = jnp.square(jnp.maximum(a, 0.0)).astype(BF16)
        return jnp.dot(a, wd_ref[...], preferred_element_type=F32)

    @pl.when(f == 0)
    def _():
        cast_next()
        for r in range(n_chunks):
            rows = slice(r * rc, (r + 1) * rc)
            h = _rms_mod(x_ref[rows, :], g_ref[...], _rows(sc_ref, rows), _rows(sh_ref, rows)).astype(BF16)
            h_scr[rows, :] = h
            acc_scr[rows, :] = up_down(h)

    @pl.when((f != 0) & (f != last))
    def _():
        cast_next()
        acc_scr[...] += up_down(h_scr[...])

    @pl.when(f == last)
    def _():
        cast_next()
        for r in range(n_chunks):
            rows = slice(r * rc, (r + 1) * rc)
            y = x_ref[rows, :] + _rows(gate_ref, rows) * (acc_scr[rows, :] + up_down(h_scr[rows, :]))
            if final_norm:
                ms = jnp.mean(y * y, axis=-1, keepdims=True)
                y = y * lax.rsqrt(ms + EPS) * gf_ref[...]
            o_ref[rows, :] = y


def _mlp(x, norm_g, mod, w_up, w_down, layer, norm_final, *, tm, tf, per_row, seq_len, cast_next=()):
    M, D = x.shape
    FF = w_up.shape[-1]
    final_norm = norm_final is not None
    cast_w = w_up.dtype != BF16
    assert FF // tf >= 2
    nf = FF // tf
    n_steps = (M // tm) * nf
    in_specs = [pl.BlockSpec((tm, D), lambda i, f: (i, 0)),
                _layer_spec(norm_g, layer),
                _mod_spec(per_row, layer, tm, D, seq_len, 4),
                _mod_spec(per_row, layer, tm, D, seq_len, 3),
                _mod_spec(per_row, layer, tm, D, seq_len, 5),
                _w_spec(w_up, layer, (D, tf), lambda i, f: (0, f)),
                _w_spec(w_down, layer, (tf, D), lambda i, f: (f, 0))]
    args = [x, norm_g, mod, mod, mod, w_up, w_down]
    if final_norm:
        in_specs.append(_const_spec(norm_final))
        args.append(norm_final)
    out_specs = [pl.BlockSpec((tm, D), lambda i, f: (i, 0), pipeline_mode=pl.Buffered(1))]
    out_shape = [jax.ShapeDtypeStruct((M, D), F32)]
    if cast_w:
        assert M == tm
        out_specs += [pl.BlockSpec((D, tf), lambda i, f: (0, f)), pl.BlockSpec((tf, D), lambda i, f: (f, 0))]
        out_shape += [jax.ShapeDtypeStruct((D, FF), BF16), jax.ShapeDtypeStruct((FF, D), BF16)]
    for w_next, wl in cast_next:
        _, R, Cn = w_next.shape
        rows = R // n_steps
        assert rows * n_steps == R and rows % 16 == 0
        in_specs.append(pl.BlockSpec((None, rows, Cn), lambda i, f, wl=wl: (wl, i * nf + f, 0)))
        args.append(w_next)
        out_specs.append(pl.BlockSpec((rows, Cn), lambda i, f: (i * nf + f, 0)))
        out_shape.append(jax.ShapeDtypeStruct((R, Cn), BF16))
    return pl.pallas_call(
        functools.partial(_mlp_kernel, final_norm=final_norm, n_chunks=_row_chunks(tm), cast_w=cast_w,
                          n_next=len(cast_next)),
        grid=(M // tm, FF // tf),
        in_specs=in_specs,
        out_specs=out_specs,
        out_shape=out_shape,
        scratch_shapes=[pltpu.VMEM((tm, D), BF16), pltpu.VMEM((tm, D), F32)],
        compiler_params=_params("parallel", "arbitrary"),
        name="mlp",
    )(*args)


def _gla_log_decay(alr, wg, bg):
    pre = jnp.dot(alr.astype(BF16), wg, preferred_element_type=F32) + bg
    return _log_sigmoid(pre) / GLA_GATE_TEMP


def _hgrn_lower_bound(lbraw_ref, layer):
    rows = [lbraw_ref[j:j + 1, :] for j in range(lbraw_ref.shape[0])]
    mx = functools.reduce(jnp.maximum, rows)
    es = [jnp.exp(r - mx) for r in rows]
    tot = functools.reduce(lambda a, b: a + b, es)
    sm = [e / tot for e in es]
    cum = functools.reduce(lambda a, b: a + b, sm[:layer + 1])
    return cum - sm[0]


def _hgrn_gates(fz, lb):
    sig, sig_neg = _sigmoid_pair(fz)
    f = lb + (1.0 - lb) * sig
    g = jnp.log(jnp.maximum(f, MIN_GATE))
    k = (1.0 - lb) * sig_neg
    return g, k


def _level_ref(b_scr, ls, m, C):
    W = ls.stop - ls.start
    if 2 * m >= 2 * SUBLANES:
        pieces = [jnp.broadcast_to(b_scr[n * 2 * m + m - 1:n * 2 * m + m, ls], (2 * m, W))
                  for n in range(C // (2 * m))]
    elif m == 4:
        pieces = [jnp.broadcast_to(b_scr[j * 8 + 3:j * 8 + 4, ls], (8, W)) for j in range(C // 8)]
    else:
        sub = lax.broadcasted_iota(jnp.int32, (8, W), 0)
        pieces = [jnp.where(sub < 4,
                            jnp.broadcast_to(b_scr[j * 8 + 1:j * 8 + 2, ls], (8, W)),
                            jnp.broadcast_to(b_scr[j * 8 + 5:j * 8 + 6, ls], (8, W)))
                  for j in range(C // 8)]
    return jnp.concatenate(pieces, axis=0)


def _rec_chunk(q, k, g, v_ref, r_ref, nw, o_scr, st_scr, b_scr, *, heads, dk, dv, C):
    W = len(heads) * dk
    ls = slice(heads[0] * dk, heads[0] * dk + W)

    row = lax.broadcasted_iota(jnp.int32, (C, C), 0)
    col = lax.broadcasted_iota(jnp.int32, (C, C), 1)
    tril = jnp.where(row >= col, 1.0, 0.0).astype(BF16)
    g1 = g.astype(BF16)
    rem = g - g1.astype(F32)
    g2 = rem.astype(BF16)
    g3 = (rem - g2.astype(F32)).astype(BF16)
    b = (jnp.dot(tril, g1, preferred_element_type=F32)
         + jnp.dot(tril, g2, preferred_element_type=F32)
         + jnp.dot(tril, g3, preferred_element_type=F32))
    b_scr[:, ls] = b

    H = len(heads)
    scores = [None] * H
    levels = []
    m = C // 2
    while m >= 1:
        levels.append(m)
        m //= 2
    sub = lax.broadcasted_iota(jnp.int32, (SUBLANES, W), 0)
    for m in levels + [0]:
        if m == 0:
            qe, ke = q.astype(BF16), k.astype(BF16)
            mask = row == col
        else:
            if m >= SUBLANES:
                src = jnp.concatenate(
                    [a[n * 2 * m + o:n * 2 * m + o + m] for n in range(C // (2 * m))
                     for a, o in ((k, 0), (q, m))], axis=0)
            else:
                upper = (sub & (2 * m - 1)) >= m
                src = jnp.concatenate(
                    [jnp.where(upper, q[j * SUBLANES:(j + 1) * SUBLANES],
                               k[j * SUBLANES:(j + 1) * SUBLANES])
                     for j in range(C // SUBLANES)], axis=0)
            if m == 1:
                odd = (sub & 1) == 1
                e = jnp.concatenate(
                    [jnp.where(odd, jnp.exp(g[j * SUBLANES:(j + 1) * SUBLANES]), 1.0)
                     for j in range(C // SUBLANES)], axis=0)
            else:
                e = jnp.exp(-jnp.abs(b - _level_ref(b_scr, ls, m, C)))
            qe = ke = (src * e).astype(BF16)
            sh = (2 * m).bit_length() - 1
            mask = (((row >> sh) == (col >> sh))
                    & ((row & (2 * m - 1)) >= m) & ((col & (2 * m - 1)) < m))
        for h in range(H):
            hs = slice(h * dk, (h + 1) * dk)
            p = lax.dot_general(qe[:, hs], ke[:, hs], _NT, preferred_element_type=F32)
            scores[h] = jnp.where(mask, p, 0.0 if scores[h] is None else scores[h])

    b_end = b_scr[C - 1:C, ls]
    qi = (q * jnp.exp(b)).astype(BF16)
    kend = (k * jnp.exp(b_end - b)).astype(BF16)
    a_end = jnp.exp(b_end)
    for h, head in enumerate(heads):
        hs = slice(h * dk, (h + 1) * dk)
        vs = slice(head * dv, (head + 1) * dv)
        v_h = v_ref[:, vs]
        st = st_scr[head]
        o = lax.dot_general(qi[:, hs], st.astype(BF16), _NT, preferred_element_type=F32)
        o = o + jnp.dot(scores[h].astype(BF16), v_h, preferred_element_type=F32)
        st_scr[head] = st * a_end[:, hs] + jnp.dot(v_h.astype(F32).T.astype(BF16), kend[:, hs],
                                                   preferred_element_type=F32)
        o_scr[:, vs] = _head_norm_gate(o, nw, r_ref[:, vs].astype(F32)).astype(o_scr.dtype)


def _pool_chunk(u, pw_ref, ps_ref, p_scr, buf, *, C, gc, chunk):
    PB = POOL_PAST + 1
    buf[PB:PB + C, :] = u
    pos = chunk * C + lax.broadcasted_iota(jnp.int32, (C, 1), 0)
    for g, w in enumerate(POOL_WINDOWS):
        cs = slice(g * gc, (g + 1) * gc)
        cur = u[:, cs]
        win = cur
        for i in range(1, w):
            win = win + buf[PB - i:PB - i + C, cs]
        cnt = jnp.minimum(pos + 1, w).astype(F32)
        d = win / cnt - cur
        y = jnp.dot(d.astype(BF16), pw_ref[g], preferred_element_type=F32) * ps_ref[:, cs]
        p_scr[:, cs] = y.astype(p_scr.dtype)
    buf[0:PB, :] = buf[C:C + PB, :]


def _ln_silu(x, g, b):
    mu = jnp.mean(x, axis=-1, keepdims=True)
    xc = x - mu
    var = jnp.mean(xc * xc, axis=-1, keepdims=True)
    return _silu(xc * lax.rsqrt(var + EPS) * g + b)


def _conv_chunk(cv_in, cw_ref, cb_ref, lg_ref, lb_ref, p_scr, buf, ph, *, C):
    HB = CONV_HIST
    S = SUBLANES
    rt = CONV_ROWS
    buf[HB:HB + C, :] = cv_in
    for p in range(1, S):
        ph[p - 1, S:HB + C, :] = buf[S - p:HB + C - p, :]
    n_sub = rt // S
    for t in range(C // rt):
        accs = [None] * n_sub
        for back in range(CONV_W):
            a, p = divmod(back, S)
            j = CONV_PAST - back
            wt = cw_ref[j * S:(j + 1) * S, :]
            for q in range(n_sub):
                r0 = HB + t * rt + q * S - a * S
                src = buf[r0:r0 + S, :] if p == 0 else ph[p - 1, r0:r0 + S, :]
                term = src * wt
                accs[q] = term if accs[q] is None else accs[q] + term
        acc = jnp.concatenate(accs, axis=0)
        y = _ln_silu(acc + cb_ref[...], lg_ref[...], lb_ref[...])
        p_scr[t * rt:(t + 1) * rt, :] = y.astype(p_scr.dtype)
    buf[0:HB, :] = buf[C:C + HB, :]


def _mixer_kernel(*refs, mode, H, dk, dv, C, n_sub, layer, gc, u_off, cols, widths):
    if mode == "gla":
        (z_ref, wg_ref, bg_ref, nw_ref, pw_ref, ps_ref,
         wa_ref, wb_ref, x_ref, g1_ref, xo_ref, s_ref, sh_ref,
         st_scr, b_scr, o_scr, p_scr, buf) = refs
        hist0, hist_rows = POOL_PAST + 1, POOL_PAST
    else:
        (z_ref, zg_ref, lbraw_ref, nw_ref, cw_ref, cb_ref, lg_ref, lb_ref,
         wa_ref, wb_ref, x_ref, g1_ref, xo_ref, s_ref, sh_ref,
         st_scr, b_scr, o_scr, p_scr, buf, ph) = refs
        hist0, hist_rows = CONV_HIST, CONV_PAST
    c = pl.program_id(1)

    @pl.when(c == 0)
    def _():
        st_scr[...] = jnp.zeros_like(st_scr)
        buf[0:hist0, :] = jnp.zeros((hist0, buf.shape[1]), F32)

    for sub in range(n_sub):
        rows = slice(sub * C, (sub + 1) * C)
        zc = [z_ref.at[rows, s:s + w] for s, w in zip(cols, widths)]
        if mode == "gla":
            q_ref, k_ref, v_ref, r_ref, alr_ref, u_ref = zc
            u = _shift_lanes(u_ref[...].astype(F32), u_off, p_scr.shape[1])
            _pool_chunk(u, pw_ref, ps_ref, p_scr, buf, C=C, gc=gc, chunk=c * n_sub + sub)
        else:
            q_ref, _, v_ref, r_ref, ga_ref, gb_ref = zc
            fz_ref = zg_ref.at[rows, :]
            cv_in = ga_ref[...].astype(F32) * _sigmoid(gb_ref[...].astype(F32))
            _conv_chunk(cv_in, cw_ref, cb_ref, lg_ref, lb_ref, p_scr, buf, ph, C=C)
        y = jnp.dot(p_scr[...], wb_ref[...], preferred_element_type=F32)
        n_grp = 2 if mode == "hgrn" else 1
        hg = H // n_grp
        for grp in range(n_grp):
            heads = tuple(range(grp * hg, (grp + 1) * hg))
            ls = slice(heads[0] * dk, (heads[-1] + 1) * dk)
            vs = slice(heads[0] * dv, (heads[-1] + 1) * dv)
            if mode == "gla":
                q = q_ref[:, ls].astype(F32) * (dk ** -0.5)
                k = k_ref[:, ls].astype(F32)
                g = _gla_log_decay(alr_ref[...], wg_ref[:, ls], bg_ref[:, ls])
            else:
                q = q_ref[:, ls].astype(F32)
                g, k = _hgrn_gates(fz_ref[:, ls], _hgrn_lower_bound(lbraw_ref, layer)[:, ls])
            _rec_chunk(q, k, g, v_ref, r_ref, nw_ref[...], o_scr, st_scr, b_scr,
                       heads=heads, dk=dk, dv=dv, C=C)
            y = y + jnp.dot(o_scr[:, vs], wa_ref[vs, :], preferred_element_type=F32)
        xo_ref[rows, :] = x_ref[rows, :] + g1_ref[...] * y

    @pl.when(c == pl.num_programs(1) - 1)
    def _():
        for h in range(H):
            s_ref[h] = st_scr[h].T
        sh_ref[...] = buf[C + hist0 - hist_rows:C + hist0, :]


def _mixer_prompt(mode, z, zg, cols, widths, consts, w_out, x, mod, layer, el, *, H, dk, dv, hist_len, u_off=0):
    B, T, _ = z.shape
    D = x.shape[-1]
    C = REC_CHUNK
    V = H * dv
    PW = D - V
    assert T % C == 0
    n_sub = 2 if T % (2 * C) == 0 else 1
    R = C * n_sub

    in_specs = [pl.BlockSpec((None, R, z.shape[-1]), lambda b, c: (b, c, 0))]
    args = [z]
    if mode == "hgrn":
        in_specs.append(pl.BlockSpec((None, R, zg.shape[-1]), lambda b, c: (b, c, 0)))
        args.append(zg)
    for a in consts:
        in_specs.append(_layer_spec(a, el) if a.ndim >= 3 else _const_spec(a))
        args.append(a)
    in_specs += [pl.BlockSpec((V, D), lambda b, c: (0, 0)),
                 pl.BlockSpec((PW, D), lambda b, c: (V // PW, 0)),
                 pl.BlockSpec((None, R, D), lambda b, c: (b, c, 0)),
                 pl.BlockSpec((None, None, 1, D), lambda b, c: (layer, b, 0, 2))]
    args += [w_out, w_out, x, mod]
    scratch = [pltpu.VMEM((H, dv, dk), F32), pltpu.VMEM((C, H * dk), F32),
               pltpu.VMEM((C, V), BF16), pltpu.VMEM((C, PW), BF16)]
    if mode == "gla":
        scratch.append(pltpu.VMEM((POOL_PAST + 1 + C, PW), F32))
    else:
        scratch += [pltpu.VMEM((CONV_HIST + C, PW), F32),
                    pltpu.VMEM((SUBLANES - 1, CONV_HIST + C, PW), F32)]
    return pl.pallas_call(
        functools.partial(_mixer_kernel, mode=mode, H=H, dk=dk, dv=dv, C=C, n_sub=n_sub, layer=el,
                          gc=PW // len(POOL_WINDOWS), u_off=u_off, cols=cols, widths=widths),
        grid=(B, T // R),
        in_specs=in_specs,
        out_specs=[pl.BlockSpec((None, R, D), lambda b, c: (b, c, 0)),
                   pl.BlockSpec((None, H, dk, dv), lambda b, c: (b, 0, 0, 0)),
                   pl.BlockSpec((None, hist_len, PW), lambda b, c: (b, 0, 0))],
        out_shape=[jax.ShapeDtypeStruct((B, T, D), F32),
                   jax.ShapeDtypeStruct((B, H, dk, dv), F32),
                   jax.ShapeDtypeStruct((B, hist_len, PW), F32)],
        scratch_shapes=scratch,
        compiler_params=_params("parallel", "arbitrary"),
        name="mixer_" + mode,
    )(*args)


def _step_recurrence(a, k, q, v_ref, s_ref, so_ref, tile_scr, row_scr, *, H, dk, dv, bb):
    for h in range(H):
        hs = slice(h * dk, (h + 1) * dk)
        tile_scr[...] = jnp.zeros_like(tile_scr)
        tile_scr[0:bb, :] = a[:, hs]
        tile_scr[bb:2 * bb, :] = k[:, hs]
        tile_scr[2 * bb:3 * bb, :] = q[:, hs]
        cols = tile_scr[...].T
        for s in range(bb):
            a_col = cols[:, s:s + 1]
            k_col = cols[:, bb + s:bb + s + 1]
            q_col = cols[:, 2 * bb + s:2 * bb + s + 1]
            v_row = v_ref[s:s + 1, h * dv:(h + 1) * dv]
            s_new = a_col * s_ref[s, h] + k_col * v_row
            so_ref[s, h] = s_new
            row_scr[s:s + 1, h * dv:(h + 1) * dv] = jnp.sum(q_col * s_new, axis=0, keepdims=True)


def _own_slab(ref, el, n_alias):
    if n_alias:
        return ref
    for j in range(ref.shape[0]):
        if j != el:
            ref[j] = jnp.zeros(ref.shape[1:], ref.dtype)
    return ref.at[el]


def _even1_kernel(*refs, H, dk, dv, bb, gc, n_alias, u_off, el):
    (q_ref, k_ref, v_ref, r_ref, alr_ref, ub_ref, sg_ref, sp_ref,
     wg_ref, bg_ref, nw_ref, pw_ref, ps_ref) = refs[:13]
    o_ref, p_ref, sgo_ref, spo_ref, tile_scr, row_scr, d_scr, u_scr = refs[13 + n_alias:]
    sgo_ref, spo_ref = _own_slab(sgo_ref, el, n_alias), _own_slab(spo_ref, el, n_alias)
    g = _gla_log_decay(alr_ref[...], wg_ref[...], bg_ref[...])
    _step_recurrence(jnp.exp(g), k_ref[...], q_ref[...] * (dk ** -0.5), v_ref, sg_ref, sgo_ref,
                     tile_scr, row_scr, H=H, dk=dk, dv=dv, bb=bb)
    nw = nw_ref[...]
    for h in range(H):
        vs = slice(h * dv, (h + 1) * dv)
        o_ref[:, vs] = _head_norm_gate(row_scr[:, vs], nw, r_ref[:, vs]).astype(o_ref.dtype)

    u_scr[...] = _shift_lanes(ub_ref[...], u_off, u_scr.shape[1])
    for s in range(bb):
        for g_i, w in enumerate(POOL_WINDOWS):
            cs = slice(g_i * gc, (g_i + 1) * gc)
            cur = u_scr[s:s + 1, cs]
            win = cur + jnp.sum(sp_ref[s, POOL_PAST - (w - 1):POOL_PAST, cs], axis=0, keepdims=True)
            cnt = float(min(PAST_LEN + 1, w))
            d_scr[s:s + 1, cs] = win / cnt - cur
        spo_ref[s, 0:POOL_PAST - 1, :] = sp_ref[s, 1:POOL_PAST, :]
        spo_ref[s, POOL_PAST - 1:POOL_PAST, :] = u_scr[s:s + 1, :]
    for g_i in range(len(POOL_WINDOWS)):
        cs = slice(g_i * gc, (g_i + 1) * gc)
        y = jnp.dot(d_scr[:, cs].astype(BF16), pw_ref[g_i], preferred_element_type=F32) * ps_ref[:, cs]
        p_ref[:, cs] = y.astype(p_ref.dtype)


def _odd1_kernel(*refs, H, dk, dv, bb, n_alias, el):
    (q_ref, fz_ref, v_ref, r_ref, ga_ref, gb_ref, sh_ref, sc_ref,
     lbraw_ref, nw_ref, cw_ref, cb_ref, lg_ref, lb_ref) = refs[:14]
    o_ref, cv_ref, sho_ref, sco_ref, tile_scr, row_scr, d_scr = refs[14 + n_alias:]
    sho_ref, sco_ref = _own_slab(sho_ref, el, n_alias), _own_slab(sco_ref, el, n_alias)
    g, k = _hgrn_gates(fz_ref[...], _hgrn_lower_bound(lbraw_ref, el))
    _step_recurrence(jnp.exp(g), k, q_ref[...], v_ref, sh_ref, sho_ref,
                     tile_scr, row_scr, H=H, dk=dk, dv=dv, bb=bb)
    nw = nw_ref[...]
    for h in range(H):
        vs = slice(h * dv, (h + 1) * dv)
        o_ref[:, vs] = _head_norm_gate(row_scr[:, vs], nw, r_ref[:, vs]).astype(o_ref.dtype)

    cv_in = ga_ref[...] * _sigmoid(gb_ref[...])
    d_scr[...] = cv_in * cw_ref[CONV_PAST:CONV_W, :]
    for s in range(bb):
        hist = sc_ref[s]
        d_scr[s:s + 1, :] += jnp.sum(hist * cw_ref[0:CONV_PAST, :], axis=0, keepdims=True)
        sco_ref[s, 0:CONV_PAST - 1, :] = sc_ref[s, 1:CONV_PAST, :]
        sco_ref[s, CONV_PAST - 1:CONV_PAST, :] = cv_in[s:s + 1, :]
    y = _ln_silu(d_scr[...] + cb_ref[...], lg_ref[...], lb_ref[...])
    cv_ref[...] = y.astype(cv_ref.dtype)


def _sample_mixer(kernel_fn, name, z, cols, widths, states, prev, consts, el, *, bb, extra_scratch=()):
    Bs = z.shape[0]
    s_mat, s_hist = states
    _, _, H, dk, dv = s_mat.shape
    hist_len, HW = s_hist.shape[2:]
    V = H * dv

    def zspec(width, start):
        return pl.BlockSpec((bb, width), lambda i: (i, start // width))

    mat_spec = pl.BlockSpec((None, bb, H, dk, dv), lambda i: (el, i, 0, 0, 0))
    hist_spec = pl.BlockSpec((None, bb, hist_len, HW), lambda i: (el, i, 0, 0))
    in_specs = [zspec(w, c) for w, c in zip(widths, cols)] + [mat_spec, hist_spec]
    args = [z] * len(cols) + [s_mat, s_hist]
    for a in consts:
        in_specs.append(_layer_spec(a, el) if a.ndim >= 3 else _const_spec(a))
        args.append(a)
    aliases = {}
    if prev is not None:
        for n, p in enumerate(prev):
            aliases[len(args)] = 2 + n
            in_specs.append(pl.BlockSpec(memory_space=pl.ANY))
            args.append(p)
    n_alias = 0 if prev is None else len(prev)
    if prev is None:
        n_lay = s_mat.shape[0]
        mat_out = pl.BlockSpec((n_lay, bb, H, dk, dv), lambda i: (0, i, 0, 0, 0))
        hist_out = pl.BlockSpec((n_lay, bb, hist_len, HW), lambda i: (0, i, 0, 0))
    else:
        mat_out, hist_out = mat_spec, hist_spec
    return pl.pallas_call(
        functools.partial(kernel_fn, H=H, dk=dk, dv=dv, bb=bb, n_alias=n_alias, el=el),
        grid=(Bs // bb,),
        in_specs=in_specs,
        out_specs=[pl.BlockSpec((bb, V), lambda i: (i, 0)),
                   pl.BlockSpec((bb, HW), lambda i: (i, 0)),
                   mat_out, hist_out],
        out_shape=[jax.ShapeDtypeStruct((Bs, V), BF16), jax.ShapeDtypeStruct((Bs, HW), BF16),
                   jax.ShapeDtypeStruct(s_mat.shape, F32), jax.ShapeDtypeStruct(s_hist.shape, F32)],
        scratch_shapes=[pltpu.VMEM((LANES, dk), F32), pltpu.VMEM((bb, V), F32),
                        pltpu.VMEM((bb, HW), F32)] + list(extra_scratch),
        input_output_aliases=aliases,
        compiler_params=_params("parallel"),
        name=name,
    )(*args)


def _pick(m, pref):
    return pref if m % pref == 0 else m


def kernel(x_prompt, x_sample, c_prompt, c_sample, state_gla, state_pool, state_hgrn, state_conv,
           w_ada, b_ada, norm_mix, norm_mlp, norm_final,
           ev_w_in, ev_w_gate_up, ev_b_gate, ev_gla_norm, ev_pool_w, ev_pool_scale, ev_w_out,
           od_w_in, od_lb_raw, od_hgrn_norm, od_conv_w, od_conv_b, od_ln_g, od_ln_b, od_w_out,
           mlp_w_up, mlp_w_down):
    B, T, D = x_prompt.shape
    Bs = x_sample.shape[0]
    depth = w_ada.shape[0]
    gla_w = D // 2
    gla_key_w = gla_w // 2
    gla_dk = gla_key_w // GLA_HEADS
    gla_dv = gla_w // GLA_HEADS
    pool_w = D - gla_w
    hgrn_w = D // 2
    hgrn_heads = hgrn_w // HGRN_DK
    hgrn_dv = hgrn_w // hgrn_heads
    conf_w = D - hgrn_w

    n_main = 2 * gla_key_w + 2 * gla_w
    ub_w = n_main // 2
    assert GLA_GATE_RANK + pool_w <= ub_w
    ev_w = jnp.pad(ev_w_in, ((0, 0), (0, 0), (0, n_main + ub_w - ev_w_in.shape[-1])))
    ev_cols = (0, gla_key_w, 2 * gla_key_w, 2 * gla_key_w + gla_w, n_main, n_main)
    ev_widths = (gla_key_w, gla_key_w, gla_w, gla_w, LANES, ub_w)
    wg = jnp.pad(ev_w_gate_up, ((0, 0), (0, LANES - GLA_GATE_RANK), (0, 0))).astype(BF16)
    od_cols = (0, hgrn_w, 2 * hgrn_w, 3 * hgrn_w, 4 * hgrn_w, 4 * hgrn_w + conf_w)
    od_widths = (hgrn_w, hgrn_w, hgrn_w, hgrn_w, conf_w, conf_w)
    pool_wb = ev_pool_w.astype(BF16)
    conv_wr = jnp.repeat(od_conv_w, SUBLANES, axis=1)
    conv_wp = jnp.pad(od_conv_w, ((0, 0), (0, 1), (0, 0)))

    lay3 = lambda a: a.reshape(a.shape[0], 1, a.shape[1])
    norm_mix3, norm_mlp3 = lay3(norm_mix), lay3(norm_mlp)
    bg3, gn3, ps3 = lay3(ev_b_gate), lay3(ev_gla_norm), lay3(ev_pool_scale)
    hn3, cb3, lg3, lb3 = lay3(od_hgrn_norm), lay3(od_conv_b), lay3(od_ln_g), lay3(od_ln_b)

    pad = (-(B + Bs)) % 16
    c_all = jnp.concatenate([c_sample, c_prompt, jnp.zeros((pad, D), F32)], axis=0)
    mod_s, mod_p = _ada(c_all, w_ada, b_ada, Bs, B)

    xp = x_prompt.reshape(B * T, D)
    xs = x_sample.reshape(Bs, D)
    Mp = B * T
    tm_in = _pick(Mp, 1024)
    tm_p = _pick(Mp, 512)
    bb = 8

    new = {k: [] for k in ("gla_p", "pool_p", "hgrn_p", "conv_p")}
    ev_prev = od_prev = cast = None
    for l in range(depth):
        e = l // 2
        nf = norm_final.reshape(1, D) if l == depth - 1 else None
        kw_p = dict(per_row=False, seq_len=T)
        kw_s = dict(per_row=True, seq_len=1)
        w_in_f, w_out_f = (ev_w, ev_w_out) if l % 2 == 0 else (od_w_in, od_w_out)
        if cast is None:
            w_in_s, w_out_s, w_up_s, w_down_s, wl = w_in_f, w_out_f, mlp_w_up, mlp_w_down, e
        else:
            (w_up_s, w_down_s, w_in_s, w_out_s), wl = cast, None
        if l % 2 == 0:
            n_in = ev_w.shape[-1]
            tn = ub_w
            zs, *w_in_b = _in_proj(xs, norm_mix3, mod_s, w_in_s, l, wl, tm=Bs, tn=tn, out_dtype=F32, **kw_s)
            a_s, b_s, sgs, sps = _sample_mixer(
                functools.partial(_even1_kernel, gc=pool_w // len(POOL_WINDOWS), u_off=GLA_GATE_RANK),
                "even_sample", zs, ev_cols, ev_widths, (state_gla, state_pool), ev_prev,
                (wg, bg3, gn3, pool_wb, ps3), e, bb=bb, extra_scratch=[pltpu.VMEM((bb, pool_w), F32)])
            ev_prev = (sgs, sps)
        else:
            n_in = od_w_in.shape[-1]
            tn = _pick(n_in, 1024)
            zs, *w_in_b = _in_proj(xs, norm_mix3, mod_s, w_in_s, l, wl, tm=Bs, tn=tn, out_dtype=F32, **kw_s)
            a_s, b_s, shs, scs = _sample_mixer(
                _odd1_kernel, "odd_sample",
                zs, od_cols, od_widths, (state_hgrn, state_conv), od_prev,
                (od_lb_raw, hn3, conv_wp, cb3, lg3, lb3), e, bb=bb)
            od_prev = (shs, scs)
        xs, *w_out_b = _out_proj_sample(a_s, b_s, w_out_s, wl, xs, mod_s, l, tn=_pick(D, 512))
        xs, *w_mlp_b = _mlp(xs, norm_mlp3, mod_s, w_up_s, w_down_s, l, nf, tm=Bs, tf=512, **kw_s)
        if cast is None:
            (w_in_b,), (w_out_b,), (w_up_b, w_down_b) = w_in_b, w_out_b, w_mlp_b
        else:
            w_up_b, w_down_b, w_in_b, w_out_b = cast

        xp3 = xp.reshape(B, T, D)
        if l % 2 == 0:
            (zp,) = _in_proj(xp, norm_mix3, mod_p, w_in_b, l, None, tm=tm_in, tn=tn, out_dtype=BF16, **kw_p)
            xp3, sg, sp = _mixer_prompt("gla", zp.reshape(B, T, n_in), None, ev_cols, ev_widths,
                                        (wg, bg3, gn3, pool_wb, ps3), w_out_b, xp3, mod_p, l, e,
                                        H=GLA_HEADS, dk=gla_dk, dv=gla_dv, hist_len=POOL_PAST,
                                        u_off=GLA_GATE_RANK)
            new["gla_p"].append(sg)
            new["pool_p"].append(sp)
        else:
            zp, zg = _in_proj(xp, norm_mix3, mod_p, w_in_b, l, None, tm=tm_in, tn=tn, out_dtype=BF16,
                              gate_block=od_cols[1] // tn, **kw_p)
            xp3, shg, scv = _mixer_prompt("hgrn", zp.reshape(B, T, n_in), zg.reshape(B, T, tn),
                                          od_cols, od_widths, (od_lb_raw, hn3, conv_wr, cb3, lg3, lb3),
                                          w_out_b, xp3, mod_p, l, e,
                                          H=hgrn_heads, dk=HGRN_DK, dv=hgrn_dv, hist_len=CONV_PAST)
            new["hgrn_p"].append(shg)
            new["conv_p"].append(scv)
        nxt = ()
        if l + 1 < depth:
            e_n = (l + 1) // 2
            w_in_n, w_out_n = (ev_w, ev_w_out) if (l + 1) % 2 == 0 else (od_w_in, od_w_out)
            nxt = ((mlp_w_up, l + 1), (mlp_w_down, l + 1), (w_in_n, e_n), (w_out_n, e_n))
        xp, *cast = _mlp(xp3.reshape(Mp, D), norm_mlp3, mod_p, w_up_b, w_down_b, l, nf,
                         tm=tm_in, tf=512, cast_next=nxt, **kw_p)
        cast = cast or None

    st = lambda k: jnp.stack(new[k])
    return (xp.reshape(B, T, D), xs.reshape(Bs, 1, D),
            st("gla_p"), st("pool_p"), st("hgrn_p"), st("conv_p"),
            ev_prev[0], ev_prev[1], od_prev[0], od_prev[1])
```

```python
import functools

import jax
import jax.numpy as jnp
from jax import lax
from jax.experimental import pallas as pl
from jax.experimental.pallas import tpu as pltpu

F32 = jnp.float32
BF16 = jnp.bfloat16

EPS = 1e-6
MIN_GATE = 1e-30
PAST_LEN = 16384
GLA_HEADS = 4
GLA_GATE_RANK = 16
GLA_GATE_TEMP = 16.0
POOL_WINDOWS = (2, 4, 8, 16)
POOL_PAST = max(POOL_WINDOWS) - 1
CONV_W = 31
CONV_PAST = CONV_W - 1
HGRN_DK = 128
LANES = 128
SUBLANES = 8
REC_CHUNK = 128
CONV_HIST = 32
CONV_ROWS = 32
VMEM_LIMIT = 56 * 1024 * 1024

_NT = (((1,), (1,)), ((), ()))


def _params(*sem):
    return pltpu.CompilerParams(dimension_semantics=sem, vmem_limit_bytes=VMEM_LIMIT)


def _sigmoid(x):
    return 0.5 * jnp.tanh(0.5 * x) + 0.5


def _sigmoid_pair(x):
    t = jnp.exp(-jnp.abs(x))
    r = 1.0 / (1.0 + t)
    tr = t * r
    pos = x >= 0.0
    return jnp.where(pos, r, tr), jnp.where(pos, tr, r)


def _silu(x):
    return x * _sigmoid(x)


def _log_sigmoid(x):
    return jnp.minimum(x, 0.0) - jnp.log(1.0 + jnp.exp(-jnp.abs(x)))


def _rms_mod(x, g, sc, sh):
    ms = jnp.mean(x * x, axis=-1, keepdims=True)
    return (x * lax.rsqrt(ms + EPS) * g) * (1.0 + sc) + sh


def _head_norm_gate(o, nw, r):
    ms = jnp.mean(o * o, axis=-1, keepdims=True)
    return o * lax.rsqrt(ms + EPS) * nw * _silu(r)


def _shift_lanes(x, off, width):
    if off == 0:
        return x[:, :width]
    return pltpu.roll(x, x.shape[1] - off, 1)[:, :width]


def _const_spec(a):
    return pl.BlockSpec(a.shape, lambda *_: (0,) * a.ndim)


def _layer_spec(a, layer):
    return pl.BlockSpec((None,) + a.shape[1:], lambda *_: (layer,) + (0,) * (a.ndim - 1))


def _ada_kernel(c_ref, w_ref, b_ref, ms_ref, mp_ref, *, n_s, n_p):
    act = _silu(c_ref[...]).astype(BF16)
    res = jnp.dot(act, w_ref[...].astype(BF16), preferred_element_type=F32) + b_ref[...]
    ms_ref[...] = res[:n_s]
    for i in range(n_p):
        mp_ref[i] = res[n_s + i:n_s + i + 1]


def _ada(c_all, w_ada, b_ada, n_s, n_p, tn=1024):
    L, D, N = w_ada.shape
    R = c_all.shape[0]
    return pl.pallas_call(
        functools.partial(_ada_kernel, n_s=n_s, n_p=n_p),
        grid=(L, N // tn),
        in_specs=[pl.BlockSpec((R, D), lambda l, j: (0, 0)),
                  pl.BlockSpec((None, D, tn), lambda l, j: (l, 0, j)),
                  pl.BlockSpec((None, 1, tn), lambda l, j: (l, 0, j))],
        out_specs=[pl.BlockSpec((None, n_s, tn), lambda l, j: (l, 0, j)),
                   pl.BlockSpec((None, n_p, 1, tn), lambda l, j: (l, 0, 0, j))],
        out_shape=[jax.ShapeDtypeStruct((L, n_s, N), F32),
                   jax.ShapeDtypeStruct((L, n_p, 1, N), F32)],
        compiler_params=_params("parallel", "parallel"),
        name="ada_mod",
    )(c_all, w_ada, b_ada.reshape(L, 1, N))


def _mod_spec(per_row, layer, tm, D, seq_len, part):
    if per_row:
        return pl.BlockSpec((None, tm, D), lambda i, *_: (layer, i, part))
    return pl.BlockSpec((None, None, 1, D), lambda i, *_: (layer, (i * tm) // seq_len, 0, part))


def _row_chunks(tm):
    return 4 if tm % 512 == 0 else 1


def _rows(ref, rows):
    return ref[...] if ref.shape[0] == 1 else ref[rows, :]


def _w_spec(w, wl, block, index):
    if w.ndim == 2:
        return pl.BlockSpec(block, index)
    return pl.BlockSpec((None,) + block, lambda *g: (wl,) + index(*g))


def _in_kernel(*refs, gate_block, n_chunks, cast_w):
    x_ref, g_ref, sc_ref, sh_ref, w_ref, z_ref = refs[:6]
    rest = list(refs[6:])
    zg_ref = rest.pop(0) if gate_block is not None else None
    wb_ref = rest.pop(0) if cast_w else None
    (h_scr,) = rest
    j = pl.program_id(1)
    rc = x_ref.shape[0] // n_chunks
    if cast_w:
        wb_ref[...] = w_ref[...].astype(BF16)
        w_ref = wb_ref

    @pl.when(j == 0)
    def _():
        for r in range(n_chunks):
            rows = slice(r * rc, (r + 1) * rc)
            h = _rms_mod(x_ref[rows, :], g_ref[...], _rows(sc_ref, rows), _rows(sh_ref, rows)).astype(BF16)
            h_scr[rows, :] = h
            z_ref[rows, :] = jnp.dot(h, w_ref[...], preferred_element_type=F32).astype(z_ref.dtype)

    @pl.when(j != 0)
    def _():
        acc = jnp.dot(h_scr[...], w_ref[...], preferred_element_type=F32)
        z_ref[...] = acc.astype(z_ref.dtype)
        if gate_block is not None:
            @pl.when(j == gate_block)
            def _():
                zg_ref[...] = acc


def _in_proj(x, norm_g, mod, w, layer, wl, *, tm, tn, per_row, seq_len, out_dtype, gate_block=None):
    M, D = x.shape
    N = w.shape[-1]
    cast_w = w.dtype != BF16
    assert gate_block != 0 and N // tn >= 2
    out_specs = [pl.BlockSpec((tm, tn), lambda i, j: (i, j))]
    out_shape = [jax.ShapeDtypeStruct((M, N), out_dtype)]
    if gate_block is not None:
        out_specs.append(pl.BlockSpec((tm, tn), lambda i, j: (i, 0)))
        out_shape.append(jax.ShapeDtypeStruct((M, tn), F32))
    if cast_w:
        assert M == tm
        out_specs.append(pl.BlockSpec((D, tn), lambda i, j: (0, j)))
        out_shape.append(jax.ShapeDtypeStruct((D, N), BF16))
    return pl.pallas_call(
        functools.partial(_in_kernel, gate_block=gate_block, n_chunks=_row_chunks(tm), cast_w=cast_w),
        grid=(M // tm, N // tn),
        in_specs=[pl.BlockSpec((tm, D), lambda i, j: (i, 0)),
                  _layer_spec(norm_g, layer),
                  _mod_spec(per_row, layer, tm, D, seq_len, 1),
                  _mod_spec(per_row, layer, tm, D, seq_len, 0),
                  _w_spec(w, wl, (D, tn), lambda i, j: (0, j))],
        out_specs=out_specs,
        out_shape=out_shape,
        scratch_shapes=[pltpu.VMEM((tm, D), BF16)],
        compiler_params=_params("parallel", "arbitrary"),
        name="in_proj",
    )(x, norm_g, mod, mod, w)


def _out_kernel(a_ref, b_ref, w_ref, x_ref, g_ref, o_ref, *wb_ref):
    ka = a_ref.shape[1]
    w = w_ref[...]
    if wb_ref:
        w = w.astype(BF16)
        wb_ref[0][...] = w
    y = jnp.dot(a_ref[...], w[:ka], preferred_element_type=F32)
    y = y + jnp.dot(b_ref[...], w[ka:], preferred_element_type=F32)
    o_ref[...] = x_ref[...] + g_ref[...] * y


def _out_proj_sample(a, b, w_out, wl, x, mod, layer, *, tn):
    M, D = x.shape
    Ka, Kb = a.shape[1], b.shape[1]
    nj = D // tn
    out_specs = [pl.BlockSpec((M, tn), lambda j: (0, j))]
    out_shape = [jax.ShapeDtypeStruct((M, D), F32)]
    if w_out.dtype != BF16:
        out_specs.append(pl.BlockSpec((Ka + Kb, tn), lambda j: (0, j)))
        out_shape.append(jax.ShapeDtypeStruct((Ka + Kb, D), BF16))
    return pl.pallas_call(
        _out_kernel,
        grid=(nj,),
        in_specs=[pl.BlockSpec((M, Ka), lambda j: (0, 0)),
                  pl.BlockSpec((M, Kb), lambda j: (0, 0)),
                  _w_spec(w_out, wl, (Ka + Kb, tn), lambda j: (0, j)),
                  pl.BlockSpec((M, tn), lambda j: (0, j)),
                  pl.BlockSpec((None, M, tn), lambda j: (layer, 0, 2 * nj + j))],
        out_specs=out_specs,
        out_shape=out_shape,
        compiler_params=_params("parallel"),
        name="out_proj_sample",
    )(a, b, w_out, x, mod)


def _mlp_kernel(*refs, final_norm, n_chunks, cast_w, n_next):
    x_ref, g_ref, sc_ref, sh_ref, gate_ref, wu_ref, wd_ref = refs[:7]
    rest = list(refs[7:])
    gf_ref = rest.pop(0) if final_norm else None
    next_in = [rest.pop(0) for _ in range(n_next)]
    o_ref = rest.pop(0)
    wub_ref, wdb_ref = (rest.pop(0), rest.pop(0)) if cast_w else (None, None)
    next_out = [rest.pop(0) for _ in range(n_next)]
    h_scr, acc_scr = rest
    f = pl.program_id(1)
    last = pl.num_programs(1) - 1
    rc = x_ref.shape[0] // n_chunks
    if cast_w:
        wub_ref[...] = wu_ref[...].astype(BF16)
        wdb_ref[...] = wd_ref[...].astype(BF16)
        wu_ref, wd_ref = wub_ref, wdb_ref

    def cast_next():
        for src, dst in zip(next_in, next_out):
            dst[...] = src[...].astype(BF16)

    def up_down(h):
        a = jnp.dot(h, wu_ref[...], preferred_element_type=F32)
        a = jnp.square(jnp.maximum(a, 0.0)).astype(BF16)
        return jnp.dot(a, wd_ref[...], preferred_element_type=F32)

    @pl.when(f == 0)
    def _():
        cast_next()
        for r in range(n_chunks):
            rows = slice(r * rc, (r + 1) * rc)
            h = _rms_mod(x_ref[rows, :], g_ref[...], _rows(sc_ref, rows), _rows(sh_ref, rows)).astype(BF16)
            h_scr[rows, :] = h
            acc_scr[rows, :] = up_down(h)

    @pl.when((f != 0) & (f != last))
    def _():
        cast_next()
        acc_scr[...] += up_down(h_scr[...])

    @pl.when(f == last)
    def _():
        cast_next()
        for r in range(n_chunks):
            rows = slice(r * rc, (r + 1) * rc)
            y = x_ref[rows, :] + _rows(gate_ref, rows) * (acc_scr[rows, :] + up_down(h_scr[rows, :]))
            if final_norm:
                ms = jnp.mean(y * y, axis=-1, keepdims=True)
                y = y * lax.rsqrt(ms + EPS) * gf_ref[...]
            o_ref[rows, :] = y


def _mlp(x, norm_g, mod, w_up, w_down, layer, norm_final, *, tm, tf, per_row, seq_len, cast_next=()):
    M, D = x.shape
    FF = w_up.shape[-1]
    final_norm = norm_final is not None
    cast_w = w_up.dtype != BF16
    assert FF // tf >= 2
    nf = FF // tf
    n_steps = (M // tm) * nf
    in_specs = [pl.BlockSpec((tm, D), lambda i, f: (i, 0)),
                _layer_spec(norm_g, layer),
                _mod_spec(per_row, layer, tm, D, seq_len, 4),
                _mod_spec(per_row, layer, tm, D, seq_len, 3),
                _mod_spec(per_row, layer, tm, D, seq_len, 5),
                _w_spec(w_up, layer, (D, tf), lambda i, f: (0, f)),
                _w_spec(w_down, layer, (tf, D), lambda i, f: (f, 0))]
    args = [x, norm_g, mod, mod, mod, w_up, w_down]
    if final_norm:
        in_specs.append(_const_spec(norm_final))
        args.append(norm_final)
    out_specs = [pl.BlockSpec((tm, D), lambda i, f: (i, 0))]
    out_shape = [jax.ShapeDtypeStruct((M, D), F32)]
    if cast_w:
        assert M == tm
        out_specs += [pl.BlockSpec((D, tf), lambda i, f: (0, f)), pl.BlockSpec((tf, D), lambda i, f: (f, 0))]
        out_shape += [jax.ShapeDtypeStruct((D, FF), BF16), jax.ShapeDtypeStruct((FF, D), BF16)]
    for w_next, wl in cast_next:
        _, R, Cn = w_next.shape
        rows = R // n_steps
        assert rows * n_steps == R and rows % 16 == 0
        in_specs.append(pl.BlockSpec((None, rows, Cn), lambda i, f, wl=wl: (wl, i * nf + f, 0)))
        args.append(w_next)
        out_specs.append(pl.BlockSpec((rows, Cn), lambda i, f: (i * nf + f, 0)))
        out_shape.append(jax.ShapeDtypeStruct((R, Cn), BF16))
    return pl.pallas_call(
        functools.partial(_mlp_kernel, final_norm=final_norm, n_chunks=_row_chunks(tm), cast_w=cast_w,
                          n_next=len(cast_next)),
        grid=(M // tm, FF // tf),
        in_specs=in_specs,
        out_specs=out_specs,
        out_shape=out_shape,
        scratch_shapes=[pltpu.VMEM((tm, D), BF16), pltpu.VMEM((tm, D), F32)],
        compiler_params=_params("parallel", "arbitrary"),
        name="mlp",
    )(*args)


def _gla_log_decay(alr, wg, bg):
    pre = jnp.dot(alr.astype(BF16), wg, preferred_element_type=F32) + bg
    return _log_sigmoid(pre) / GLA_GATE_TEMP


def _hgrn_lower_bound(lbraw_ref, layer):
    rows = [lbraw_ref[j:j + 1, :] for j in range(lbraw_ref.shape[0])]
    mx = functools.reduce(jnp.maximum, rows)
    es = [jnp.exp(r - mx) for r in rows]
    tot = functools.reduce(lambda a, b: a + b, es)
    sm = [e / tot for e in es]
    cum = functools.reduce(lambda a, b: a + b, sm[:layer + 1])
    return cum - sm[0]


def _hgrn_gates(fz, lb):
    sig, sig_neg = _sigmoid_pair(fz)
    f = lb + (1.0 - lb) * sig
    g = jnp.log(jnp.maximum(f, MIN_GATE))
    k = (1.0 - lb) * sig_neg
    return g, k


def _level_ref(b_scr, ls, m, C):
    W = ls.stop - ls.start
    if 2 * m >= 2 * SUBLANES:
        pieces = [jnp.broadcast_to(b_scr[n * 2 * m + m - 1:n * 2 * m + m, ls], (2 * m, W))
                  for n in range(C // (2 * m))]
    elif m == 4:
        pieces = [jnp.broadcast_to(b_scr[j * 8 + 3:j * 8 + 4, ls], (8, W)) for j in range(C // 8)]
    else:
        sub = lax.broadcasted_iota(jnp.int32, (8, W), 0)
        pieces = [jnp.where(sub < 4,
                            jnp.broadcast_to(b_scr[j * 8 + 1:j * 8 + 2, ls], (8, W)),
                            jnp.broadcast_to(b_scr[j * 8 + 5:j * 8 + 6, ls], (8, W)))
                  for j in range(C // 8)]
    return jnp.concatenate(pieces, axis=0)


def _rec_chunk(q, k, g, v_ref, r_ref, nw, o_scr, st_scr, b_scr, *, heads, dk, dv, C):
    W = len(heads) * dk
    ls = slice(heads[0] * dk, heads[0] * dk + W)

    row = lax.broadcasted_iota(jnp.int32, (C, C), 0)
    col = lax.broadcasted_iota(jnp.int32, (C, C), 1)
    tril = jnp.where(row >= col, 1.0, 0.0).astype(BF16)
    g1 = g.astype(BF16)
    rem = g - g1.astype(F32)
    g2 = rem.astype(BF16)
    g3 = (rem - g2.astype(F32)).astype(BF16)
    b = (jnp.dot(tril, g1, preferred_element_type=F32)
         + jnp.dot(tril, g2, preferred_element_type=F32)
         + jnp.dot(tril, g3, preferred_element_type=F32))
    b_scr[:, ls] = b

    H = len(heads)
    scores = [None] * H
    levels = []
    m = C // 2
    while m >= 1:
        levels.append(m)
        m //= 2
    sub = lax.broadcasted_iota(jnp.int32, (SUBLANES, W), 0)
    for m in levels + [0]:
        if m == 0:
            qe, ke = q.astype(BF16), k.astype(BF16)
            mask = row == col
        else:
            if m >= SUBLANES:
                src = jnp.concatenate(
                    [a[n * 2 * m + o:n * 2 * m + o + m] for n in range(C // (2 * m))
                     for a, o in ((k, 0), (q, m))], axis=0)
            else:
                upper = (sub & (2 * m - 1)) >= m
                src = jnp.concatenate(
                    [jnp.where(upper, q[j * SUBLANES:(j + 1) * SUBLANES],
                               k[j * SUBLANES:(j + 1) * SUBLANES])
                     for j in range(C // SUBLANES)], axis=0)
            if m == 1:
                odd = (sub & 1) == 1
                e = jnp.concatenate(
                    [jnp.where(odd, jnp.exp(g[j * SUBLANES:(j + 1) * SUBLANES]), 1.0)
                     for j in range(C // SUBLANES)], axis=0)
            else:
                e = jnp.exp(-jnp.abs(b - _level_ref(b_scr, ls, m, C)))
            qe = ke = (src * e).astype(BF16)
            sh = (2 * m).bit_length() - 1
            mask = (((row >> sh) == (col >> sh))
                    & ((row & (2 * m - 1)) >= m) & ((col & (2 * m - 1)) < m))
        for h in range(H):
            hs = slice(h * dk, (h + 1) * dk)
            p = lax.dot_general(qe[:, hs], ke[:, hs], _NT, preferred_element_type=F32)
            scores[h] = jnp.where(mask, p, 0.0 if scores[h] is None else scores[h])

    b_end = b_scr[C - 1:C, ls]
    qi = (q * jnp.exp(b)).astype(BF16)
    kend = (k * jnp.exp(b_end - b)).astype(BF16)
    a_end = jnp.exp(b_end)
    for h, head in enumerate(heads):
        hs = slice(h * dk, (h + 1) * dk)
        vs = slice(head * dv, (head + 1) * dv)
        v_h = v_ref[:, vs]
        st = st_scr[head]
        o = lax.dot_general(qi[:, hs], st.astype(BF16), _NT, preferred_element_type=F32)
        o = o + jnp.dot(scores[h].astype(BF16), v_h, preferred_element_type=F32)
        st_scr[head] = st * a_end[:, hs] + jnp.dot(v_h.astype(F32).T.astype(BF16), kend[:, hs],
                                                   preferred_element_type=F32)
        o_scr[:, vs] = _head_norm_gate(o, nw, r_ref[:, vs].astype(F32)).astype(o_scr.dtype)


def _pool_chunk(u, pw_ref, ps_ref, p_scr, buf, *, C, gc, chunk):
    PB = POOL_PAST + 1
    buf[PB:PB + C, :] = u
    pos = chunk * C + lax.broadcasted_iota(jnp.int32, (C, 1), 0)
    for g, w in enumerate(POOL_WINDOWS):
        cs = slice(g * gc, (g + 1) * gc)
        cur = u[:, cs]
        win = cur
        for i in range(1, w):
            win = win + buf[PB - i:PB - i + C, cs]
        cnt = jnp.minimum(pos + 1, w).astype(F32)
        d = win / cnt - cur
        y = jnp.dot(d.astype(BF16), pw_ref[g], preferred_element_type=F32) * ps_ref[:, cs]
        p_scr[:, cs] = y.astype(p_scr.dtype)
    buf[0:PB, :] = buf[C:C + PB, :]


def _ln_silu(x, g, b):
    mu = jnp.mean(x, axis=-1, keepdims=True)
    xc = x - mu
    var = jnp.mean(xc * xc, axis=-1, keepdims=True)
    return _silu(xc * lax.rsqrt(var + EPS) * g + b)


def _conv_chunk(cv_in, cw_ref, cb_ref, lg_ref, lb_ref, p_scr, buf, ph, *, C):
    HB = CONV_HIST
    S = SUBLANES
    rt = CONV_ROWS
    buf[HB:HB + C, :] = cv_in
    for p in range(1, S):
        ph[p - 1, S:HB + C, :] = buf[S - p:HB + C - p, :]
    n_sub = rt // S
    for t in range(C // rt):
        accs = [None] * n_sub
        for back in range(CONV_W):
            a, p = divmod(back, S)
            j = CONV_PAST - back
            wt = cw_ref[j * S:(j + 1) * S, :]
            for q in range(n_sub):
                r0 = HB + t * rt + q * S - a * S
                src = buf[r0:r0 + S, :] if p == 0 else ph[p - 1, r0:r0 + S, :]
                term = src * wt
                accs[q] = term if accs[q] is None else accs[q] + term
        acc = jnp.concatenate(accs, axis=0)
        y = _ln_silu(acc + cb_ref[...], lg_ref[...], lb_ref[...])
        p_scr[t * rt:(t + 1) * rt, :] = y.astype(p_scr.dtype)
    buf[0:HB, :] = buf[C:C + HB, :]


def _mixer_kernel(*refs, mode, H, dk, dv, C, n_sub, layer, gc, u_off, cols, widths):
    if mode == "gla":
        (z_ref, wg_ref, bg_ref, nw_ref, pw_ref, ps_ref,
         wa_ref, wb_ref, x_ref, g1_ref, xo_ref, s_ref, sh_ref,
         st_scr, b_scr, o_scr, p_scr, buf) = refs
        hist0, hist_rows = POOL_PAST + 1, POOL_PAST
    else:
        (z_ref, zg_ref, lbraw_ref, nw_ref, cw_ref, cb_ref, lg_ref, lb_ref,
         wa_ref, wb_ref, x_ref, g1_ref, xo_ref, s_ref, sh_ref,
         st_scr, b_scr, o_scr, p_scr, buf, ph) = refs
        hist0, hist_rows = CONV_HIST, CONV_PAST
    c = pl.program_id(1)

    @pl.when(c == 0)
    def _():
        st_scr[...] = jnp.zeros_like(st_scr)
        buf[0:hist0, :] = jnp.zeros((hist0, buf.shape[1]), F32)

    for sub in range(n_sub):
        rows = slice(sub * C, (sub + 1) * C)
        zc = [z_ref.at[rows, s:s + w] for s, w in zip(cols, widths)]
        if mode == "gla":
            q_ref, k_ref, v_ref, r_ref, alr_ref, u_ref = zc
            u = _shift_lanes(u_ref[...].astype(F32), u_off, p_scr.shape[1])
            _pool_chunk(u, pw_ref, ps_ref, p_scr, buf, C=C, gc=gc, chunk=c * n_sub + sub)
        else:
            q_ref, _, v_ref, r_ref, ga_ref, gb_ref = zc
            fz_ref = zg_ref.at[rows, :]
            cv_in = ga_ref[...].astype(F32) * _sigmoid(gb_ref[...].astype(F32))
            _conv_chunk(cv_in, cw_ref, cb_ref, lg_ref, lb_ref, p_scr, buf, ph, C=C)
        y = jnp.dot(p_scr[...], wb_ref[...], preferred_element_type=F32)
        n_grp = 2 if mode == "hgrn" else 1
        hg = H // n_grp
        for grp in range(n_grp):
            heads = tuple(range(grp * hg, (grp + 1) * hg))
            ls = slice(heads[0] * dk, (heads[-1] + 1) * dk)
            vs = slice(heads[0] * dv, (heads[-1] + 1) * dv)
            if mode == "gla":
                q = q_ref[:, ls].astype(F32) * (dk ** -0.5)
                k = k_ref[:, ls].astype(F32)
                g = _gla_log_decay(alr_ref[...], wg_ref[:, ls], bg_ref[:, ls])
            else:
                q = q_ref[:, ls].astype(F32)
                g, k = _hgrn_gates(fz_ref[:, ls], _hgrn_lower_bound(lbraw_ref, layer)[:, ls])
            _rec_chunk(q, k, g, v_ref, r_ref, nw_ref[...], o_scr, st_scr, b_scr,
                       heads=heads, dk=dk, dv=dv, C=C)
            y = y + jnp.dot(o_scr[:, vs], wa_ref[vs, :], preferred_element_type=F32)
        xo_ref[rows, :] = x_ref[rows, :] + g1_ref[...] * y

    @pl.when(c == pl.num_programs(1) - 1)
    def _():
        for h in range(H):
            s_ref[h] = st_scr[h].T
        sh_ref[...] = buf[C + hist0 - hist_rows:C + hist0, :]


def _mixer_prompt(mode, z, zg, cols, widths, consts, w_out, x, mod, layer, el, *, H, dk, dv, hist_len, u_off=0):
    B, T, _ = z.shape
    D = x.shape[-1]
    C = REC_CHUNK
    V = H * dv
    PW = D - V
    assert T % C == 0
    n_sub = 2 if T % (2 * C) == 0 else 1
    R = C * n_sub

    in_specs = [pl.BlockSpec((None, R, z.shape[-1]), lambda b, c: (b, c, 0))]
    args = [z]
    if mode == "hgrn":
        in_specs.append(pl.BlockSpec((None, R, zg.shape[-1]), lambda b, c: (b, c, 0)))
        args.append(zg)
    for a in consts:
        in_specs.append(_layer_spec(a, el) if a.ndim >= 3 else _const_spec(a))
        args.append(a)
    in_specs += [pl.BlockSpec((V, D), lambda b, c: (0, 0)),
                 pl.BlockSpec((PW, D), lambda b, c: (V // PW, 0)),
                 pl.BlockSpec((None, R, D), lambda b, c: (b, c, 0)),
                 pl.BlockSpec((None, None, 1, D), lambda b, c: (layer, b, 0, 2))]
    args += [w_out, w_out, x, mod]
    scratch = [pltpu.VMEM((H, dv, dk), F32), pltpu.VMEM((C, H * dk), F32),
               pltpu.VMEM((C, V), BF16), pltpu.VMEM((C, PW), BF16)]
    if mode == "gla":
        scratch.append(pltpu.VMEM((POOL_PAST + 1 + C, PW), F32))
    else:
        scratch += [pltpu.VMEM((CONV_HIST + C, PW), F32),
                    pltpu.VMEM((SUBLANES - 1, CONV_HIST + C, PW), F32)]
    return pl.pallas_call(
        functools.partial(_mixer_kernel, mode=mode, H=H, dk=dk, dv=dv, C=C, n_sub=n_sub, layer=el,
                          gc=PW // len(POOL_WINDOWS), u_off=u_off, cols=cols, widths=widths),
        grid=(B, T // R),
        in_specs=in_specs,
        out_specs=[pl.BlockSpec((None, R, D), lambda b, c: (b, c, 0)),
                   pl.BlockSpec((None, H, dk, dv), lambda b, c: (b, 0, 0, 0)),
                   pl.BlockSpec((None, hist_len, PW), lambda b, c: (b, 0, 0))],
        out_shape=[jax.ShapeDtypeStruct((B, T, D), F32),
                   jax.ShapeDtypeStruct((B, H, dk, dv), F32),
                   jax.ShapeDtypeStruct((B, hist_len, PW), F32)],
        scratch_shapes=scratch,
        compiler_params=_params("parallel", "arbitrary"),
        name="mixer_" + mode,
    )(*args)


def _step_recurrence(a, k, q, v_ref, s_ref, so_ref, tile_scr, row_scr, *, H, dk, dv, bb):
    for h in range(H):
        hs = slice(h * dk, (h + 1) * dk)
        tile_scr[...] = jnp.zeros_like(tile_scr)
        tile_scr[0:bb, :] = a[:, hs]
        tile_scr[bb:2 * bb, :] = k[:, hs]
        tile_scr[2 * bb:3 * bb, :] = q[:, hs]
        cols = tile_scr[...].T
        for s in range(bb):
            a_col = cols[:, s:s + 1]
            k_col = cols[:, bb + s:bb + s + 1]
            q_col = cols[:, 2 * bb + s:2 * bb + s + 1]
            v_row = v_ref[s:s + 1, h * dv:(h + 1) * dv]
            s_new = a_col * s_ref[s, h] + k_col * v_row
            so_ref[s, h] = s_new
            row_scr[s:s + 1, h * dv:(h + 1) * dv] = jnp.sum(q_col * s_new, axis=0, keepdims=True)


def _own_slab(ref, el, n_alias):
    if n_alias:
        return ref
    for j in range(ref.shape[0]):
        if j != el:
            ref[j] = jnp.zeros(ref.shape[1:], ref.dtype)
    return ref.at[el]


def _even1_kernel(*refs, H, dk, dv, bb, gc, n_alias, u_off, el, cols, widths):
    z_ref, sg_ref, sp_ref, wg_ref, bg_ref, nw_ref, pw_ref, ps_ref = refs[:8]
    q_ref, k_ref, v_ref, r_ref, alr_ref, ub_ref = [z_ref.at[:, s:s + w] for s, w in zip(cols, widths)]
    o_ref, p_ref, sgo_ref, spo_ref, tile_scr, row_scr, d_scr, u_scr = refs[8 + n_alias:]
    sgo_ref, spo_ref = _own_slab(sgo_ref, el, n_alias), _own_slab(spo_ref, el, n_alias)
    g = _gla_log_decay(alr_ref[...], wg_ref[...], bg_ref[...])
    _step_recurrence(jnp.exp(g), k_ref[...], q_ref[...] * (dk ** -0.5), v_ref, sg_ref, sgo_ref,
                     tile_scr, row_scr, H=H, dk=dk, dv=dv, bb=bb)
    nw = nw_ref[...]
    for h in range(H):
        vs = slice(h * dv, (h + 1) * dv)
        o_ref[:, vs] = _head_norm_gate(row_scr[:, vs], nw, r_ref[:, vs]).astype(o_ref.dtype)

    u_scr[...] = _shift_lanes(ub_ref[...], u_off, u_scr.shape[1])
    for s in range(bb):
        for g_i, w in enumerate(POOL_WINDOWS):
            cs = slice(g_i * gc, (g_i + 1) * gc)
            cur = u_scr[s:s + 1, cs]
            win = cur + jnp.sum(sp_ref[s, POOL_PAST - (w - 1):POOL_PAST, cs], axis=0, keepdims=True)
            cnt = float(min(PAST_LEN + 1, w))
            d_scr[s:s + 1, cs] = win / cnt - cur
        spo_ref[s, 0:POOL_PAST - 1, :] = sp_ref[s, 1:POOL_PAST, :]
        spo_ref[s, POOL_PAST - 1:POOL_PAST, :] = u_scr[s:s + 1, :]
    for g_i in range(len(POOL_WINDOWS)):
        cs = slice(g_i * gc, (g_i + 1) * gc)
        y = jnp.dot(d_scr[:, cs].astype(BF16), pw_ref[g_i], preferred_element_type=F32) * ps_ref[:, cs]
        p_ref[:, cs] = y.astype(p_ref.dtype)


def _odd1_kernel(*refs, H, dk, dv, bb, n_alias, el, cols, widths):
    z_ref, sh_ref, sc_ref, lbraw_ref, nw_ref, cw_ref, cb_ref, lg_ref, lb_ref = refs[:9]
    q_ref, fz_ref, v_ref, r_ref, ga_ref, gb_ref = [z_ref.at[:, s:s + w] for s, w in zip(cols, widths)]
    o_ref, cv_ref, sho_ref, sco_ref, tile_scr, row_scr, d_scr = refs[9 + n_alias:]
    sho_ref, sco_ref = _own_slab(sho_ref, el, n_alias), _own_slab(sco_ref, el, n_alias)
    g, k = _hgrn_gates(fz_ref[...], _hgrn_lower_bound(lbraw_ref, el))
    _step_recurrence(jnp.exp(g), k, q_ref[...], v_ref, sh_ref, sho_ref,
                     tile_scr, row_scr, H=H, dk=dk, dv=dv, bb=bb)
    nw = nw_ref[...]
    for h in range(H):
        vs = slice(h * dv, (h + 1) * dv)
        o_ref[:, vs] = _head_norm_gate(row_scr[:, vs], nw, r_ref[:, vs]).astype(o_ref.dtype)

    cv_in = ga_ref[...] * _sigmoid(gb_ref[...])
    d_scr[...] = cv_in * cw_ref[CONV_PAST:CONV_W, :]
    for s in range(bb):
        hist = sc_ref[s]
        d_scr[s:s + 1, :] += jnp.sum(hist * cw_ref[0:CONV_PAST, :], axis=0, keepdims=True)
        sco_ref[s, 0:CONV_PAST - 1, :] = sc_ref[s, 1:CONV_PAST, :]
        sco_ref[s, CONV_PAST - 1:CONV_PAST, :] = cv_in[s:s + 1, :]
    y = _ln_silu(d_scr[...] + cb_ref[...], lg_ref[...], lb_ref[...])
    cv_ref[...] = y.astype(cv_ref.dtype)


def _sample_mixer(kernel_fn, name, z, cols, widths, states, prev, consts, el, *, bb, extra_scratch=()):
    Bs = z.shape[0]
    s_mat, s_hist = states
    _, _, H, dk, dv = s_mat.shape
    hist_len, HW = s_hist.shape[2:]
    V = H * dv

    mat_spec = pl.BlockSpec((None, bb, H, dk, dv), lambda i: (el, i, 0, 0, 0))
    hist_spec = pl.BlockSpec((None, bb, hist_len, HW), lambda i: (el, i, 0, 0))
    in_specs = [pl.BlockSpec((bb, z.shape[1]), lambda i: (i, 0)), mat_spec, hist_spec]
    args = [z, s_mat, s_hist]
    for a in consts:
        in_specs.append(_layer_spec(a, el) if a.ndim >= 3 else _const_spec(a))
        args.append(a)
    aliases = {}
    if prev is not None:
        for n, p in enumerate(prev):
            aliases[len(args)] = 2 + n
            in_specs.append(pl.BlockSpec(memory_space=pl.ANY))
            args.append(p)
    n_alias = 0 if prev is None else len(prev)
    if prev is None:
        n_lay = s_mat.shape[0]
        mat_out = pl.BlockSpec((n_lay, bb, H, dk, dv), lambda i: (0, i, 0, 0, 0))
        hist_out = pl.BlockSpec((n_lay, bb, hist_len, HW), lambda i: (0, i, 0, 0))
    else:
        mat_out, hist_out = mat_spec, hist_spec
    return pl.pallas_call(
        functools.partial(kernel_fn, H=H, dk=dk, dv=dv, bb=bb, n_alias=n_alias, el=el,
                          cols=cols, widths=widths),
        grid=(Bs // bb,),
        in_specs=in_specs,
        out_specs=[pl.BlockSpec((bb, V), lambda i: (i, 0)),
                   pl.BlockSpec((bb, HW), lambda i: (i, 0)),
                   mat_out, hist_out],
        out_shape=[jax.ShapeDtypeStruct((Bs, V), BF16), jax.ShapeDtypeStruct((Bs, HW), BF16),
                   jax.ShapeDtypeStruct(s_mat.shape, F32), jax.ShapeDtypeStruct(s_hist.shape, F32)],
        scratch_shapes=[pltpu.VMEM((LANES, dk), F32), pltpu.VMEM((bb, V), F32),
                        pltpu.VMEM((bb, HW), F32)] + list(extra_scratch),
        input_output_aliases=aliases,
        compiler_params=_params("parallel"),
        name=name,
    )(*args)


def _pick(m, pref):
    return pref if m % pref == 0 else m


def kernel(x_prompt, x_sample, c_prompt, c_sample, state_gla, state_pool, state_hgrn, state_conv,
           w_ada, b_ada, norm_mix, norm_mlp, norm_final,
           ev_w_in, ev_w_gate_up, ev_b_gate, ev_gla_norm, ev_pool_w, ev_pool_scale, ev_w_out,
           od_w_in, od_lb_raw, od_hgrn_norm, od_conv_w, od_conv_b, od_ln_g, od_ln_b, od_w_out,
           mlp_w_up, mlp_w_down):
    B, T, D = x_prompt.shape
    Bs = x_sample.shape[0]
    depth = w_ada.shape[0]
    gla_w = D // 2
    gla_key_w = gla_w // 2
    gla_dk = gla_key_w // GLA_HEADS
    gla_dv = gla_w // GLA_HEADS
    pool_w = D - gla_w
    hgrn_w = D // 2
    hgrn_heads = hgrn_w // HGRN_DK
    hgrn_dv = hgrn_w // hgrn_heads
    conf_w = D - hgrn_w

    n_main = 2 * gla_key_w + 2 * gla_w
    n_ev = -(-ev_w_in.shape[-1] // LANES) * LANES
    ub_w = n_ev - n_main
    tn_ev = n_ev // 3
    assert GLA_GATE_RANK + pool_w <= ub_w and tn_ev * 3 == n_ev and tn_ev % LANES == 0
    ev_w = jnp.pad(ev_w_in, ((0, 0), (0, 0), (0, n_ev - ev_w_in.shape[-1])))
    ev_cols = (0, gla_key_w, 2 * gla_key_w, 2 * gla_key_w + gla_w, n_main, n_main)
    ev_widths = (gla_key_w, gla_key_w, gla_w, gla_w, LANES, ub_w)
    wg = jnp.pad(ev_w_gate_up, ((0, 0), (0, LANES - GLA_GATE_RANK), (0, 0))).astype(BF16)
    od_cols = (0, hgrn_w, 2 * hgrn_w, 3 * hgrn_w, 4 * hgrn_w, 4 * hgrn_w + conf_w)
    od_widths = (hgrn_w, hgrn_w, hgrn_w, hgrn_w, conf_w, conf_w)
    pool_wb = ev_pool_w.astype(BF16)
    conv_wr = jnp.repeat(od_conv_w, SUBLANES, axis=1)
    conv_wp = jnp.pad(od_conv_w, ((0, 0), (0, 1), (0, 0)))

    lay3 = lambda a: a.reshape(a.shape[0], 1, a.shape[1])
    norm_mix3, norm_mlp3 = lay3(norm_mix), lay3(norm_mlp)
    bg3, gn3, ps3 = lay3(ev_b_gate), lay3(ev_gla_norm), lay3(ev_pool_scale)
    hn3, cb3, lg3, lb3 = lay3(od_hgrn_norm), lay3(od_conv_b), lay3(od_ln_g), lay3(od_ln_b)

    pad = (-(B + Bs)) % 16
    c_all = jnp.concatenate([c_sample, c_prompt, jnp.zeros((pad, D), F32)], axis=0)
    mod_s, mod_p = _ada(c_all, w_ada, b_ada, Bs, B)

    xp = x_prompt.reshape(B * T, D)
    xs = x_sample.reshape(Bs, D)
    Mp = B * T
    tm_in = _pick(Mp, 1024)
    tm_p = _pick(Mp, 512)
    bb = 8

    new = {k: [] for k in ("gla_p", "pool_p", "hgrn_p", "conv_p")}
    ev_prev = od_prev = cast = None
    for l in range(depth):
        e = l // 2
        nf = norm_final.reshape(1, D) if l == depth - 1 else None
        kw_p = dict(per_row=False, seq_len=T)
        kw_s = dict(per_row=True, seq_len=1)
        w_in_f, w_out_f = (ev_w, ev_w_out) if l % 2 == 0 else (od_w_in, od_w_out)
        if cast is None:
            w_in_s, w_out_s, w_up_s, w_down_s, wl = w_in_f, w_out_f, mlp_w_up, mlp_w_down, e
        else:
            (w_up_s, w_down_s, w_in_s, w_out_s), wl = cast, None
        if l % 2 == 0:
            n_in = ev_w.shape[-1]
            tn = tn_ev
            zs, *w_in_b = _in_proj(xs, norm_mix3, mod_s, w_in_s, l, wl, tm=Bs, tn=tn, out_dtype=F32, **kw_s)
            a_s, b_s, sgs, sps = _sample_mixer(
                functools.partial(_even1_kernel, gc=pool_w // len(POOL_WINDOWS), u_off=GLA_GATE_RANK),
                "even_sample", zs, ev_cols, ev_widths, (state_gla, state_pool), ev_prev,
                (wg, bg3, gn3, pool_wb, ps3), e, bb=bb, extra_scratch=[pltpu.VMEM((bb, pool_w), F32)])
            ev_prev = (sgs, sps)
        else:
            n_in = od_w_in.shape[-1]
            tn = _pick(n_in, 1024)
            zs, *w_in_b = _in_proj(xs, norm_mix3, mod_s, w_in_s, l, wl, tm=Bs, tn=tn, out_dtype=F32, **kw_s)
            a_s, b_s, shs, scs = _sample_mixer(
                _odd1_kernel, "odd_sample",
                zs, od_cols, od_widths, (state_hgrn, state_conv), od_prev,
                (od_lb_raw, hn3, conv_wp, cb3, lg3, lb3), e, bb=bb)
            od_prev = (shs, scs)
        xs, *w_out_b = _out_proj_sample(a_s, b_s, w_out_s, wl, xs, mod_s, l, tn=_pick(D, 512))
        xs, *w_mlp_b = _mlp(xs, norm_mlp3, mod_s, w_up_s, w_down_s, l, nf, tm=Bs,
                            tf=512 if cast is None else 1024, **kw_s)
        if cast is None:
            (w_in_b,), (w_out_b,), (w_up_b, w_down_b) = w_in_b, w_out_b, w_mlp_b
        else:
            w_up_b, w_down_b, w_in_b, w_out_b = cast

        xp3 = xp.reshape(B, T, D)
        if l % 2 == 0:
            (zp,) = _in_proj(xp, norm_mix3, mod_p, w_in_b, l, None, tm=tm_in, tn=tn, out_dtype=BF16, **kw_p)
            xp3, sg, sp = _mixer_prompt("gla", zp.reshape(B, T, n_in), None, ev_cols, ev_widths,
                                        (wg, bg3, gn3, pool_wb, ps3), w_out_b, xp3, mod_p, l, e,
                                        H=GLA_HEADS, dk=gla_dk, dv=gla_dv, hist_len=POOL_PAST,
                                        u_off=GLA_GATE_RANK)
            new["gla_p"].append(sg)
            new["pool_p"].append(sp)
        else:
            zp, zg = _in_proj(xp, norm_mix3, mod_p, w_in_b, l, None, tm=tm_in, tn=tn, out_dtype=BF16,
                              gate_block=od_cols[1] // tn, **kw_p)
            xp3, shg, scv = _mixer_prompt("hgrn", zp.reshape(B, T, n_in), zg.reshape(B, T, tn),
                                          od_cols, od_widths, (od_lb_raw, hn3, conv_wr, cb3, lg3, lb3),
                                          w_out_b, xp3, mod_p, l, e,
                                          H=hgrn_heads, dk=HGRN_DK, dv=hgrn_dv, hist_len=CONV_PAST)
            new["hgrn_p"].append(shg)
            new["conv_p"].append(scv)
        nxt = ()
        if l + 1 < depth:
            e_n = (l + 1) // 2
            w_in_n, w_out_n = (ev_w, ev_w_out) if (l + 1) % 2 == 0 else (od_w_in, od_w_out)
            nxt = ((mlp_w_up, l + 1), (mlp_w_down, l + 1), (w_in_n, e_n), (w_out_n, e_n))
        xp, *cast = _mlp(xp3.reshape(Mp, D), norm_mlp3, mod_p, w_up_b, w_down_b, l, nf,
                         tm=tm_p, tf=1024, cast_next=nxt, **kw_p)
        cast = cast or None

    st = lambda k: jnp.stack(new[k])
    return (xp.reshape(B, T, D), xs.reshape(Bs, 1, D),
            st("gla_p"), st("pool_p"), st("hgrn_p"), st("conv_p"),
            ev_prev[0], ev_prev[1], od_prev[0], od_prev[1])
```

```python
import functools

import jax
import jax.numpy as jnp
from jax import lax
from jax.experimental import pallas as pl
from jax.experimental.pallas import tpu as pltpu

F32 = jnp.float32
BF16 = jnp.bfloat16

EPS = 1e-6
MIN_GATE = 1e-30
PAST_LEN = 16384
GLA_HEADS = 4
GLA_GATE_RANK = 16
GLA_GATE_TEMP = 16.0
POOL_WINDOWS = (2, 4, 8, 16)
POOL_PAST = max(POOL_WINDOWS) - 1
CONV_W = 31
CONV_PAST = CONV_W - 1
HGRN_DK = 128
LANES = 128
SUBLANES = 8
REC_CHUNK = 128
CONV_HIST = 32
CONV_ROWS = 32
VMEM_LIMIT = 56 * 1024 * 1024

_NT = (((1,), (1,)), ((), ()))


def _params(*sem):
    return pltpu.CompilerParams(dimension_semantics=sem, vmem_limit_bytes=VMEM_LIMIT)


def _sigmoid(x):
    return 0.5 * jnp.tanh(0.5 * x) + 0.5


def _sigmoid_pair(x):
    t = jnp.exp(-jnp.abs(x))
    r = 1.0 / (1.0 + t)
    tr = t * r
    pos = x >= 0.0
    return jnp.where(pos, r, tr), jnp.where(pos, tr, r)


def _silu(x):
    return x * _sigmoid(x)


def _log_sigmoid(x):
    return jnp.minimum(x, 0.0) - jnp.log(1.0 + jnp.exp(-jnp.abs(x)))


def _rms_mod(x, g, sc, sh):
    ms = jnp.mean(x * x, axis=-1, keepdims=True)
    return (x * lax.rsqrt(ms + EPS) * g) * (1.0 + sc) + sh


def _head_norm_gate(o, nw, r):
    ms = jnp.mean(o * o, axis=-1, keepdims=True)
    return o * lax.rsqrt(ms + EPS) * nw * _silu(r)


def _shift_lanes(x, off, width):
    if off == 0:
        return x[:, :width]
    return pltpu.roll(x, x.shape[1] - off, 1)[:, :width]


def _const_spec(a):
    return pl.BlockSpec(a.shape, lambda *_: (0,) * a.ndim)


def _layer_spec(a, layer):
    return pl.BlockSpec((None,) + a.shape[1:], lambda *_: (layer,) + (0,) * (a.ndim - 1))


def _ada_kernel(c_ref, w_ref, b_ref, ms_ref, mp_ref, *, n_s, n_p):
    act = _silu(c_ref[...]).astype(BF16)
    res = jnp.dot(act, w_ref[...].astype(BF16), preferred_element_type=F32) + b_ref[...]
    ms_ref[...] = res[:n_s]
    for i in range(n_p):
        mp_ref[i] = res[n_s + i:n_s + i + 1]


def _ada(c_all, w_ada, b_ada, n_s, n_p, tn=1024):
    L, D, N = w_ada.shape
    R = c_all.shape[0]
    return pl.pallas_call(
        functools.partial(_ada_kernel, n_s=n_s, n_p=n_p),
        grid=(L, N // tn),
        in_specs=[pl.BlockSpec((R, D), lambda l, j: (0, 0)),
                  pl.BlockSpec((None, D, tn), lambda l, j: (l, 0, j)),
                  pl.BlockSpec((None, 1, tn), lambda l, j: (l, 0, j))],
        out_specs=[pl.BlockSpec((None, n_s, tn), lambda l, j: (l, 0, j)),
                   pl.BlockSpec((None, n_p, 1, tn), lambda l, j: (l, 0, 0, j))],
        out_shape=[jax.ShapeDtypeStruct((L, n_s, N), F32),
                   jax.ShapeDtypeStruct((L, n_p, 1, N), F32)],
        compiler_params=_params("parallel", "parallel"),
        name="ada_mod",
    )(c_all, w_ada, b_ada.reshape(L, 1, N))


def _mod_spec(per_row, layer, tm, D, seq_len, part):
    if per_row:
        return pl.BlockSpec((None, tm, D), lambda i, *_: (layer, i, part))
    return pl.BlockSpec((None, None, 1, D), lambda i, *_: (layer, (i * tm) // seq_len, 0, part))


def _row_chunks(tm):
    return 4 if tm % 512 == 0 else 1


def _rows(ref, rows):
    return ref[...] if ref.shape[0] == 1 else ref[rows, :]


def _w_spec(w, wl, block, index):
    if w.ndim == 2:
        return pl.BlockSpec(block, index)
    return pl.BlockSpec((None,) + block, lambda *g: (wl,) + index(*g))


def _in_kernel(*refs, gate_block, n_chunks, cast_w):
    x_ref, g_ref, sc_ref, sh_ref, w_ref, z_ref = refs[:6]
    rest = list(refs[6:])
    zg_ref = rest.pop(0) if gate_block is not None else None
    wb_ref = rest.pop(0) if cast_w else None
    (h_scr,) = rest
    j = pl.program_id(1)
    rc = x_ref.shape[0] // n_chunks
    if cast_w:
        wb_ref[...] = w_ref[...].astype(BF16)
        w_ref = wb_ref

    @pl.when(j == 0)
    def _():
        for r in range(n_chunks):
            rows = slice(r * rc, (r + 1) * rc)
            h = _rms_mod(x_ref[rows, :], g_ref[...], _rows(sc_ref, rows), _rows(sh_ref, rows)).astype(BF16)
            h_scr[rows, :] = h
            z_ref[rows, :] = jnp.dot(h, w_ref[...], preferred_element_type=F32).astype(z_ref.dtype)

    @pl.when(j != 0)
    def _():
        acc = jnp.dot(h_scr[...], w_ref[...], preferred_element_type=F32)
        z_ref[...] = acc.astype(z_ref.dtype)
        if gate_block is not None:
            @pl.when(j == gate_block)
            def _():
                zg_ref[...] = acc


def _in_proj(x, norm_g, mod, w, layer, wl, *, tm, tn, per_row, seq_len, out_dtype, gate_block=None):
    M, D = x.shape
    N = w.shape[-1]
    cast_w = w.dtype != BF16
    assert gate_block != 0 and N // tn >= 2
    out_specs = [pl.BlockSpec((tm, tn), lambda i, j: (i, j))]
    out_shape = [jax.ShapeDtypeStruct((M, N), out_dtype)]
    if gate_block is not None:
        out_specs.append(pl.BlockSpec((tm, tn), lambda i, j: (i, 0)))
        out_shape.append(jax.ShapeDtypeStruct((M, tn), F32))
    if cast_w:
        assert M == tm
        out_specs.append(pl.BlockSpec((D, tn), lambda i, j: (0, j)))
        out_shape.append(jax.ShapeDtypeStruct((D, N), BF16))
    return pl.pallas_call(
        functools.partial(_in_kernel, gate_block=gate_block, n_chunks=_row_chunks(tm), cast_w=cast_w),
        grid=(M // tm, N // tn),
        in_specs=[pl.BlockSpec((tm, D), lambda i, j: (i, 0)),
                  _layer_spec(norm_g, layer),
                  _mod_spec(per_row, layer, tm, D, seq_len, 1),
                  _mod_spec(per_row, layer, tm, D, seq_len, 0),
                  _w_spec(w, wl, (D, tn), lambda i, j: (0, j))],
        out_specs=out_specs,
        out_shape=out_shape,
        scratch_shapes=[pltpu.VMEM((tm, D), BF16)],
        compiler_params=_params("parallel", "arbitrary"),
        name="in_proj",
    )(x, norm_g, mod, mod, w)


def _out_kernel(a_ref, b_ref, w_ref, x_ref, g_ref, o_ref, *wb_ref):
    ka = a_ref.shape[1]
    w = w_ref[...]
    if wb_ref:
        w = w.astype(BF16)
        wb_ref[0][...] = w
    y = jnp.dot(a_ref[...], w[:ka], preferred_element_type=F32)
    y = y + jnp.dot(b_ref[...], w[ka:], preferred_element_type=F32)
    o_ref[...] = x_ref[...] + g_ref[...] * y


def _out_proj_sample(a, b, w_out, wl, x, mod, layer, *, tn):
    M, D = x.shape
    Ka, Kb = a.shape[1], b.shape[1]
    nj = D // tn
    out_specs = [pl.BlockSpec((M, tn), lambda j: (0, j))]
    out_shape = [jax.ShapeDtypeStruct((M, D), F32)]
    if w_out.dtype != BF16:
        out_specs.append(pl.BlockSpec((Ka + Kb, tn), lambda j: (0, j)))
        out_shape.append(jax.ShapeDtypeStruct((Ka + Kb, D), BF16))
    return pl.pallas_call(
        _out_kernel,
        grid=(nj,),
        in_specs=[pl.BlockSpec((M, Ka), lambda j: (0, 0)),
                  pl.BlockSpec((M, Kb), lambda j: (0, 0)),
                  _w_spec(w_out, wl, (Ka + Kb, tn), lambda j: (0, j)),
                  pl.BlockSpec((M, tn), lambda j: (0, j)),
                  pl.BlockSpec((None, M, tn), lambda j: (layer, 0, 2 * nj + j))],
        out_specs=out_specs,
        out_shape=out_shape,
        compiler_params=_params("parallel"),
        name="out_proj_sample",
    )(a, b, w_out, x, mod)


def _mlp_kernel(*refs, final_norm, n_chunks, cast_w, n_next):
    x_ref, g_ref, sc_ref, sh_ref, gate_ref, wu_ref, wd_ref = refs[:7]
    rest = list(refs[7:])
    gf_ref = rest.pop(0) if final_norm else None
    next_in = [rest.pop(0) for _ in range(n_next)]
    o_ref = rest.pop(0)
    wub_ref, wdb_ref = (rest.pop(0), rest.pop(0)) if cast_w else (None, None)
    next_out = [rest.pop(0) for _ in range(n_next)]
    h_scr, acc_scr = rest
    f = pl.program_id(1)
    last = pl.num_programs(1) - 1
    rc = x_ref.shape[0] // n_chunks
    if cast_w:
        wub_ref[...] = wu_ref[...].astype(BF16)
        wdb_ref[...] = wd_ref[...].astype(BF16)
        wu_ref, wd_ref = wub_ref, wdb_ref

    def cast_next():
        for src, dst in zip(next_in, next_out):
            dst[...] = src[...].astype(BF16)

    def up_down(h):
        a = jnp.dot(h, wu_ref[...], preferred_element_type=F32)
        a = jnp.square(jnp.maximum(a, 0.0)).astype(BF16)
        return jnp.dot(a, wd_ref[...], preferred_element_type=F32)

    @pl.when(f == 0)
    def _():
        cast_next()
        for r in range(n_chunks):
            rows = slice(r * rc, (r + 1) * rc)
            h = _rms_mod(x_ref[rows, :], g_ref[...], _rows(sc_ref, rows), _rows(sh_ref, rows)).astype(BF16)
            h_scr[rows, :] = h
            acc_scr[rows, :] = up_down(h)

    @pl.when((f != 0) & (f != last))
    def _():
        cast_next()
        acc_scr[...] += up_down(h_scr[...])

    @pl.when(f == last)
    def _():
        cast_next()
        for r in range(n_chunks):
            rows = slice(r * rc, (r + 1) * rc)
            y = x_ref[rows, :] + _rows(gate_ref, rows) * (acc_scr[rows, :] + up_down(h_scr[rows, :]))
            if final_norm:
                ms = jnp.mean(y * y, axis=-1, keepdims=True)
                y = y * lax.rsqrt(ms + EPS) * gf_ref[...]
            o_ref[rows, :] = y


def _mlp(x, norm_g, mod, w_up, w_down, layer, norm_final, *, tm, tf, per_row, seq_len, cast_next=()):
    M, D = x.shape
    FF = w_up.shape[-1]
    final_norm = norm_final is not None
    cast_w = w_up.dtype != BF16
    assert FF // tf >= 2
    nf = FF // tf
    n_steps = (M // tm) * nf
    in_specs = [pl.BlockSpec((tm, D), lambda i, f: (i, 0)),
                _layer_spec(norm_g, layer),
                _mod_spec(per_row, layer, tm, D, seq_len, 4),
                _mod_spec(per_row, layer, tm, D, seq_len, 3),
                _mod_spec(per_row, layer, tm, D, seq_len, 5),
                _w_spec(w_up, layer, (D, tf), lambda i, f: (0, f)),
                _w_spec(w_down, layer, (tf, D), lambda i, f: (f, 0))]
    args = [x, norm_g, mod, mod, mod, w_up, w_down]
    if final_norm:
        in_specs.append(_const_spec(norm_final))
        args.append(norm_final)
    out_specs = [pl.BlockSpec((tm, D), lambda i, f: (i, 0))]
    out_shape = [jax.ShapeDtypeStruct((M, D), F32)]
    if cast_w:
        assert M == tm
        out_specs += [pl.BlockSpec((D, tf), lambda i, f: (0, f)), pl.BlockSpec((tf, D), lambda i, f: (f, 0))]
        out_shape += [jax.ShapeDtypeStruct((D, FF), BF16), jax.ShapeDtypeStruct((FF, D), BF16)]
    for w_next, wl in cast_next:
        _, R, Cn = w_next.shape
        rows = R // n_steps
        assert rows * n_steps == R and rows % 16 == 0
        in_specs.append(pl.BlockSpec((None, rows, Cn), lambda i, f, wl=wl: (wl, i * nf + f, 0)))
        args.append(w_next)
        out_specs.append(pl.BlockSpec((rows, Cn), lambda i, f: (i * nf + f, 0)))
        out_shape.append(jax.ShapeDtypeStruct((R, Cn), BF16))
    return pl.pallas_call(
        functools.partial(_mlp_kernel, final_norm=final_norm, n_chunks=_row_chunks(tm), cast_w=cast_w,
                          n_next=len(cast_next)),
        grid=(M // tm, FF // tf),
        in_specs=in_specs,
        out_specs=out_specs,
        out_shape=out_shape,
        scratch_shapes=[pltpu.VMEM((tm, D), BF16), pltpu.VMEM((tm, D), F32)],
        compiler_params=_params("parallel", "arbitrary"),
        name="mlp",
    )(*args)


def _gla_log_decay(alr, wg, bg):
    pre = jnp.dot(alr.astype(BF16), wg, preferred_element_type=F32) + bg
    return _log_sigmoid(pre) / GLA_GATE_TEMP


def _hgrn_lower_bound(lbraw_ref, layer):
    rows = [lbraw_ref[j:j + 1, :] for j in range(lbraw_ref.shape[0])]
    mx = functools.reduce(jnp.maximum, rows)
    es = [jnp.exp(r - mx) for r in rows]
    tot = functools.reduce(lambda a, b: a + b, es)
    sm = [e / tot for e in es]
    cum = functools.reduce(lambda a, b: a + b, sm[:layer + 1])
    return cum - sm[0]


def _hgrn_gates(fz, lb):
    sig, sig_neg = _sigmoid_pair(fz)
    f = lb + (1.0 - lb) * sig
    g = jnp.log(jnp.maximum(f, MIN_GATE))
    k = (1.0 - lb) * sig_neg
    return g, k


def _level_ref(b_scr, ls, m, C):
    W = ls.stop - ls.start
    if 2 * m >= 2 * SUBLANES:
        pieces = [jnp.broadcast_to(b_scr[n * 2 * m + m - 1:n * 2 * m + m, ls], (2 * m, W))
                  for n in range(C // (2 * m))]
    elif m == 4:
        pieces = [jnp.broadcast_to(b_scr[j * 8 + 3:j * 8 + 4, ls], (8, W)) for j in range(C // 8)]
    else:
        sub = lax.broadcasted_iota(jnp.int32, (8, W), 0)
        pieces = [jnp.where(sub < 4,
                            jnp.broadcast_to(b_scr[j * 8 + 1:j * 8 + 2, ls], (8, W)),
                            jnp.broadcast_to(b_scr[j * 8 + 5:j * 8 + 6, ls], (8, W)))
                  for j in range(C // 8)]
    return jnp.concatenate(pieces, axis=0)


def _rec_chunk(q, k, g, v_ref, r_ref, nw, o_scr, st_scr, b_scr, *, heads, dk, dv, C):
    W = len(heads) * dk
    ls = slice(heads[0] * dk, heads[0] * dk + W)

    row = lax.broadcasted_iota(jnp.int32, (C, C), 0)
    col = lax.broadcasted_iota(jnp.int32, (C, C), 1)
    tril = jnp.where(row >= col, 1.0, 0.0).astype(BF16)
    g1 = g.astype(BF16)
    rem = g - g1.astype(F32)
    g2 = rem.astype(BF16)
    g3 = (rem - g2.astype(F32)).astype(BF16)
    b = (jnp.dot(tril, g1, preferred_element_type=F32)
         + jnp.dot(tril, g2, preferred_element_type=F32)
         + jnp.dot(tril, g3, preferred_element_type=F32))
    b_scr[:, ls] = b

    H = len(heads)
    scores = [None] * H
    levels = []
    m = C // 2
    while m >= 1:
        levels.append(m)
        m //= 2
    sub = lax.broadcasted_iota(jnp.int32, (SUBLANES, W), 0)
    for m in levels + [0]:
        if m == 0:
            qe, ke = q.astype(BF16), k.astype(BF16)
            mask = row == col
        else:
            if m >= SUBLANES:
                src = jnp.concatenate(
                    [a[n * 2 * m + o:n * 2 * m + o + m] for n in range(C // (2 * m))
                     for a, o in ((k, 0), (q, m))], axis=0)
            else:
                upper = (sub & (2 * m - 1)) >= m
                src = jnp.concatenate(
                    [jnp.where(upper, q[j * SUBLANES:(j + 1) * SUBLANES],
                               k[j * SUBLANES:(j + 1) * SUBLANES])
                     for j in range(C // SUBLANES)], axis=0)
            if m == 1:
                odd = (sub & 1) == 1
                e = jnp.concatenate(
                    [jnp.where(odd, jnp.exp(g[j * SUBLANES:(j + 1) * SUBLANES]), 1.0)
                     for j in range(C // SUBLANES)], axis=0)
            else:
                e = jnp.exp(-jnp.abs(b - _level_ref(b_scr, ls, m, C)))
            qe = ke = (src * e).astype(BF16)
            sh = (2 * m).bit_length() - 1
            mask = (((row >> sh) == (col >> sh))
                    & ((row & (2 * m - 1)) >= m) & ((col & (2 * m - 1)) < m))
        for h in range(H):
            hs = slice(h * dk, (h + 1) * dk)
            p = lax.dot_general(qe[:, hs], ke[:, hs], _NT, preferred_element_type=F32)
            scores[h] = jnp.where(mask, p, 0.0 if scores[h] is None else scores[h])

    b_end = b_scr[C - 1:C, ls]
    qi = (q * jnp.exp(b)).astype(BF16)
    kend = (k * jnp.exp(b_end - b)).astype(BF16)
    a_end = jnp.exp(b_end)
    for h, head in enumerate(heads):
        hs = slice(h * dk, (h + 1) * dk)
        vs = slice(head * dv, (head + 1) * dv)
        v_h = v_ref[:, vs]
        st = st_scr[head]
        o = lax.dot_general(qi[:, hs], st.astype(BF16), _NT, preferred_element_type=F32)
        o = o + jnp.dot(scores[h].astype(BF16), v_h, preferred_element_type=F32)
        st_scr[head] = st * a_end[:, hs] + jnp.dot(v_h.astype(F32).T.astype(BF16), kend[:, hs],
                                                   preferred_element_type=F32)
        o_scr[:, vs] = _head_norm_gate(o, nw, r_ref[:, vs].astype(F32)).astype(o_scr.dtype)


def _pool_chunk(u, pw_ref, ps_ref, p_scr, buf, *, C, gc, chunk):
    PB = POOL_PAST + 1
    buf[PB:PB + C, :] = u
    pos = chunk * C + lax.broadcasted_iota(jnp.int32, (C, 1), 0)
    for g, w in enumerate(POOL_WINDOWS):
        cs = slice(g * gc, (g + 1) * gc)
        cur = u[:, cs]
        win = cur
        for i in range(1, w):
            win = win + buf[PB - i:PB - i + C, cs]
        cnt = jnp.minimum(pos + 1, w).astype(F32)
        d = win / cnt - cur
        y = jnp.dot(d.astype(BF16), pw_ref[g], preferred_element_type=F32) * ps_ref[:, cs]
        p_scr[:, cs] = y.astype(p_scr.dtype)
    buf[0:PB, :] = buf[C:C + PB, :]


def _ln_silu(x, g, b):
    mu = jnp.mean(x, axis=-1, keepdims=True)
    xc = x - mu
    var = jnp.mean(xc * xc, axis=-1, keepdims=True)
    return _silu(xc * lax.rsqrt(var + EPS) * g + b)


def _conv_chunk(cv_in, cw_ref, cb_ref, lg_ref, lb_ref, p_scr, buf, ph, *, C):
    HB = CONV_HIST
    S = SUBLANES
    rt = CONV_ROWS
    buf[HB:HB + C, :] = cv_in
    for p in range(1, S):
        ph[p - 1, S:HB + C, :] = buf[S - p:HB + C - p, :]
    n_sub = rt // S
    for t in range(C // rt):
        accs = [None] * n_sub
        for back in range(CONV_W):
            a, p = divmod(back, S)
            j = CONV_PAST - back
            wt = cw_ref[j * S:(j + 1) * S, :]
            for q in range(n_sub):
                r0 = HB + t * rt + q * S - a * S
                src = buf[r0:r0 + S, :] if p == 0 else ph[p - 1, r0:r0 + S, :]
                term = src * wt
                accs[q] = term if accs[q] is None else accs[q] + term
        acc = jnp.concatenate(accs, axis=0)
        y = _ln_silu(acc + cb_ref[...], lg_ref[...], lb_ref[...])
        p_scr[t * rt:(t + 1) * rt, :] = y.astype(p_scr.dtype)
    buf[0:HB, :] = buf[C:C + HB, :]


def _mixer_kernel(*refs, mode, H, dk, dv, C, n_sub, layer, gc, u_off, cols, widths):
    if mode == "gla":
        (z_ref, wg_ref, bg_ref, nw_ref, pw_ref, ps_ref,
         wa_ref, wb_ref, x_ref, g1_ref, xo_ref, s_ref, sh_ref,
         st_scr, b_scr, o_scr, p_scr, buf) = refs
        hist0, hist_rows = POOL_PAST + 1, POOL_PAST
    else:
        (z_ref, zg_ref, lbraw_ref, nw_ref, cw_ref, cb_ref, lg_ref, lb_ref,
         wa_ref, wb_ref, x_ref, g1_ref, xo_ref, s_ref, sh_ref,
         st_scr, b_scr, o_scr, p_scr, buf, ph) = refs
        hist0, hist_rows = CONV_HIST, CONV_PAST
    c = pl.program_id(1)

    @pl.when(c == 0)
    def _():
        st_scr[...] = jnp.zeros_like(st_scr)
        buf[0:hist0, :] = jnp.zeros((hist0, buf.shape[1]), F32)

    for sub in range(n_sub):
        rows = slice(sub * C, (sub + 1) * C)
        zc = [z_ref.at[rows, s:s + w] for s, w in zip(cols, widths)]
        if mode == "gla":
            q_ref, k_ref, v_ref, r_ref, alr_ref, u_ref = zc
            u = _shift_lanes(u_ref[...].astype(F32), u_off, p_scr.shape[1])
            _pool_chunk(u, pw_ref, ps_ref, p_scr, buf, C=C, gc=gc, chunk=c * n_sub + sub)
        else:
            q_ref, _, v_ref, r_ref, ga_ref, gb_ref = zc
            fz_ref = zg_ref.at[rows, :]
            cv_in = ga_ref[...].astype(F32) * _sigmoid(gb_ref[...].astype(F32))
            _conv_chunk(cv_in, cw_ref, cb_ref, lg_ref, lb_ref, p_scr, buf, ph, C=C)
        y = jnp.dot(p_scr[...], wb_ref[...], preferred_element_type=F32)
        n_grp = 1
        hg = H // n_grp
        for grp in range(n_grp):
            heads = tuple(range(grp * hg, (grp + 1) * hg))
            ls = slice(heads[0] * dk, (heads[-1] + 1) * dk)
            vs = slice(heads[0] * dv, (heads[-1] + 1) * dv)
            if mode == "gla":
                q = q_ref[:, ls].astype(F32) * (dk ** -0.5)
                k = k_ref[:, ls].astype(F32)
                g = _gla_log_decay(alr_ref[...], wg_ref[:, ls], bg_ref[:, ls])
            else:
                q = q_ref[:, ls].astype(F32)
                g, k = _hgrn_gates(fz_ref[:, ls], _hgrn_lower_bound(lbraw_ref, layer)[:, ls])
            _rec_chunk(q, k, g, v_ref, r_ref, nw_ref[...], o_scr, st_scr, b_scr,
                       heads=heads, dk=dk, dv=dv, C=C)
            y = y + jnp.dot(o_scr[:, vs], wa_ref[vs, :], preferred_element_type=F32)
        xo_ref[rows, :] = x_ref[rows, :] + g1_ref[...] * y

    @pl.when(c == pl.num_programs(1) - 1)
    def _():
        for h in range(H):
            s_ref[h] = st_scr[h].T
        sh_ref[...] = buf[C + hist0 - hist_rows:C + hist0, :]


def _mixer_prompt(mode, z, zg, cols, widths, consts, w_out, x, mod, layer, el, *, H, dk, dv, hist_len, u_off=0):
    B, T, _ = z.shape
    D = x.shape[-1]
    C = REC_CHUNK
    V = H * dv
    PW = D - V
    assert T % C == 0
    n_sub = 2 if T % (2 * C) == 0 else 1
    R = C * n_sub

    in_specs = [pl.BlockSpec((None, R, z.shape[-1]), lambda b, c: (b, c, 0))]
    args = [z]
    if mode == "hgrn":
        in_specs.append(pl.BlockSpec((None, R, zg.shape[-1]), lambda b, c: (b, c, 0)))
        args.append(zg)
    for a in consts:
        in_specs.append(_layer_spec(a, el) if a.ndim >= 3 else _const_spec(a))
        args.append(a)
    in_specs += [pl.BlockSpec((V, D), lambda b, c: (0, 0)),
                 pl.BlockSpec((PW, D), lambda b, c: (V // PW, 0)),
                 pl.BlockSpec((None, R, D), lambda b, c: (b, c, 0)),
                 pl.BlockSpec((None, None, 1, D), lambda b, c: (layer, b, 0, 2))]
    args += [w_out, w_out, x, mod]
    scratch = [pltpu.VMEM((H, dv, dk), F32), pltpu.VMEM((C, H * dk), F32),
               pltpu.VMEM((C, V), BF16), pltpu.VMEM((C, PW), BF16)]
    if mode == "gla":
        scratch.append(pltpu.VMEM((POOL_PAST + 1 + C, PW), F32))
    else:
        scratch += [pltpu.VMEM((CONV_HIST + C, PW), F32),
                    pltpu.VMEM((SUBLANES - 1, CONV_HIST + C, PW), F32)]
    return pl.pallas_call(
        functools.partial(_mixer_kernel, mode=mode, H=H, dk=dk, dv=dv, C=C, n_sub=n_sub, layer=el,
                          gc=PW // len(POOL_WINDOWS), u_off=u_off, cols=cols, widths=widths),
        grid=(B, T // R),
        in_specs=in_specs,
        out_specs=[pl.BlockSpec((None, R, D), lambda b, c: (b, c, 0)),
                   pl.BlockSpec((None, H, dk, dv), lambda b, c: (b, 0, 0, 0)),
                   pl.BlockSpec((None, hist_len, PW), lambda b, c: (b, 0, 0))],
        out_shape=[jax.ShapeDtypeStruct((B, T, D), F32),
                   jax.ShapeDtypeStruct((B, H, dk, dv), F32),
                   jax.ShapeDtypeStruct((B, hist_len, PW), F32)],
        scratch_shapes=scratch,
        compiler_params=_params("parallel", "arbitrary"),
        name="mixer_" + mode,
    )(*args)


def _step_recurrence(a, k, q, v_ref, s_ref, so_ref, tile_scr, row_scr, *, H, dk, dv, bb):
    for h in range(H):
        hs = slice(h * dk, (h + 1) * dk)
        tile_scr[...] = jnp.zeros_like(tile_scr)
        tile_scr[0:bb, :] = a[:, hs]
        tile_scr[bb:2 * bb, :] = k[:, hs]
        tile_scr[2 * bb:3 * bb, :] = q[:, hs]
        cols = tile_scr[...].T
        for s in range(bb):
            a_col = cols[:, s:s + 1]
            k_col = cols[:, bb + s:bb + s + 1]
            q_col = cols[:, 2 * bb + s:2 * bb + s + 1]
            v_row = v_ref[s:s + 1, h * dv:(h + 1) * dv]
            s_new = a_col * s_ref[s, h] + k_col * v_row
            so_ref[s, h] = s_new
            row_scr[s:s + 1, h * dv:(h + 1) * dv] = jnp.sum(q_col * s_new, axis=0, keepdims=True)


def _own_slab(ref, el, n_alias):
    if n_alias:
        return ref
    for j in range(ref.shape[0]):
        if j != el:
            ref[j] = jnp.zeros(ref.shape[1:], ref.dtype)
    return ref.at[el]


def _even1_kernel(*refs, H, dk, dv, bb, gc, n_alias, u_off, el, cols, widths):
    z_ref, sg_ref, sp_ref, wg_ref, bg_ref, nw_ref, pw_ref, ps_ref = refs[:8]
    q_ref, k_ref, v_ref, r_ref, alr_ref, ub_ref = [z_ref.at[:, s:s + w] for s, w in zip(cols, widths)]
    o_ref, p_ref, sgo_ref, spo_ref, tile_scr, row_scr, d_scr, u_scr = refs[8 + n_alias:]
    sgo_ref, spo_ref = _own_slab(sgo_ref, el, n_alias), _own_slab(spo_ref, el, n_alias)
    g = _gla_log_decay(alr_ref[...], wg_ref[...], bg_ref[...])
    _step_recurrence(jnp.exp(g), k_ref[...], q_ref[...] * (dk ** -0.5), v_ref, sg_ref, sgo_ref,
                     tile_scr, row_scr, H=H, dk=dk, dv=dv, bb=bb)
    nw = nw_ref[...]
    for h in range(H):
        vs = slice(h * dv, (h + 1) * dv)
        o_ref[:, vs] = _head_norm_gate(row_scr[:, vs], nw, r_ref[:, vs]).astype(o_ref.dtype)

    u_scr[...] = _shift_lanes(ub_ref[...], u_off, u_scr.shape[1])
    for s in range(bb):
        for g_i, w in enumerate(POOL_WINDOWS):
            cs = slice(g_i * gc, (g_i + 1) * gc)
            cur = u_scr[s:s + 1, cs]
            win = cur + jnp.sum(sp_ref[s, POOL_PAST - (w - 1):POOL_PAST, cs], axis=0, keepdims=True)
            cnt = float(min(PAST_LEN + 1, w))
            d_scr[s:s + 1, cs] = win / cnt - cur
        spo_ref[s, 0:POOL_PAST - 1, :] = sp_ref[s, 1:POOL_PAST, :]
        spo_ref[s, POOL_PAST - 1:POOL_PAST, :] = u_scr[s:s + 1, :]
    for g_i in range(len(POOL_WINDOWS)):
        cs = slice(g_i * gc, (g_i + 1) * gc)
        y = jnp.dot(d_scr[:, cs].astype(BF16), pw_ref[g_i], preferred_element_type=F32) * ps_ref[:, cs]
        p_ref[:, cs] = y.astype(p_ref.dtype)


def _odd1_kernel(*refs, H, dk, dv, bb, n_alias, el, cols, widths):
    z_ref, sh_ref, sc_ref, lbraw_ref, nw_ref, cw_ref, cb_ref, lg_ref, lb_ref = refs[:9]
    q_ref, fz_ref, v_ref, r_ref, ga_ref, gb_ref = [z_ref.at[:, s:s + w] for s, w in zip(cols, widths)]
    o_ref, cv_ref, sho_ref, sco_ref, tile_scr, row_scr, d_scr = refs[9 + n_alias:]
    sho_ref, sco_ref = _own_slab(sho_ref, el, n_alias), _own_slab(sco_ref, el, n_alias)
    g, k = _hgrn_gates(fz_ref[...], _hgrn_lower_bound(lbraw_ref, el))
    _step_recurrence(jnp.exp(g), k, q_ref[...], v_ref, sh_ref, sho_ref,
                     tile_scr, row_scr, H=H, dk=dk, dv=dv, bb=bb)
    nw = nw_ref[...]
    for h in range(H):
        vs = slice(h * dv, (h + 1) * dv)
        o_ref[:, vs] = _head_norm_gate(row_scr[:, vs], nw, r_ref[:, vs]).astype(o_ref.dtype)

    cv_in = ga_ref[...] * _sigmoid(gb_ref[...])
    d_scr[...] = cv_in * cw_ref[CONV_PAST:CONV_W, :]
    for s in range(bb):
        hist = sc_ref[s]
        d_scr[s:s + 1, :] += jnp.sum(hist * cw_ref[0:CONV_PAST, :], axis=0, keepdims=True)
        sco_ref[s, 0:CONV_PAST - 1, :] = sc_ref[s, 1:CONV_PAST, :]
        sco_ref[s, CONV_PAST - 1:CONV_PAST, :] = cv_in[s:s + 1, :]
    y = _ln_silu(d_scr[...] + cb_ref[...], lg_ref[...], lb_ref[...])
    cv_ref[...] = y.astype(cv_ref.dtype)


def _sample_mixer(kernel_fn, name, z, cols, widths, states, prev, consts, el, *, bb, extra_scratch=()):
    Bs = z.shape[0]
    s_mat, s_hist = states
    _, _, H, dk, dv = s_mat.shape
    hist_len, HW = s_hist.shape[2:]
    V = H * dv

    mat_spec = pl.BlockSpec((None, bb, H, dk, dv), lambda i: (el, i, 0, 0, 0))
    hist_spec = pl.BlockSpec((None, bb, hist_len, HW), lambda i: (el, i, 0, 0))
    in_specs = [pl.BlockSpec((bb, z.shape[1]), lambda i: (i, 0)), mat_spec, hist_spec]
    args = [z, s_mat, s_hist]
    for a in consts:
        in_specs.append(_layer_spec(a, el) if a.ndim >= 3 else _const_spec(a))
        args.append(a)
    aliases = {}
    if prev is not None:
        for n, p in enumerate(prev):
            aliases[len(args)] = 2 + n
            in_specs.append(pl.BlockSpec(memory_space=pl.ANY))
            args.append(p)
    n_alias = 0 if prev is None else len(prev)
    if prev is None:
        n_lay = s_mat.shape[0]
        mat_out = pl.BlockSpec((n_lay, bb, H, dk, dv), lambda i: (0, i, 0, 0, 0))
        hist_out = pl.BlockSpec((n_lay, bb, hist_len, HW), lambda i: (0, i, 0, 0))
    else:
        mat_out, hist_out = mat_spec, hist_spec
    return pl.pallas_call(
        functools.partial(kernel_fn, H=H, dk=dk, dv=dv, bb=bb, n_alias=n_alias, el=el,
                          cols=cols, widths=widths),
        grid=(Bs // bb,),
        in_specs=in_specs,
        out_specs=[pl.BlockSpec((bb, V), lambda i: (i, 0)),
                   pl.BlockSpec((bb, HW), lambda i: (i, 0)),
                   mat_out, hist_out],
        out_shape=[jax.ShapeDtypeStruct((Bs, V), BF16), jax.ShapeDtypeStruct((Bs, HW), BF16),
                   jax.ShapeDtypeStruct(s_mat.shape, F32), jax.ShapeDtypeStruct(s_hist.shape, F32)],
        scratch_shapes=[pltpu.VMEM((LANES, dk), F32), pltpu.VMEM((bb, V), F32),
                        pltpu.VMEM((bb, HW), F32)] + list(extra_scratch),
        input_output_aliases=aliases,
        compiler_params=_params("parallel"),
        name=name,
    )(*args)


def _pick(m, pref):
    return pref if m % pref == 0 else m


def kernel(x_prompt, x_sample, c_prompt, c_sample, state_gla, state_pool, state_hgrn, state_conv,
           w_ada, b_ada, norm_mix, norm_mlp, norm_final,
           ev_w_in, ev_w_gate_up, ev_b_gate, ev_gla_norm, ev_pool_w, ev_pool_scale, ev_w_out,
           od_w_in, od_lb_raw, od_hgrn_norm, od_conv_w, od_conv_b, od_ln_g, od_ln_b, od_w_out,
           mlp_w_up, mlp_w_down):
    B, T, D = x_prompt.shape
    Bs = x_sample.shape[0]
    depth = w_ada.shape[0]
    gla_w = D // 2
    gla_key_w = gla_w // 2
    gla_dk = gla_key_w // GLA_HEADS
    gla_dv = gla_w // GLA_HEADS
    pool_w = D - gla_w
    hgrn_w = D // 2
    hgrn_heads = hgrn_w // HGRN_DK
    hgrn_dv = hgrn_w // hgrn_heads
    conf_w = D - hgrn_w

    n_main = 2 * gla_key_w + 2 * gla_w
    n_ev = -(-ev_w_in.shape[-1] // LANES) * LANES
    ub_w = n_ev - n_main
    tn_ev = n_ev // 3
    assert GLA_GATE_RANK + pool_w <= ub_w and tn_ev * 3 == n_ev and tn_ev % LANES == 0
    ev_w = jnp.pad(ev_w_in, ((0, 0), (0, 0), (0, n_ev - ev_w_in.shape[-1])))
    ev_cols = (0, gla_key_w, 2 * gla_key_w, 2 * gla_key_w + gla_w, n_main, n_main)
    ev_widths = (gla_key_w, gla_key_w, gla_w, gla_w, LANES, ub_w)
    wg = jnp.pad(ev_w_gate_up, ((0, 0), (0, LANES - GLA_GATE_RANK), (0, 0))).astype(BF16)
    od_cols = (0, hgrn_w, 2 * hgrn_w, 3 * hgrn_w, 4 * hgrn_w, 4 * hgrn_w + conf_w)
    od_widths = (hgrn_w, hgrn_w, hgrn_w, hgrn_w, conf_w, conf_w)
    pool_wb = ev_pool_w.astype(BF16)
    conv_wr = jnp.repeat(od_conv_w, SUBLANES, axis=1)
    conv_wp = jnp.pad(od_conv_w, ((0, 0), (0, 1), (0, 0)))

    lay3 = lambda a: a.reshape(a.shape[0], 1, a.shape[1])
    norm_mix3, norm_mlp3 = lay3(norm_mix), lay3(norm_mlp)
    bg3, gn3, ps3 = lay3(ev_b_gate), lay3(ev_gla_norm), lay3(ev_pool_scale)
    hn3, cb3, lg3, lb3 = lay3(od_hgrn_norm), lay3(od_conv_b), lay3(od_ln_g), lay3(od_ln_b)

    pad = (-(B + Bs)) % 16
    c_all = jnp.concatenate([c_sample, c_prompt, jnp.zeros((pad, D), F32)], axis=0)
    mod_s, mod_p = _ada(c_all, w_ada, b_ada, Bs, B)

    xp = x_prompt.reshape(B * T, D)
    xs = x_sample.reshape(Bs, D)
    Mp = B * T
    tm_in = _pick(Mp, 1024)
    tm_p = _pick(Mp, 512)
    bb = 8

    new = {k: [] for k in ("gla_p", "pool_p", "hgrn_p", "conv_p")}
    ev_prev = od_prev = cast = None
    for l in range(depth):
        e = l // 2
        nf = norm_final.reshape(1, D) if l == depth - 1 else None
        kw_p = dict(per_row=False, seq_len=T)
        kw_s = dict(per_row=True, seq_len=1)
        w_in_f, w_out_f = (ev_w, ev_w_out) if l % 2 == 0 else (od_w_in, od_w_out)
        if cast is None:
            w_in_s, w_out_s, w_up_s, w_down_s, wl = w_in_f, w_out_f, mlp_w_up, mlp_w_down, e
        else:
            (w_up_s, w_down_s, w_in_s, w_out_s), wl = cast, None
        if l % 2 == 0:
            n_in = ev_w.shape[-1]
            tn = tn_ev
            zs, *w_in_b = _in_proj(xs, norm_mix3, mod_s, w_in_s, l, wl, tm=Bs, tn=tn, out_dtype=F32, **kw_s)
            a_s, b_s, sgs, sps = _sample_mixer(
                functools.partial(_even1_kernel, gc=pool_w // len(POOL_WINDOWS), u_off=GLA_GATE_RANK),
                "even_sample", zs, ev_cols, ev_widths, (state_gla, state_pool), ev_prev,
                (wg, bg3, gn3, pool_wb, ps3), e, bb=bb, extra_scratch=[pltpu.VMEM((bb, pool_w), F32)])
            ev_prev = (sgs, sps)
        else:
            n_in = od_w_in.shape[-1]
            tn = _pick(n_in, 1024)
            zs, *w_in_b = _in_proj(xs, norm_mix3, mod_s, w_in_s, l, wl, tm=Bs, tn=tn, out_dtype=F32, **kw_s)
            a_s, b_s, shs, scs = _sample_mixer(
                _odd1_kernel, "odd_sample",
                zs, od_cols, od_widths, (state_hgrn, state_conv), od_prev,
                (od_lb_raw, hn3, conv_wp, cb3, lg3, lb3), e, bb=bb)
            od_prev = (shs, scs)
        xs, *w_out_b = _out_proj_sample(a_s, b_s, w_out_s, wl, xs, mod_s, l, tn=_pick(D, 512))
        xs, *w_mlp_b = _mlp(xs, norm_mlp3, mod_s, w_up_s, w_down_s, l, nf, tm=Bs,
                            tf=512 if cast is None else 1024, **kw_s)
        if cast is None:
            (w_in_b,), (w_out_b,), (w_up_b, w_down_b) = w_in_b, w_out_b, w_mlp_b
        else:
            w_up_b, w_down_b, w_in_b, w_out_b = cast

        xp3 = xp.reshape(B, T, D)
        if l % 2 == 0:
            (zp,) = _in_proj(xp, norm_mix3, mod_p, w_in_b, l, None, tm=tm_in, tn=tn, out_dtype=BF16, **kw_p)
            xp3, sg, sp = _mixer_prompt("gla", zp.reshape(B, T, n_in), None, ev_cols, ev_widths,
                                        (wg, bg3, gn3, pool_wb, ps3), w_out_b, xp3, mod_p, l, e,
                                        H=GLA_HEADS, dk=gla_dk, dv=gla_dv, hist_len=POOL_PAST,
                                        u_off=GLA_GATE_RANK)
            new["gla_p"].append(sg)
            new["pool_p"].append(sp)
        else:
            zp, zg = _in_proj(xp, norm_mix3, mod_p, w_in_b, l, None, tm=tm_in, tn=tn, out_dtype=BF16,
                              gate_block=od_cols[1] // tn, **kw_p)
            xp3, shg, scv = _mixer_prompt("hgrn", zp.reshape(B, T, n_in), zg.reshape(B, T, tn),
                                          od_cols, od_widths, (od_lb_raw, hn3, conv_wr, cb3, lg3, lb3),
                                          w_out_b, xp3, mod_p, l, e,
                                          H=hgrn_heads, dk=HGRN_DK, dv=hgrn_dv, hist_len=CONV_PAST)
            new["hgrn_p"].append(shg)
            new["conv_p"].append(scv)
        nxt = ()
        if l + 1 < depth:
            e_n = (l + 1) // 2
            w_in_n, w_out_n = (ev_w, ev_w_out) if (l + 1) % 2 == 0 else (od_w_in, od_w_out)
            nxt = ((mlp_w_up, l + 1), (mlp_w_down, l + 1), (w_in_n, e_n), (w_out_n, e_n))
        xp, *cast = _mlp(xp3.reshape(Mp, D), norm_mlp3, mod_p, w_up_b, w_down_b, l, nf,
                         tm=tm_p, tf=1024, cast_next=nxt, **kw_p)
        cast = cast or None

    st = lambda k: jnp.stack(new[k])
    return (xp.reshape(B, T, D), xs.reshape(Bs, 1, D),
            st("gla_p"), st("pool_p"), st("hgrn_p"), st("conv_p"),
            ev_prev[0], ev_prev[1], od_prev[0], od_prev[1])
```
